```python
import math
import jax, jax.numpy as jnp
from jax import lax
import numpy as np

D_MODEL = 1024
BATCH = 8
SEQ = 2048
DEPTH = 4
DEC_BATCH = 128
DEC_SEQ = 4
PAST_LEN = 16384
PAGE_SIZE = 128

N_BRANCH = 4
C_BR = D_MODEL // 4
RW_HEAD = 64
RW_HEADS = C_BR // RW_HEAD
RW_LW = 64
RW_LA = 64
RW_LG = 128
RW_IN = 3 * C_BR + RW_LW + RW_LA + RW_LG
RW_GN_EPS = 64e-5
S5_GW = 16
S5_GROUPS = C_BR // S5_GW
S5_STATE = 64
CONV_W = 31
CHUNK = 128
GM_HEADS = 4
GM_HEAD = C_BR // GM_HEADS
N_GROUPS = 4
E_PER_GROUP = 4
N_EXPERTS = N_GROUPS * E_PER_GROUP
TOP_K_INNER = 2
D_EXPERT = D_MODEL // 4
LN_EPS = 1e-5
DN_ALPHA = (2 * DEPTH) ** 0.25
DN_BETA = (8 * DEPTH) ** -0.25
OFF_S5 = RW_IN
OFF_CV = OFF_S5 + C_BR
OFF_GM = OFF_CV + 2 * C_BR
OFF_GATE = OFF_GM + 2 * C_BR
N_IN = OFF_GATE + N_BRANCH * D_MODEL

kernel_name = 'hybrid_rwkv7_s5_conv_gmlp_hmoe_step'


def _layer_norm(x, g, b, eps=LN_EPS):
    xf = x.astype(jnp.float32)
    mu = jnp.mean(xf, axis=-1, keepdims=True)
    var = jnp.mean(jnp.square(xf - mu), axis=-1, keepdims=True)
    return ((xf - mu) * lax.rsqrt(var + eps)).astype(x.dtype) * g + b


def _rwkv7_time_mix(z, shift0, wkv0, mu, w0, w2, a0, a2, g2, k_k, k_a, r_k, gn_g, gn_b):
    bsz, t_len, _ = z.shape
    f32 = jnp.float32
    z_prev = jnp.concatenate([shift0[:, None, :].astype(z.dtype), z[:, :-1]], axis=1)
    zs = z + mu * (z_prev - z)
    r, k, v, lw, la, lg = jnp.split(
        zs, [C_BR, 2 * C_BR, 3 * C_BR, 3 * C_BR + RW_LW, 3 * C_BR + RW_LW + RW_LA], axis=-1)
    w_log = -jax.nn.softplus(-(w0 + jnp.tanh(lw) @ w2)) - 0.5
    decay = jnp.exp(-jnp.exp(w_log.astype(f32)))
    a = jax.nn.sigmoid(a0 + la @ a2)
    g = jax.nn.sigmoid(lg) @ g2

    def heads(u):
        return u.reshape(bsz, t_len, RW_HEADS, RW_HEAD).astype(f32)

    kk = heads(k * k_k)
    kk = kk * lax.rsqrt(jnp.maximum(jnp.sum(kk * kk, axis=-1, keepdims=True), 1e-24))
    k = k * (1 + (a - 1) * k_a)
    rh, kh, vh, ah = heads(r), heads(k), heads(v), heads(a)

    def step(s, inp):
        r_t, w_t, k_t, v_t, kk_t, a_t = inp
        s_kk = jnp.einsum('bhvk,bhk->bhv', s, kk_t)
        s = (s * w_t[:, :, None, :]
             - s_kk[..., None] * (kk_t * a_t)[:, :, None, :]
             + v_t[..., None] * k_t[:, :, None, :])
        return s, jnp.einsum('bhvk,bhk->bhv', s, r_t)

    xs = tuple(jnp.moveaxis(u, 1, 0) for u in (rh, heads(decay), kh, vh, kk, ah))
    s_final, o = lax.scan(step, wkv0.astype(f32), xs)
    o = jnp.moveaxis(o, 0, 1)
    o_mu = jnp.mean(o, axis=-1, keepdims=True)
    o_var = jnp.mean(jnp.square(o - o_mu), axis=-1, keepdims=True)
    o = ((o - o_mu) * lax.rsqrt(o_var + RW_GN_EPS)).reshape(bsz, t_len, C_BR) * gn_g + gn_b
    bonus = jnp.sum(rh * kh * r_k.astype(f32), axis=-1, keepdims=True) * vh
    out = (o + bonus.reshape(bsz, t_len, C_BR)) * g
    return out.astype(z.dtype), z[:, -1], s_final.astype(z.dtype)


def _complex_affine_combine(e1, e2):
    a1r, a1i, b1r, b1i = e1
    a2r, a2i, b2r, b2i = e2
    return (a2r * a1r - a2i * a1i,
            a2r * a1i + a2i * a1r,
            a2r * b1r - a2i * b1i + b2r,
            a2r * b1i + a2i * b1r + b2i)


def _s5_ssm(u, h0_re, h0_im, lam_re, lam_im, log_dt, b_re, b_im, c_re, c_im, d, glu_w, glu_b):
    bsz, t_len, _ = u.shape
    f32 = jnp.float32
    ug = u.reshape(bsz, t_len, S5_GROUPS, S5_GW).astype(f32)
    lr, li = lam_re.astype(f32), lam_im.astype(f32)
    dt = jnp.exp(log_dt.astype(f32))[:, None]
    mag = jnp.exp(lr * dt)
    lb_re, lb_im = mag * jnp.cos(li * dt), mag * jnp.sin(li * dt)
    den = lr * lr + li * li
    q_re = ((lb_re - 1.0) * lr + lb_im * li) / den
    q_im = (lb_im * lr - (lb_re - 1.0) * li) / den
    br, bi = b_re.astype(f32), b_im.astype(f32)
    bb_re = q_re[..., None] * br - q_im[..., None] * bi
    bb_im = q_re[..., None] * bi + q_im[..., None] * br
    bu_re = jnp.einsum('gpc,btgc->btgp', bb_re, ug)
    bu_im = jnp.einsum('gpc,btgc->btgp', bb_im, ug)
    hr0, hi0 = h0_re.astype(f32), h0_im.astype(f32)
    bu_re = bu_re.at[:, 0].add(lb_re * hr0 - lb_im * hi0)
    bu_im = bu_im.at[:, 0].add(lb_re * hi0 + lb_im * hr0)
    a_re = jnp.broadcast_to(lb_re, bu_re.shape)
    a_im = jnp.broadcast_to(lb_im, bu_im.shape)
    _, _, h_re, h_im = lax.associative_scan(_complex_affine_combine, (a_re, a_im, bu_re, bu_im), axis=1)
    y = (jnp.einsum('gcp,btgp->btgc', c_re.astype(f32), h_re)
         - jnp.einsum('gcp,btgp->btgc', c_im.astype(f32), h_im)
         + d.astype(f32) * ug)
    y = jax.nn.gelu(y.reshape(bsz, t_len, C_BR)).astype(u.dtype)
    y = y * jax.nn.sigmoid(y @ glu_w + glu_b)
    return y, h_re[:, -1].astype(u.dtype), h_im[:, -1].astype(u.dtype)


def _conformer_conv(z, buf, w, b, ln_g, ln_b):
    c = z[..., :C_BR] * jax.nn.sigmoid(z[..., C_BR:])
    full = jnp.concatenate([buf.astype(c.dtype), c], axis=1)
    y = lax.conv_general_dilated(full, w[:, None, :].astype(c.dtype), (1,), 'VALID',
                                 dimension_numbers=('NWC', 'WIO', 'NWC'),
                                 feature_group_count=C_BR) + b
    y = jax.nn.silu(_layer_norm(y, ln_g, ln_b))
    return y, full[:, -(CONV_W - 1):]


def _chunk_gmlp(z, ln_g, ln_b, ws, bs):
    bsz, t_len, _ = z.shape
    u, v = z[..., :C_BR], z[..., C_BR:]
    v = _layer_norm(v, ln_g, ln_b)
    n_chunks = -(-t_len // CHUNK)
    pad = n_chunks * CHUNK - t_len
    vc = jnp.pad(v, ((0, 0), (0, pad), (0, 0))).reshape(bsz, n_chunks, CHUNK, GM_HEADS, GM_HEAD)
    causal = jnp.tril(jnp.ones((CHUNK, CHUNK), dtype=bool))
    wm = jnp.where(causal, ws, 0)
    s = jnp.einsum('hij,bnjhd->bnihd', wm, vc) + jnp.swapaxes(bs, 0, 1)[None, None, :, :, None]
    s = s.reshape(bsz, n_chunks * CHUNK, C_BR)[:, :t_len]
    return u * s, v


def _hier_moe(x, wg1, bg1, wg2, bg2, w_up, w_down):
    shp = x.shape
    t = x.reshape(-1, D_MODEL)
    n_tok = t.shape[0]
    f32 = jnp.float32
    lg_group = (t @ wg1 + bg1).astype(f32)
    g_sel = jnp.argmax(lg_group, axis=-1)
    p_group = jnp.take_along_axis(jax.nn.softmax(lg_group, axis=-1), g_sel[:, None], axis=-1)
    lg_exp = (t @ wg2 + bg2).astype(f32).reshape(n_tok, N_GROUPS, E_PER_GROUP)
    idx = jnp.broadcast_to(g_sel[:, None, None], (n_tok, 1, E_PER_GROUP))
    lg_exp = jnp.take_along_axis(lg_exp, idx, axis=1)[:, 0]
    top_v, top_i = lax.top_k(lg_exp, TOP_K_INNER)
    wts = jax.nn.softmax(top_v, axis=-1) * p_group
    e_idx = g_sel[:, None] * E_PER_GROUP + top_i
    comb = jnp.sum(jax.nn.one_hot(e_idx, N_EXPERTS, dtype=f32) * wts[..., None], axis=1)
    h = jnp.einsum('nd,edf->nef', t, w_up)
    h = jax.nn.silu(h[..., :D_EXPERT]) * h[..., D_EXPERT:] * comb[..., None].astype(t.dtype)
    return jnp.einsum('nef,efd->nd', h, w_down).reshape(shp)


def _trunk_layer(x, l, p, wkv0, shift0, s5r0, s5i0, conv0):
    bsz, t_len, _ = x.shape
    z = x @ p['w_in'][l]
    y_rw, shift1, wkv1 = _rwkv7_time_mix(
        z[..., :OFF_S5], shift0, wkv0, p['rw_mu'][l], p['rw_w0'][l], p['rw_w2'][l], p['rw_a0'][l],
        p['rw_a2'][l], p['rw_g2'][l], p['rw_kk'][l], p['rw_ka'][l], p['rw_rk'][l],
        p['rw_gn_g'][l], p['rw_gn_b'][l])
    y_s5, s5r1, s5i1 = _s5_ssm(
        z[..., OFF_S5:OFF_CV], s5r0, s5i0, p['s5_lam_re'][l], p['s5_lam_im'][l], p['s5_log_dt'][l],
        p['s5_b_re'][l], p['s5_b_im'][l], p['s5_c_re'][l], p['s5_c_im'][l], p['s5_d'][l],
        p['s5_glu_w'][l], p['s5_glu_b'][l])
    y_cv, conv1 = _conformer_conv(z[..., OFF_CV:OFF_GM], conv0, p['cv_w'][l], p['cv_b'][l],
                                  p['cv_ln_g'][l], p['cv_ln_b'][l])
    y_gm, v_gm = _chunk_gmlp(z[..., OFF_GM:OFF_GATE], p['gm_ln_g'][l], p['gm_ln_b'][l],
                             p['gm_ws'][l], p['gm_bs'][l])
    gates = jax.nn.sigmoid(z[..., OFF_GATE:]).reshape(bsz, t_len, N_BRANCH, D_MODEL)
    merged = jnp.zeros_like(x)
    for i, y_b in enumerate((y_rw, y_s5, y_cv, y_gm)):
        merged = merged + gates[:, :, i] * (y_b @ p['w_branch'][l, i])
    x = _layer_norm(DN_ALPHA * x + merged @ p['w_out'][l], p['ln1_g'][l], p['ln1_b'][l])
    moe = _hier_moe(x, p['moe_wg1'][l], p['moe_bg1'][l], p['moe_wg2'][l], p['moe_bg2'][l],
                    p['moe_w_up'][l], p['moe_w_down'][l])
    x = _layer_norm(DN_ALPHA * x + moe, p['ln2_g'][l], p['ln2_b'][l])
    return x, (wkv1, shift1, s5r1, s5i1, conv1, v_gm)


def _run_trunk(x, wkv0, shift0, s5r0, s5i0, conv0, p):
    per_layer = []
    for l in range(DEPTH):
        x, st = _trunk_layer(x, l, p, wkv0[:, l], shift0[:, l], s5r0[:, l], s5i0[:, l], conv0[:, l])
        per_layer.append(st)
    states = tuple(jnp.stack([st[i] for st in per_layer], axis=1) for i in range(6))
    return x, states


def setup_inputs(seed: int = 0) -> dict:
    key = jax.random.key(seed)
    keys = iter(jax.random.split(key, 64))
    f32 = jnp.float32

    def nrm(shape, scale):
        return jax.random.normal(next(keys), shape, f32) * scale

    def unif(shape, lo, hi):
        return jax.random.uniform(next(keys), shape, f32, lo, hi)

    L = DEPTH
    modes = math.pi * jnp.arange(S5_STATE, dtype=f32)
    return {
        'x_prompt': nrm((BATCH, SEQ, D_MODEL), 1.0),
        'x_sample': nrm((DEC_BATCH, DEC_SEQ, D_MODEL), 1.0),
        'state_rwkv_wkv': nrm((DEC_BATCH, L, RW_HEADS, RW_HEAD, RW_HEAD), 1.0),
        'state_rwkv_shift': nrm((DEC_BATCH, L, RW_IN), 1.0),
        'state_s5_re': nrm((DEC_BATCH, L, S5_GROUPS, S5_STATE), 0.1),
        'state_s5_im': nrm((DEC_BATCH, L, S5_GROUPS, S5_STATE), 0.1),
        'cache_conv': nrm((DEC_BATCH, L, CONV_W - 1, C_BR), 0.5),
        'w_in': nrm((L, D_MODEL, N_IN), D_MODEL ** -0.5),
        'rw_mu': unif((L, RW_IN), 0.0, 1.0),
        'rw_w0': unif((L, C_BR), -6.0, -1.0),
        'rw_w2': nrm((L, RW_LW, C_BR), RW_LW ** -0.5),
        'rw_a0': nrm((L, C_BR), 0.1),
        'rw_a2': nrm((L, RW_LA, C_BR), RW_LA ** -0.5),
        'rw_g2': nrm((L, RW_LG, C_BR), RW_LG ** -0.5),
        'rw_kk': 0.85 + nrm((L, C_BR), 0.05),
        'rw_ka': 1.0 + nrm((L, C_BR), 0.05),
        'rw_rk': nrm((L, RW_HEADS, RW_HEAD), 0.1),
        'rw_gn_g': 1.0 + nrm((L, C_BR), 0.05),
        'rw_gn_b': nrm((L, C_BR), 0.02),
        's5_lam_re': -0.5 + nrm((L, S5_GROUPS, S5_STATE), 0.01),
        's5_lam_im': jnp.broadcast_to(modes, (L, S5_GROUPS, S5_STATE)) + nrm((L, S5_GROUPS, S5_STATE), 0.01),
        's5_log_dt': unif((L, S5_GROUPS), math.log(1e-3), math.log(1e-1)),
        's5_b_re': nrm((L, S5_GROUPS, S5_STATE, S5_GW), (2 * S5_GW) ** -0.5),
        's5_b_im': nrm((L, S5_GROUPS, S5_STATE, S5_GW), (2 * S5_GW) ** -0.5),
        's5_c_re': nrm((L, S5_GROUPS, S5_GW, S5_STATE), S5_STATE ** -0.5),
        's5_c_im': nrm((L, S5_GROUPS, S5_GW, S5_STATE), S5_STATE ** -0.5),
        's5_d': nrm((L, S5_GROUPS, S5_GW), 1.0),
        's5_glu_w': nrm((L, C_BR, C_BR), C_BR ** -0.5),
        's5_glu_b': nrm((L, C_BR), 0.02),
        'cv_w': nrm((L, CONV_W, C_BR), CONV_W ** -0.5),
        'cv_b': nrm((L, C_BR), 0.02),
        'cv_ln_g': 1.0 + nrm((L, C_BR), 0.05),
        'cv_ln_b': nrm((L, C_BR), 0.02),
        'gm_ln_g': 1.0 + nrm((L, C_BR), 0.05),
        'gm_ln_b': nrm((L, C_BR), 0.02),
        'gm_ws': nrm((L, GM_HEADS, CHUNK, CHUNK), 0.5 * CHUNK ** -0.5),
        'gm_bs': 1.0 + nrm((L, GM_HEADS, CHUNK), 0.05),
        'w_branch': nrm((L, N_BRANCH, C_BR, D_MODEL), C_BR ** -0.5),
        'w_out': nrm((L, D_MODEL, D_MODEL), DN_BETA * D_MODEL ** -0.5),
        'ln1_g': 1.0 + nrm((L, D_MODEL), 0.05),
        'ln1_b': nrm((L, D_MODEL), 0.02),
        'moe_wg1': nrm((L, D_MODEL, N_GROUPS), D_MODEL ** -0.5),
        'moe_bg1': nrm((L, N_GROUPS), 0.01),
        'moe_wg2': nrm((L, D_MODEL, N_EXPERTS), D_MODEL ** -0.5),
        'moe_bg2': nrm((L, N_EXPERTS), 0.01),
        'moe_w_up': nrm((L, N_EXPERTS, D_MODEL, 2 * D_EXPERT), D_MODEL ** -0.5),
        'moe_w_down': nrm((L, N_EXPERTS, D_EXPERT, D_MODEL), DN_BETA * D_EXPERT ** -0.5),
        'ln2_g': 1.0 + nrm((L, D_MODEL), 0.05),
        'ln2_b': nrm((L, D_MODEL), 0.02),
    }


def reference(x_prompt, x_sample, state_rwkv_wkv, state_rwkv_shift, state_s5_re, state_s5_im, cache_conv,
              w_in, rw_mu, rw_w0, rw_w2, rw_a0, rw_a2, rw_g2, rw_kk, rw_ka, rw_rk, rw_gn_g, rw_gn_b,
              s5_lam_re, s5_lam_im, s5_log_dt, s5_b_re, s5_b_im, s5_c_re, s5_c_im, s5_d, s5_glu_w, s5_glu_b,
              cv_w, cv_b, cv_ln_g, cv_ln_b, gm_ln_g, gm_ln_b, gm_ws, gm_bs,
              w_branch, w_out, ln1_g, ln1_b,
              moe_wg1, moe_bg1, moe_wg2, moe_bg2, moe_w_up, moe_w_down, ln2_g, ln2_b):
    p = {
        'w_in': w_in, 'rw_mu': rw_mu, 'rw_w0': rw_w0, 'rw_w2': rw_w2, 'rw_a0': rw_a0, 'rw_a2': rw_a2,
        'rw_g2': rw_g2, 'rw_kk': rw_kk, 'rw_ka': rw_ka, 'rw_rk': rw_rk, 'rw_gn_g': rw_gn_g, 'rw_gn_b': rw_gn_b,
        's5_lam_re': s5_lam_re, 's5_lam_im': s5_lam_im, 's5_log_dt': s5_log_dt, 's5_b_re': s5_b_re,
        's5_b_im': s5_b_im, 's5_c_re': s5_c_re, 's5_c_im': s5_c_im, 's5_d': s5_d,
        's5_glu_w': s5_glu_w, 's5_glu_b': s5_glu_b,
        'cv_w': cv_w, 'cv_b': cv_b, 'cv_ln_g': cv_ln_g, 'cv_ln_b': cv_ln_b,
        'gm_ln_g': gm_ln_g, 'gm_ln_b': gm_ln_b, 'gm_ws': gm_ws, 'gm_bs': gm_bs,
        'w_branch': w_branch, 'w_out': w_out, 'ln1_g': ln1_g, 'ln1_b': ln1_b,
        'moe_wg1': moe_wg1, 'moe_bg1': moe_bg1, 'moe_wg2': moe_wg2, 'moe_bg2': moe_bg2,
        'moe_w_up': moe_w_up, 'moe_w_down': moe_w_down, 'ln2_g': ln2_g, 'ln2_b': ln2_b,
    }
    bp = x_prompt.shape[0]
    dt = x_prompt.dtype
    y_prompt, (p_wkv, p_shift, p_s5r, p_s5i, p_conv, _) = _run_trunk(
        x_prompt,
        jnp.zeros((bp, DEPTH, RW_HEADS, RW_HEAD, RW_HEAD), dt),
        jnp.zeros((bp, DEPTH, RW_IN), dt),
        jnp.zeros((bp, DEPTH, S5_GROUPS, S5_STATE), dt),
        jnp.zeros((bp, DEPTH, S5_GROUPS, S5_STATE), dt),
        jnp.zeros((bp, DEPTH, CONV_W - 1, C_BR), dt),
        p)
    y_sample, (s_wkv, s_shift, s_s5r, s_s5i, s_conv, s_gmv) = _run_trunk(
        x_sample, state_rwkv_wkv, state_rwkv_shift, state_s5_re, state_s5_im, cache_conv, p)
    return (y_prompt, y_sample, p_wkv, p_shift, p_s5r, p_s5i, p_conv,
            s_wkv, s_shift, s_s5r, s_s5i, s_conv, s_gmv)
```

```python
import functools
import math

import numpy as np
import jax
import jax.numpy as jnp
from jax import lax
from jax.experimental import pallas as pl
from jax.experimental.pallas import tpu as pltpu

D_MODEL = 1024
DEPTH = 4
N_BRANCH = 4
C_BR = D_MODEL // 4
RW_HEAD = 64
RW_HEADS = C_BR // RW_HEAD
RW_LW = 64
RW_LA = 64
RW_LG = 128
RW_IN = 3 * C_BR + RW_LW + RW_LA + RW_LG
RW_GN_EPS = 64e-5
S5_GW = 16
S5_GROUPS = C_BR // S5_GW
S5_STATE = 64
S5_N = S5_GROUPS * S5_STATE
CONV_W = 31
CONV_HIST = CONV_W - 1
CONV_HIST_PAD = 32
CHUNK = 128
GM_HEADS = 4
GM_HEAD = C_BR // GM_HEADS
N_GROUPS = 4
E_PER_GROUP = 4
N_EXPERTS = N_GROUPS * E_PER_GROUP
D_EXPERT = D_MODEL // 4
LN_EPS = 1e-5
DN_ALPHA = (2 * DEPTH) ** 0.25
OFF_S5 = RW_IN
OFF_CV = OFF_S5 + C_BR
OFF_GM = OFF_CV + 2 * C_BR
OFF_GATE = OFF_GM + 2 * C_BR
N_IN = OFF_GATE + N_BRANCH * D_MODEL
P_GATE = 0
P_RW = N_BRANCH * D_MODEL
P_CV = P_RW + RW_IN
P_GM = P_CV + 2 * C_BR
P_S5 = P_GM + 2 * C_BR

LANES = 128
VMEM_LIMIT = 56 * 1024 * 1024

F32 = jnp.float32
MM_DTYPE = jnp.bfloat16
HI = lax.Precision.HIGHEST
NEG_BIG = -1e30


def _mm(a, b):
    return jnp.dot(a.astype(MM_DTYPE), b.astype(MM_DTYPE), preferred_element_type=F32)


def _mm_hi(a, b):
    return jnp.dot(a, b, precision=HI, preferred_element_type=F32)


def _mm_nt_hi(a, b):
    return lax.dot_general(a, b, (((1,), (1,)), ((), ())), precision=HI, preferred_element_type=F32)


def _mm_tn_hi(a, b):
    return lax.dot_general(a, b, (((0,), (0,)), ((), ())), precision=HI, preferred_element_type=F32)


def _sigmoid(x):
    return jax.nn.sigmoid(x)


def _softplus(x):
    return jnp.maximum(x, 0.0) + jnp.log1p(jnp.exp(-jnp.abs(x)))


def _gelu_tanh(x):
    return 0.5 * x * (1.0 + jnp.tanh(math.sqrt(2.0 / math.pi) * (x + 0.044715 * (x * x * x))))


def _layer_norm(x, g, b):
    mu = jnp.mean(x, axis=-1, keepdims=True)
    d = x - mu
    var = jnp.mean(d * d, axis=-1, keepdims=True)
    return d * lax.rsqrt(var + LN_EPS) * g + b


def _params(sem):
    return pltpu.CompilerParams(dimension_semantics=sem, vmem_limit_bytes=VMEM_LIMIT)


def _full(shape):
    nd = len(shape)
    return pl.BlockSpec(shape, lambda *_: (0,) * nd)


def _inproj_kernel(x_ref, w_ref, z_ref):
    z_ref[...] = _mm(x_ref[...], w_ref[...])


def _inproj(x2d, w_bf16):
    n = x2d.shape[0]
    tm = 512
    tn = 1280
    return pl.pallas_call(
        _inproj_kernel,
        grid=(n // tm, N_IN // tn),
        in_specs=[pl.BlockSpec((tm, D_MODEL), lambda i, j: (i, 0)),
                  pl.BlockSpec((D_MODEL, tn), lambda i, j: (0, j))],
        out_specs=pl.BlockSpec((tm, tn), lambda i, j: (i, j)),
        out_shape=jax.ShapeDtypeStruct((n, N_IN), F32),
        compiler_params=_params(("parallel", "arbitrary")),
        name="inproj",
    )(x2d, w_bf16)


def _rwkv_kernel(z_ref, sh0_ref, wkv0_ref, mu_ref, w0_ref, w2_ref, a0_ref, a2_ref, g2_ref, kkw_ref, kaw_ref,
                 rk_ref, gng_ref, gnb_ref, hsum_ref, tri_ref, same_ref,
                 y_ref, wkv1_ref,
                 s_scr, prev_scr, rt_scr, kkt_scr, kh_scr, bh_scr, kw_scr, bw_scr, v_scr, o_scr, wc_scr,
                 *, t_tile, chunk, t_valid, n_tiles):
    i = pl.program_id(1)

    @pl.when(i == 0)
    def _():
        s_scr[...] = wkv0_ref[...]
        prev_scr[...] = sh0_ref[...]

    z = z_ref[...]
    row = lax.broadcasted_iota(jnp.int32, (t_tile, 1), 0)
    z_prev = jnp.where(row == 0, prev_scr[...], pltpu.roll(z, 1, 0))
    prev_scr[...] = z[t_tile - 1:t_tile, :]
    zs = z + mu_ref[...] * (z_prev - z)
    r = zs[:, 0:C_BR]
    k = zs[:, C_BR:2 * C_BR]
    v = zs[:, 2 * C_BR:3 * C_BR]
    lwla = zs[:, 3 * C_BR:3 * C_BR + RW_LW + RW_LA]
    lg = zs[:, 3 * C_BR + RW_LW + RW_LA:]
    w_log = -_softplus(-(w0_ref[...] + _mm(jnp.tanh(lwla), w2_ref[...]))) - 0.5
    ld = -jnp.exp(w_log)
    a = _sigmoid(a0_ref[...] + _mm(lwla, a2_ref[...]))
    g = _mm(_sigmoid(lg), g2_ref[...])
    hsum = hsum_ref[...]
    kk = k * kkw_ref[...]
    kk = kk * lax.rsqrt(jnp.maximum(_mm_hi(kk * kk, hsum), 1e-24))
    k2 = k * (1.0 + (a - 1.0) * kaw_ref[...])
    bv = kk * a
    if t_valid < t_tile * n_tiles:
        valid = (row + i * t_tile) < t_valid
        ld = jnp.where(valid, ld, 0.0)
        k2 = jnp.where(valid, k2, 0.0)
        v = jnp.where(valid, v, 0.0)
        bv = jnp.where(valid, bv, 0.0)
    lc = _mm_hi(tri_ref[...], ld)
    lend = _mm_hi(same_ref[...], ld)
    le = lc - ld
    e_end = jnp.exp(lend - lc)
    e_neg = jnp.exp(-lc)
    per_head = ((rt_scr, r * jnp.exp(lc)), (kkt_scr, kk * jnp.exp(le)), (kh_scr, k2 * e_neg), (bh_scr, bv * e_neg),
                (kw_scr, k2 * e_end), (bw_scr, bv * e_end), (v_scr, v), (wc_scr, jnp.exp(lend)))
    for scr, val in per_head:
        for h in range(RW_HEADS):
            scr[h] = val[:, h * RW_HEAD:(h + 1) * RW_HEAD]

    ci = lax.broadcasted_iota(jnp.int32, (chunk, chunk), 0)
    cj = lax.broadcasted_iota(jnp.int32, (chunk, chunk), 1)
    strict = cj < ci
    incl = cj <= ci
    eye = (ci == cj).astype(F32)
    n_double = int(math.log2(chunk)) - 1

    def chunk_body(c, carry):
        sl = pl.ds(pl.multiple_of(c * chunk, chunk), chunk)
        for h in range(RW_HEADS):
            s = s_scr[h]
            rt = rt_scr[h, sl, :]
            kkt = kkt_scr[h, sl, :]
            kh = kh_scr[h, sl, :]
            bh = bh_scr[h, sl, :]
            vv = v_scr[h, sl, :]
            a_kk = jnp.where(strict, _mm_nt_hi(kkt, kh), 0.0)
            a_kb = jnp.where(strict, _mm_nt_hi(kkt, bh), 0.0)
            a_rk = jnp.where(incl, _mm_nt_hi(rt, kh), 0.0)
            a_rb = jnp.where(incl, _mm_nt_hi(rt, bh), 0.0)
            t_inv = eye - a_kb
            pw = a_kb
            for _ in range(n_double):
                pw = _mm_hi(pw, pw)
                t_inv = t_inv + _mm_hi(t_inv, pw)
            u = _mm_hi(t_inv, _mm_nt_hi(kkt, s) + _mm_hi(a_kk, vv))
            o = _mm_nt_hi(rt, s) + _mm_hi(a_rk, vv) - _mm_hi(a_rb, u)
            wc = wc_scr[h, pl.ds(c * chunk, 1), :]
            s_scr[h] = s * wc + _mm_tn_hi(vv, kw_scr[h, sl, :]) - _mm_tn_hi(u, bw_scr[h, sl, :])
            o_scr[h, sl, :] = o
        return carry

    lax.fori_loop(0, t_tile // chunk, chunk_body, 0)

    o = jnp.concatenate([o_scr[h] for h in range(RW_HEADS)], axis=1)
    inv_n = 1.0 / RW_HEAD
    o_mu = _mm_hi(o, hsum) * inv_n
    od = o - o_mu
    o_var = _mm_hi(od * od, hsum) * inv_n
    on = od * lax.rsqrt(o_var + RW_GN_EPS) * gng_ref[...] + gnb_ref[...]
    bonus = _mm_hi(r * k2 * rk_ref[...], hsum) * v
    y_ref[...] = (on + bonus) * g

    @pl.when(i == n_tiles - 1)
    def _():
        wkv1_ref[...] = s_scr[...]


def _rwkv(z3d, col_blk, shift0, wkv0, pw, *, t_tile, chunk, t_valid):
    bsz, t_len, _ = z3d.shape
    n_tiles = t_len // t_tile
    idx = np.arange(t_tile)
    same = (idx[:, None] // chunk == idx[None, :] // chunk)
    tri = jnp.asarray((same & (idx[None, :] <= idx[:, None])).astype(np.float32))
    same = jnp.asarray(same.astype(np.float32))
    hid = np.arange(C_BR) // RW_HEAD
    hsum = jnp.asarray((hid[:, None] == hid[None, :]).astype(np.float32))
    vec = lambda n: _full((1, n))
    head_scr = pltpu.VMEM((RW_HEADS, t_tile, RW_HEAD), F32)
    kern = functools.partial(_rwkv_kernel, t_tile=t_tile, chunk=chunk, t_valid=t_valid, n_tiles=n_tiles)
    return pl.pallas_call(
        kern,
        grid=(bsz, n_tiles),
        in_specs=[pl.BlockSpec((None, t_tile, RW_IN), lambda b, i: (b, i, col_blk)),
                  pl.BlockSpec((None, 1, RW_IN), lambda b, i: (b, 0, 0)),
                  pl.BlockSpec((None, RW_HEADS, RW_HEAD, RW_HEAD), lambda b, i: (b, 0, 0, 0)),
                  vec(RW_IN), vec(C_BR), _full((RW_LW + RW_LA, C_BR)), vec(C_BR), _full((RW_LW + RW_LA, C_BR)),
                  _full((RW_LG, C_BR)), vec(C_BR), vec(C_BR), vec(C_BR), vec(C_BR), vec(C_BR),
                  _full((C_BR, C_BR)), _full((t_tile, t_tile)), _full((t_tile, t_tile))],
        out_specs=[pl.BlockSpec((None, t_tile, C_BR), lambda b, i: (b, i, 0)),
                   pl.BlockSpec((None, RW_HEADS, RW_HEAD, RW_HEAD), lambda b, i: (b, 0, 0, 0))],
        out_shape=[jax.ShapeDtypeStruct((bsz, t_len, C_BR), F32),
                   jax.ShapeDtypeStruct((bsz, RW_HEADS, RW_HEAD, RW_HEAD), F32)],
        scratch_shapes=[pltpu.VMEM((RW_HEADS, RW_HEAD, RW_HEAD), F32), pltpu.VMEM((1, RW_IN), F32)]
        + [head_scr] * 9,
        compiler_params=_params(("parallel", "arbitrary")),
        name="rwkv7",
    )(z3d, shift0, wkv0, pw['mu'], pw['w0'], pw['w2'], pw['a0'], pw['a2'], pw['g2'], pw['kk'], pw['ka'],
      pw['rk'], pw['gn_g'], pw['gn_b'], hsum, tri, same)


def _s5_kernel(u_ref, h0_ref, lbr_ref, lbi_ref, bb_ref, cc_ref, d_ref, gw_ref, gb_ref,
               y_ref, h1_ref, bu_scr, h_scr, *, bsz, t_tile, n_tiles):
    i = pl.program_id(0)

    @pl.when(i == 0)
    def _():
        h_scr[...] = h0_ref[...]

    rows = bsz * t_tile
    n_lt = S5_N // LANES
    u = u_ref[...].reshape(rows, C_BR)
    bu = _mm(u, bb_ref[...])
    for j in range(2 * n_lt):
        bu_scr[j] = bu[:, j * LANES:(j + 1) * LANES]
    lane_tile = lambda ref, j: ref[:, j * LANES:(j + 1) * LANES]
    lbr = [jnp.broadcast_to(lane_tile(lbr_ref, j), (bsz, LANES)) for j in range(n_lt)]
    lbi = [jnp.broadcast_to(lane_tile(lbi_ref, j), (bsz, LANES)) for j in range(n_lt)]

    def step(t, carry):
        hr, hi = carry
        sl = pl.ds(t, bsz, stride=t_tile)
        new_r, new_i = [], []
        for j in range(n_lt):
            nr = lbr[j] * hr[j] - lbi[j] * hi[j] + bu_scr[j, sl, :]
            ni = lbr[j] * hi[j] + lbi[j] * hr[j] + bu_scr[n_lt + j, sl, :]
            bu_scr[j, sl, :] = nr
            bu_scr[n_lt + j, sl, :] = ni
            new_r.append(nr)
            new_i.append(ni)
        return tuple(new_r), tuple(new_i)

    h_init = (tuple(lane_tile(h_scr, j) for j in range(n_lt)),
              tuple(lane_tile(h_scr, n_lt + j) for j in range(n_lt)))
    hr, hi = lax.fori_loop(0, t_tile, step, h_init)
    for j in range(n_lt):
        h_scr[:, j * LANES:(j + 1) * LANES] = hr[j]
        h_scr[:, (n_lt + j) * LANES:(n_lt + j + 1) * LANES] = hi[j]

    hs = jnp.concatenate([bu_scr[j] for j in range(2 * n_lt)], axis=1)
    y = _mm(hs, cc_ref[...]) + d_ref[...] * u
    y = _gelu_tanh(y)
    y = y * _sigmoid(_mm(y, gw_ref[...]) + gb_ref[...])
    y_ref[...] = y.reshape(y_ref.shape)

    @pl.when(i == n_tiles - 1)
    def _():
        h1_ref[...] = h_scr[...]


def _s5(z, col_blk, h0, ps, *, bsz, t_len, t_tile):
    n_tiles = t_len // t_tile
    rows = bsz * t_tile
    if z.ndim == 3:
        u_spec = pl.BlockSpec((bsz, t_tile, C_BR), lambda i: (0, i, col_blk))
        y_spec = pl.BlockSpec((bsz, t_tile, C_BR), lambda i: (0, i, 0))
        y_shape = (bsz, t_len, C_BR)
    else:
        assert n_tiles == 1
        u_spec = pl.BlockSpec((rows, C_BR), lambda i: (0, col_blk))
        y_spec = pl.BlockSpec((rows, C_BR), lambda i: (0, 0))
        y_shape = (rows, C_BR)
    kern = functools.partial(_s5_kernel, bsz=bsz, t_tile=t_tile, n_tiles=n_tiles)
    return pl.pallas_call(
        kern,
        grid=(n_tiles,),
        in_specs=[u_spec, _full((bsz, 2 * S5_N)), _full((1, S5_N)), _full((1, S5_N)),
                  _full((C_BR, 2 * S5_N)), _full((2 * S5_N, C_BR)), _full((1, C_BR)),
                  _full((C_BR, C_BR)), _full((1, C_BR))],
        out_specs=[y_spec, _full((bsz, 2 * S5_N))],
        out_shape=[jax.ShapeDtypeStruct(y_shape, F32), jax.ShapeDtypeStruct((bsz, 2 * S5_N), F32)],
        scratch_shapes=[pltpu.VMEM((2 * S5_N // LANES, rows, LANES), F32), pltpu.VMEM((bsz, 2 * S5_N), F32)],
        compiler_params=_params(("arbitrary",)),
        name="s5",
    )(z, h0, ps['lb_re'], ps['lb_im'], ps['bb'], ps['cc'], ps['d'], ps['glu_w'], ps['glu_b'])


def _conv_kernel(z_ref, c0_ref, w_ref, b_ref, g_ref, be_ref, y_ref, c1_ref, full_scr, *, t_tile, n_tiles):
    i = pl.program_id(1)
    lo = CONV_HIST_PAD - CONV_HIST

    @pl.when(i == 0)
    def _():
        full_scr[0:lo, :] = jnp.zeros((lo, C_BR), F32)
        full_scr[lo:CONV_HIST_PAD, :] = c0_ref[...]

    z = z_ref[...]
    c = z[:, 0:C_BR] * _sigmoid(z[:, C_BR:2 * C_BR])
    full_scr[CONV_HIST_PAD:CONV_HIST_PAD + t_tile, :] = c
    acc = jnp.zeros((t_tile, C_BR), F32) + b_ref[...]
    for j in range(CONV_W):
        acc = acc + full_scr[lo + j:lo + j + t_tile, :] * w_ref[j:j + 1, :]
    y = _layer_norm(acc, g_ref[...], be_ref[...])
    y_ref[...] = y * _sigmoid(y)
    hist = full_scr[t_tile:t_tile + CONV_HIST_PAD, :]
    full_scr[0:CONV_HIST_PAD, :] = hist

    @pl.when(i == n_tiles - 1)
    def _():
        c1_ref[...] = hist[lo:, :]


def _conv(z3d, col_blk, conv0, pc, *, t_tile):
    bsz, t_len, _ = z3d.shape
    n_tiles = t_len // t_tile
    kern = functools.partial(_conv_kernel, t_tile=t_tile, n_tiles=n_tiles)
    return pl.pallas_call(
        kern,
        grid=(bsz, n_tiles),
        in_specs=[pl.BlockSpec((None, t_tile, 2 * C_BR), lambda b, i: (b, i, col_blk)),
                  pl.BlockSpec((None, CONV_HIST, C_BR), lambda b, i: (b, 0, 0)),
                  _full((CONV_W, C_BR)), _full((1, C_BR)), _full((1, C_BR)), _full((1, C_BR))],
        out_specs=[pl.BlockSpec((None, t_tile, C_BR), lambda b, i: (b, i, 0)),
                   pl.BlockSpec((None, CONV_HIST, C_BR), lambda b, i: (b, 0, 0))],
        out_shape=[jax.ShapeDtypeStruct((bsz, t_len, C_BR), F32),
                   jax.ShapeDtypeStruct((bsz, CONV_HIST, C_BR), F32)],
        scratch_shapes=[pltpu.VMEM((CONV_HIST_PAD + t_tile, C_BR), F32)],
        compiler_params=_params(("parallel", "arbitrary")),
        name="conv",
    )(z3d, conv0, pc['w'], pc['b'], pc['ln_g'], pc['ln_b'])


def _gmlp_kernel(z_ref, g_ref, b_ref, wm_ref, bias_ref, y_ref, v_ref):
    z = z_ref[...]
    u = z[:, 0:C_BR]
    v = _layer_norm(z[:, C_BR:2 * C_BR], g_ref[...], b_ref[...])
    v_ref[...] = v
    head = lax.broadcasted_iota(jnp.int32, (CHUNK, C_BR), 1) // GM_HEAD
    s = bias_ref[...]
    for h in range(GM_HEADS):
        s = s + jnp.where(head == h, _mm(wm_ref[h], v), 0.0)
    y_ref[...] = u * s


def _gmlp(z3d, col_blk, pg, wm, bias):
    bsz, t_len, _ = z3d.shape
    return pl.pallas_call(
        _gmlp_kernel,
        grid=(bsz, t_len // CHUNK),
        in_specs=[pl.BlockSpec((None, CHUNK, 2 * C_BR), lambda b, i: (b, i, col_blk)),
                  _full((1, C_BR)), _full((1, C_BR)), _full((GM_HEADS, CHUNK, CHUNK)), _full((CHUNK, C_BR))],
        out_specs=[pl.BlockSpec((None, CHUNK, C_BR), lambda b, i: (b, i, 0)),
                   pl.BlockSpec((None, CHUNK, C_BR), lambda b, i: (b, i, 0))],
        out_shape=[jax.ShapeDtypeStruct((bsz, t_len, C_BR), F32)] * 2,
        compiler_params=_params(("parallel", "parallel")),
        name="gmlp",
    )(z3d, pg['ln_g'], pg['ln_b'], wm, bias)


def _merge_kernel(x_ref, gate_ref, yrw_ref, ys5_ref, ycv_ref, ygm_ref, wb_ref, wo_ref, g_ref, b_ref, o_ref):
    merged = None
    for bidx, y_ref in enumerate((yrw_ref, ys5_ref, ycv_ref, ygm_ref)):
        gate = _sigmoid(gate_ref[:, bidx * D_MODEL:(bidx + 1) * D_MODEL])
        term = gate * _mm(y_ref[...], wb_ref[bidx])
        merged = term if merged is None else merged + term
    x = DN_ALPHA * x_ref[...] + _mm(merged, wo_ref[...])
    o_ref[...] = _layer_norm(x, g_ref[...], b_ref[...])


def _merge(x2d, z2d, ys, pm):
    n = x2d.shape[0]
    tm = 256
    row = lambda w: pl.BlockSpec((tm, w), lambda i: (i, 0))
    return pl.pallas_call(
        _merge_kernel,
        grid=(n // tm,),
        in_specs=[row(D_MODEL), row(N_BRANCH * D_MODEL), row(C_BR), row(C_BR), row(C_BR), row(C_BR),
                  _full((N_BRANCH, C_BR, D_MODEL)), _full((D_MODEL, D_MODEL)), _full((1, D_MODEL)),
                  _full((1, D_MODEL))],
        out_specs=row(D_MODEL),
        out_shape=jax.ShapeDtypeStruct((n, D_MODEL), F32),
        compiler_params=_params(("parallel",)),
        name="merge",
    )(x2d, z2d, *ys, pm['w_branch'], pm['w_out'], pm['ln1_g'], pm['ln1_b'])


def _moe_kernel(x_ref, wg_ref, bg_ref, wu_ref, wd_ref, g_ref, b_ref, o_ref, xb_scr, comb_scr, acc_scr, *, tm):
    e = pl.program_id(1)
    lane = lax.broadcasted_iota(jnp.int32, (tm, LANES), 1)

    @pl.when(e == 0)
    def _():
        x = x_ref[...]
        xb_scr[...] = x.astype(MM_DTYPE)
        acc_scr[...] = jnp.zeros_like(acc_scr)
        logits = _mm_hi(x, wg_ref[...]) + bg_ref[...]
        gl = jnp.where(lane < N_GROUPS, logits, NEG_BIG)
        gmax = jnp.max(gl, axis=-1, keepdims=True)
        g_sel = jnp.min(jnp.where(gl == gmax, lane, LANES), axis=-1, keepdims=True)
        p_group = 1.0 / jnp.sum(jnp.where(lane < N_GROUPS, jnp.exp(gl - gmax), 0.0), axis=-1, keepdims=True)
        first = N_GROUPS + g_sel * E_PER_GROUP
        el = jnp.where((lane >= first) & (lane < first + E_PER_GROUP), logits, NEG_BIG)
        m1 = jnp.max(el, axis=-1, keepdims=True)
        i1 = jnp.min(jnp.where(el == m1, lane, LANES), axis=-1, keepdims=True)
        el2 = jnp.where(lane == i1, NEG_BIG, el)
        m2 = jnp.max(el2, axis=-1, keepdims=True)
        i2 = jnp.min(jnp.where(el2 == m2, lane, LANES), axis=-1, keepdims=True)
        e2 = jnp.exp(m2 - m1)
        w1 = p_group / (1.0 + e2)
        w2 = p_group * e2 / (1.0 + e2)
        comb_scr[...] = jnp.where(lane == i1, w1, 0.0) + jnp.where(lane == i2, w2, 0.0)

    comb_e = jnp.sum(jnp.where(lane == e + N_GROUPS, comb_scr[...], 0.0), axis=-1, keepdims=True)
    h = jnp.dot(xb_scr[...], wu_ref[...], preferred_element_type=F32)
    h1 = h[:, 0:D_EXPERT]
    hh = h1 * _sigmoid(h1) * h[:, D_EXPERT:2 * D_EXPERT] * comb_e
    acc_scr[...] += _mm(hh, wd_ref[...])

    @pl.when(e == N_EXPERTS - 1)
    def _():
        o_ref[...] = _layer_norm(DN_ALPHA * x_ref[...] + acc_scr[...], g_ref[...], b_ref[...])


def _moe(x2d, pe):
    n = x2d.shape[0]
    tm = 512
    kern = functools.partial(_moe_kernel, tm=tm)
    return pl.pallas_call(
        kern,
        grid=(n // tm, N_EXPERTS),
        in_specs=[pl.BlockSpec((tm, D_MODEL), lambda i, e: (i, 0)),
                  _full((D_MODEL, LANES)), _full((1, LANES)),
                  pl.BlockSpec((None, D_MODEL, 2 * D_EXPERT), lambda i, e: (e, 0, 0)),
                  pl.BlockSpec((None, D_EXPERT, D_MODEL), lambda i, e: (e, 0, 0)),
                  _full((1, D_MODEL)), _full((1, D_MODEL))],
        out_specs=pl.BlockSpec((tm, D_MODEL), lambda i, e: (i, 0)),
        out_shape=jax.ShapeDtypeStruct((n, D_MODEL), F32),
        scratch_shapes=[pltpu.VMEM((tm, D_MODEL), MM_DTYPE), pltpu.VMEM((tm, LANES), F32),
                        pltpu.VMEM((tm, D_MODEL), F32)],
        compiler_params=_params(("parallel", "arbitrary")),
        name="moe",
    )(x2d, pe['wg'], pe['bg'], pe['w_up'], pe['w_down'], pe['ln2_g'], pe['ln2_b'])


def _block_diag(blocks):
    g, m, n = blocks.shape
    eye = jnp.eye(g, dtype=blocks.dtype)
    return (eye[:, None, :, None] * blocks[:, :, None, :]).reshape(g * m, g * n)


def _prep_layer(l, p):
    w_in = p['w_in'][l]
    w_in_p = jnp.concatenate([w_in[:, OFF_GATE:], w_in[:, :OFF_S5], w_in[:, OFF_CV:OFF_GM],
                              w_in[:, OFF_GM:OFF_GATE], w_in[:, OFF_S5:OFF_CV]], axis=1).astype(MM_DTYPE)
    row = lambda a: a.reshape(1, -1).astype(F32)
    zeros_lora = jnp.zeros((RW_LW, C_BR), F32)
    rw = dict(mu=row(p['rw_mu'][l]), w0=row(p['rw_w0'][l]),
              w2=jnp.concatenate([p['rw_w2'][l], zeros_lora], axis=0).astype(MM_DTYPE),
              a0=row(p['rw_a0'][l]),
              a2=jnp.concatenate([zeros_lora, p['rw_a2'][l]], axis=0).astype(MM_DTYPE),
              g2=p['rw_g2'][l].astype(MM_DTYPE), kk=row(p['rw_kk'][l]), ka=row(p['rw_ka'][l]),
              rk=row(p['rw_rk'][l]), gn_g=row(p['rw_gn_g'][l]), gn_b=row(p['rw_gn_b'][l]))
    lr, li = p['s5_lam_re'][l].astype(F32), p['s5_lam_im'][l].astype(F32)
    dt = jnp.exp(p['s5_log_dt'][l].astype(F32))[:, None]
    mag = jnp.exp(lr * dt)
    lb_re, lb_im = mag * jnp.cos(li * dt), mag * jnp.sin(li * dt)
    den = lr * lr + li * li
    q_re = ((lb_re - 1.0) * lr + lb_im * li) / den
    q_im = (lb_im * lr - (lb_re - 1.0) * li) / den
    br, bi = p['s5_b_re'][l].astype(F32), p['s5_b_im'][l].astype(F32)
    bb_re = q_re[..., None] * br - q_im[..., None] * bi
    bb_im = q_re[..., None] * bi + q_im[..., None] * br
    bb = jnp.concatenate([_block_diag(jnp.swapaxes(bb_re, 1, 2)), _block_diag(jnp.swapaxes(bb_im, 1, 2))],
                         axis=1).astype(MM_DTYPE)
    cc = jnp.concatenate([_block_diag(jnp.swapaxes(p['s5_c_re'][l].astype(F32), 1, 2)),
                          -_block_diag(jnp.swapaxes(p['s5_c_im'][l].astype(F32), 1, 2))],
                         axis=0).astype(MM_DTYPE)
    s5 = dict(lb_re=lb_re.reshape(1, S5_N), lb_im=lb_im.reshape(1, S5_N), bb=bb, cc=cc, d=row(p['s5_d'][l]),
              glu_w=p['s5_glu_w'][l].astype(MM_DTYPE), glu_b=row(p['s5_glu_b'][l]))
    cv = dict(w=p['cv_w'][l].astype(F32), b=row(p['cv_b'][l]), ln_g=row(p['cv_ln_g'][l]), ln_b=row(p['cv_ln_b'][l]))
    causal = jnp.tril(jnp.ones((CHUNK, CHUNK), dtype=bool))
    wm = jnp.where(causal, p['gm_ws'][l], 0).astype(F32)
    bias = jnp.repeat(jnp.swapaxes(p['gm_bs'][l], 0, 1), GM_HEAD, axis=1).astype(F32)
    gm = dict(ln_g=row(p['gm_ln_g'][l]), ln_b=row(p['gm_ln_b'][l]), wm=wm, bias=bias)
    mg = dict(w_branch=p['w_branch'][l].astype(MM_DTYPE), w_out=p['w_out'][l].astype(MM_DTYPE),
              ln1_g=row(p['ln1_g'][l]), ln1_b=row(p['ln1_b'][l]))
    pad = LANES - N_GROUPS - N_EXPERTS
    wg = jnp.concatenate([p['moe_wg1'][l], p['moe_wg2'][l], jnp.zeros((D_MODEL, pad), F32)], axis=1).astype(F32)
    bg = jnp.concatenate([p['moe_bg1'][l], p['moe_bg2'][l], jnp.zeros((pad,), F32)]).reshape(1, LANES).astype(F32)
    moe = dict(wg=wg, bg=bg, w_up=p['moe_w_up'][l].astype(MM_DTYPE), w_down=p['moe_w_down'][l].astype(MM_DTYPE),
               ln2_g=row(p['ln2_g'][l]), ln2_b=row(p['ln2_b'][l]))
    return dict(w_in=w_in_p, rw=rw, s5=s5, cv=cv, gm=gm, mg=mg, moe=moe)


def _gmlp_group(z3d, pg, *, is_prompt):
    bsz, t_len, n_cols = z3d.shape
    if is_prompt:
        return _gmlp(z3d, P_GM // (2 * C_BR), pg, pg['wm'].astype(MM_DTYPE), pg['bias'])
    reps = CHUNK // t_len
    wm = jnp.stack([jnp.kron(jnp.eye(reps, dtype=F32), pg['wm'][h, :t_len, :t_len]) for h in range(GM_HEADS)])
    bias = jnp.tile(pg['bias'][:t_len], (reps, 1))
    y, v = _gmlp(z3d.reshape(1, bsz * t_len, n_cols), P_GM // (2 * C_BR), pg, wm.astype(MM_DTYPE), bias)
    return y.reshape(bsz, t_len, C_BR), v.reshape(bsz, t_len, C_BR)


def _run_group(x, wkv0, shift0, s5r0, s5i0, conv0, layers, *, is_prompt):
    bsz, t_len, _ = x.shape
    n = bsz * t_len
    x2d = x.reshape(n, D_MODEL)
    outs = []
    for l, lp in enumerate(layers):
        z2d = _inproj(x2d, lp['w_in'])
        z3d = z2d.reshape(bsz, t_len, N_IN)
        sh0 = shift0[:, l].reshape(bsz, 1, RW_IN)
        if is_prompt:
            y_rw, wkv1 = _rwkv(z3d, P_RW // RW_IN, sh0, wkv0[:, l], lp['rw'], t_tile=256, chunk=64,
                               t_valid=t_len)
        else:
            t_pad = 8
            z_rw = jnp.pad(z3d[:, :, P_RW:P_RW + RW_IN], ((0, 0), (0, t_pad - t_len), (0, 0)))
            y_rw, wkv1 = _rwkv(z_rw, 0, sh0, wkv0[:, l], lp['rw'], t_tile=t_pad, chunk=t_pad, t_valid=t_len)
            y_rw = y_rw[:, :t_len]
        shift1 = z3d[:, t_len - 1, P_RW:P_RW + RW_IN]
        h0 = jnp.concatenate([s5r0[:, l].reshape(bsz, S5_N), s5i0[:, l].reshape(bsz, S5_N)], axis=1)
        if is_prompt:
            y_s5, h1 = _s5(z3d, P_S5 // C_BR, h0, lp['s5'], bsz=bsz, t_len=t_len, t_tile=128)
        else:
            y_s5, h1 = _s5(z2d, P_S5 // C_BR, h0, lp['s5'], bsz=bsz, t_len=t_len, t_tile=t_len)
        s5r1 = h1[:, :S5_N].reshape(bsz, S5_GROUPS, S5_STATE)
        s5i1 = h1[:, S5_N:].reshape(bsz, S5_GROUPS, S5_STATE)
        y_cv, conv1 = _conv(z3d, P_CV // (2 * C_BR), conv0[:, l], lp['cv'], t_tile=256 if is_prompt else t_len)
        y_gm, v_gm = _gmlp_group(z3d, lp['gm'], is_prompt=is_prompt)
        ys = [y.reshape(n, C_BR) for y in (y_rw, y_s5, y_cv, y_gm)]
        x2d = _merge(x2d, z2d, ys, lp['mg'])
        x2d = _moe(x2d, lp['moe'])
        outs.append((wkv1, shift1, s5r1, s5i1, conv1, v_gm))
    states = tuple(jnp.stack([o[i] for o in outs], axis=1) for i in range(6))
    return x2d.reshape(bsz, t_len, D_MODEL), states


def kernel(x_prompt, x_sample, state_rwkv_wkv, state_rwkv_shift, state_s5_re, state_s5_im, cache_conv,
           w_in, rw_mu, rw_w0, rw_w2, rw_a0, rw_a2, rw_g2, rw_kk, rw_ka, rw_rk, rw_gn_g, rw_gn_b,
           s5_lam_re, s5_lam_im, s5_log_dt, s5_b_re, s5_b_im, s5_c_re, s5_c_im, s5_d, s5_glu_w, s5_glu_b,
           cv_w, cv_b, cv_ln_g, cv_ln_b, gm_ln_g, gm_ln_b, gm_ws, gm_bs,
           w_branch, w_out, ln1_g, ln1_b,
           moe_wg1, moe_bg1, moe_wg2, moe_bg2, moe_w_up, moe_w_down, ln2_g, ln2_b):
    p = dict(w_in=w_in, rw_mu=rw_mu, rw_w0=rw_w0, rw_w2=rw_w2, rw_a0=rw_a0, rw_a2=rw_a2, rw_g2=rw_g2,
             rw_kk=rw_kk, rw_ka=rw_ka, rw_rk=rw_rk, rw_gn_g=rw_gn_g, rw_gn_b=rw_gn_b,
             s5_lam_re=s5_lam_re, s5_lam_im=s5_lam_im, s5_log_dt=s5_log_dt, s5_b_re=s5_b_re, s5_b_im=s5_b_im,
             s5_c_re=s5_c_re, s5_c_im=s5_c_im, s5_d=s5_d, s5_glu_w=s5_glu_w, s5_glu_b=s5_glu_b,
             cv_w=cv_w, cv_b=cv_b, cv_ln_g=cv_ln_g, cv_ln_b=cv_ln_b, gm_ln_g=gm_ln_g, gm_ln_b=gm_ln_b,
             gm_ws=gm_ws, gm_bs=gm_bs, w_branch=w_branch, w_out=w_out, ln1_g=ln1_g, ln1_b=ln1_b,
             moe_wg1=moe_wg1, moe_bg1=moe_bg1, moe_wg2=moe_wg2, moe_bg2=moe_bg2, moe_w_up=moe_w_up,
             moe_w_down=moe_w_down, ln2_g=ln2_g, ln2_b=ln2_b)
    layers = [_prep_layer(l, p) for l in range(DEPTH)]
    bp = x_prompt.shape[0]
    dt = x_prompt.dtype
    y_prompt, (p_wkv, p_shift, p_s5r, p_s5i, p_conv, _) = _run_group(
        x_prompt,
        jnp.zeros((bp, DEPTH, RW_HEADS, RW_HEAD, RW_HEAD), dt),
        jnp.zeros((bp, DEPTH, RW_IN), dt),
        jnp.zeros((bp, DEPTH, S5_GROUPS, S5_STATE), dt),
        jnp.zeros((bp, DEPTH, S5_GROUPS, S5_STATE), dt),
        jnp.zeros((bp, DEPTH, CONV_HIST, C_BR), dt),
        layers, is_prompt=True)
    y_sample, (s_wkv, s_shift, s_s5r, s_s5i, s_conv, s_gmv) = _run_group(
        x_sample, state_rwkv_wkv, state_rwkv_shift, state_s5_re, state_s5_im, cache_conv, layers,
        is_prompt=False)
    return (y_prompt, y_sample, p_wkv, p_shift, p_s5r, p_s5i, p_conv,
            s_wkv, s_shift, s_s5r, s_s5i, s_conv, s_gmv)
```

```python
import functools
import math

import numpy as np
import jax
import jax.numpy as jnp
from jax import lax
from jax.experimental import pallas as pl
from jax.experimental.pallas import tpu as pltpu

D_MODEL = 1024
DEPTH = 4
N_BRANCH = 4
C_BR = D_MODEL // 4
RW_HEAD = 64
RW_HEADS = C_BR // RW_HEAD
RW_LW = 64
RW_LA = 64
RW_LG = 128
RW_IN = 3 * C_BR + RW_LW + RW_LA + RW_LG
RW_GN_EPS = 64e-5
RW_PAD = 8
RW_GROUP = 8
S5_GW = 16
S5_GROUPS = C_BR // S5_GW
S5_STATE = 64
S5_N = S5_GROUPS * S5_STATE
CONV_W = 31
CONV_HIST = CONV_W - 1
CONV_HIST_PAD = 32
CHUNK = 128
GM_HEADS = 4
GM_HEAD = C_BR // GM_HEADS
N_GROUPS = 4
E_PER_GROUP = 4
N_EXPERTS = N_GROUPS * E_PER_GROUP
D_EXPERT = D_MODEL // 4
LN_EPS = 1e-5
DN_ALPHA = (2 * DEPTH) ** 0.25
OFF_S5 = RW_IN
OFF_CV = OFF_S5 + C_BR
OFF_GM = OFF_CV + 2 * C_BR
OFF_GATE = OFF_GM + 2 * C_BR
N_IN = OFF_GATE + N_BRANCH * D_MODEL
P_RW = 0
P_CV = P_RW + RW_IN
P_GM = P_CV + 2 * C_BR
P_S5 = P_GM + 2 * C_BR
N_MIX = P_S5 + C_BR

LANES = 128
VMEM_LIMIT = 56 * 1024 * 1024

F32 = jnp.float32
BF16 = jnp.bfloat16
MM_DTYPE = jnp.bfloat16
HI = lax.Precision.HIGHEST
NEG_BIG = -1e30


def _mm(a, b):
    return jnp.dot(a.astype(MM_DTYPE), b.astype(MM_DTYPE), preferred_element_type=F32)


def _split_bf16(a):
    hi = a.astype(BF16)
    return hi, (a - hi.astype(F32)).astype(BF16)


_NN = ((1,), (0,))
_NT = ((1,), (1,))
_TN = ((0,), (0,))


def _dot(a, b, mode, dims=_NN, exact=None):
    dn = (dims, ((), ()))
    if mode == 'hi':
        return lax.dot_general(a, b, dn, precision=HI, preferred_element_type=F32)
    f = lambda x, y: lax.dot_general(x, y, dn, preferred_element_type=F32)
    if mode == 'bf16':
        return f(a.astype(BF16), b.astype(BF16))
    assert mode == 'x3'
    if exact == 'a':
        b_hi, b_lo = _split_bf16(b)
        a = a.astype(BF16)
        return f(a, b_hi) + f(a, b_lo)
    if exact == 'b':
        a_hi, a_lo = _split_bf16(a)
        b = b.astype(BF16)
        return f(a_hi, b) + f(a_lo, b)
    a_hi, a_lo = _split_bf16(a)
    b_hi, b_lo = _split_bf16(b)
    return f(a_hi, b_hi) + (f(a_hi, b_lo) + f(a_lo, b_hi))


PREC = dict(cumsum='x3', headsum='x3', amat='bf16', inv='bf16', state='bf16', apply='bf16', update='bf16', route='hi')


def _sigmoid(x):
    return jax.nn.sigmoid(x)


def _softplus(x):
    return jnp.maximum(x, 0.0) + jnp.log1p(jnp.exp(-jnp.abs(x)))


def _gelu_tanh(x):
    return 0.5 * x * (1.0 + jnp.tanh(math.sqrt(2.0 / math.pi) * (x + 0.044715 * (x * x * x))))


def _layer_norm(x, g, b):
    mu = jnp.mean(x, axis=-1, keepdims=True)
    d = x - mu
    var = jnp.mean(d * d, axis=-1, keepdims=True)
    return d * lax.rsqrt(var + LN_EPS) * g + b


def _params(sem):
    return pltpu.CompilerParams(dimension_semantics=sem, vmem_limit_bytes=VMEM_LIMIT)


def _full(shape):
    nd = len(shape)
    return pl.BlockSpec(shape, lambda *_: (0,) * nd)


def _inproj_kernel(x_ref, w_ref, z_ref):
    z_ref[...] = _mm(x_ref[...], w_ref[...])


def _inproj(x2d, w_bf16):
    n = x2d.shape[0]
    tm = 512
    return pl.pallas_call(
        _inproj_kernel,
        grid=(n // tm,),
        in_specs=[pl.BlockSpec((tm, D_MODEL), lambda i: (i, 0)), _full((D_MODEL, N_MIX))],
        out_specs=pl.BlockSpec((tm, N_MIX), lambda i: (i, 0)),
        out_shape=jax.ShapeDtypeStruct((n, N_MIX), F32),
        compiler_params=_params(("parallel",)),
        name="inproj",
    )(x2d, w_bf16)


def _heads_bd(x, lane_head):
    return jnp.concatenate([jnp.where(lane_head == h, x, 0.0) for h in range(RW_HEADS)], axis=0)


def _state_bd(wkv):
    zeros_blk = jnp.zeros((RW_HEAD, RW_HEAD), F32)
    return jnp.concatenate(
        [jnp.concatenate([wkv[h] if g == h else zeros_blk for g in range(RW_HEADS)], axis=1)
         for h in range(RW_HEADS)], axis=0)


def _rwkv_prep(z, z_prev, valid, mu_ref, w0_ref, w2_ref, a0_ref, a2_ref, g2_ref, kkw_ref, kaw_ref, hsum, tri_ref,
               same_ref):
    zs = z + mu_ref[...] * (z_prev - z)
    r = zs[:, 0:C_BR]
    k = zs[:, C_BR:2 * C_BR]
    v = zs[:, 2 * C_BR:3 * C_BR]
    lwla = zs[:, 3 * C_BR:3 * C_BR + RW_LW + RW_LA]
    lg = zs[:, 3 * C_BR + RW_LW + RW_LA:]
    w_log = -_softplus(-(w0_ref[...] + _mm(jnp.tanh(lwla), w2_ref[...]))) - 0.5
    ld = -jnp.exp(w_log)
    a = _sigmoid(a0_ref[...] + _mm(lwla, a2_ref[...]))
    g = _mm(_sigmoid(lg), g2_ref[...])
    kk = k * kkw_ref[...]
    kk = kk * lax.rsqrt(jnp.maximum(_dot(kk * kk, hsum, PREC['headsum'], exact='b'), 1e-24))
    k2 = k * (1.0 + (a - 1.0) * kaw_ref[...])
    bv = kk * a
    if valid is not None:
        ld = jnp.where(valid, ld, 0.0)
        k2 = jnp.where(valid, k2, 0.0)
        v = jnp.where(valid, v, 0.0)
        bv = jnp.where(valid, bv, 0.0)
    lc = _dot(tri_ref[...], ld, PREC['cumsum'], exact='a')
    lend = _dot(same_ref[...], ld, PREC['cumsum'], exact='a')
    e_end = jnp.exp(lend - lc)
    e_neg = jnp.exp(-lc)
    return dict(r=r, k2=k2, v=v, g=g, rt=r * jnp.exp(lc), kkt=kk * jnp.exp(lc - ld), kh=k2 * e_neg, bh=bv * e_neg,
                kw=k2 * e_end, bw=bv * e_end, wc=jnp.exp(lend))


def _rwkv_chunk(s, kkt, rt, kh, bh, vv, kw, bw, wc, strict_ref, incl_ref, lvl_ref, chunk):
    hc = RW_HEADS * chunk
    lhs = jnp.concatenate([kkt, rt], axis=0)
    amat = _dot(lhs, jnp.concatenate([kh, bh], axis=0), PREC['amat'], _NT)
    strict = strict_ref[...] != 0.0
    incl = incl_ref[...] != 0.0
    a_kk = jnp.where(strict, amat[0:hc, 0:hc], 0.0)
    a_kb = jnp.where(strict, amat[0:hc, hc:2 * hc], 0.0)
    a_rk = jnp.where(incl, amat[hc:2 * hc, 0:hc], 0.0)
    a_rb = jnp.where(incl, amat[hc:2 * hc, hc:2 * hc], 0.0)
    ri = lax.broadcasted_iota(jnp.int32, (hc, hc), 0)
    cj = lax.broadcasted_iota(jnp.int32, (hc, hc), 1)
    t_inv = jnp.where(ri == cj, 1.0, 0.0) - jnp.where(lvl_ref[0] != 0.0, a_kb, 0.0)
    for lv in range(1, lvl_ref.shape[0]):
        off = jnp.where(lvl_ref[lv] != 0.0, a_kb, 0.0)
        t_inv = t_inv - _dot(_dot(t_inv, off, PREC['inv']), t_inv, PREC['inv'])
    ls = _dot(lhs, s, PREC['state'], _NT)
    av = _dot(jnp.concatenate([a_kk, a_rk], axis=0), vv, PREC['apply'])
    u = _dot(t_inv, ls[0:hc] + av[0:hc], PREC['apply'])
    o = ls[hc:2 * hc] + av[hc:2 * hc] - _dot(a_rb, u, PREC['apply'])
    s_new = s * wc + _dot(jnp.concatenate([vv, u], axis=0), jnp.concatenate([kw, -bw], axis=0),
                          PREC['update'], _TN)
    o_c = o[0:chunk]
    for h in range(1, RW_HEADS):
        o_c = o_c + o[h * chunk:(h + 1) * chunk]
    return s_new, o_c


def _rwkv_post(o, pre, rk_ref, gng_ref, gnb_ref, hsum):
    inv_n = 1.0 / RW_HEAD
    o_mu = _dot(o, hsum, PREC['headsum'], exact='b') * inv_n
    od = o - o_mu
    o_var = _dot(od * od, hsum, PREC['headsum'], exact='b') * inv_n
    on = od * lax.rsqrt(o_var + RW_GN_EPS) * gng_ref[...] + gnb_ref[...]
    bonus = _dot(pre['r'] * pre['k2'] * rk_ref[...], hsum, PREC['headsum'], exact='b') * pre['v']
    return (on + bonus) * pre['g']


_RWKV_TILE_KEYS = ('kkt', 'rt', 'kh', 'bh', 'vv', 'kw', 'bw')


def _rwkv_kernel(z_ref, prev0_ref, wkv0_ref, mu_ref, w0_ref, w2_ref, a0_ref, a2_ref, g2_ref, kkw_ref, kaw_ref,
                 rk_ref, gng_ref, gnb_ref, hsum_ref, tri_ref, same_ref, strict_ref, incl_ref, lvl_ref,
                 y_ref, wkv1_ref,
                 s_scr, prev_scr, kkt_scr, rt_scr, kh_scr, bh_scr, vv_scr, kw_scr, bw_scr, o_scr, wc_scr,
                 *, t_tile, chunk, t_valid, n_tiles, carry):
    i = pl.program_id(1)
    z = z_ref[...]
    row = lax.broadcasted_iota(jnp.int32, (t_tile, 1), 0)
    if carry:
        assert t_valid == t_tile * n_tiles

        @pl.when(i == 0)
        def _():
            s_scr[...] = _state_bd(wkv0_ref)
            prev_scr[...] = prev0_ref[...]

        z_prev = jnp.where(row == 0, prev_scr[...], pltpu.roll(z, 1, 0))
        prev_scr[...] = z[t_tile - 1:t_tile, :]
        valid = None
    else:
        step = row % chunk
        z_prev = jnp.where(step == 0, prev0_ref[...], pltpu.roll(z, 1, 0))
        valid = step < t_valid
    hsum = hsum_ref[...]
    pre = _rwkv_prep(z, z_prev, valid, mu_ref, w0_ref, w2_ref, a0_ref, a2_ref, g2_ref, kkw_ref, kaw_ref, hsum,
                     tri_ref, same_ref)
    pre['vv'] = pre['v']
    tile_scr = dict(zip(_RWKV_TILE_KEYS, (kkt_scr, rt_scr, kh_scr, bh_scr, vv_scr, kw_scr, bw_scr)))
    for key, scr in tile_scr.items():
        scr[...] = pre[key]
    wc_scr[...] = pre['wc']
    lane_head = lax.broadcasted_iota(jnp.int32, (chunk, C_BR), 1) // RW_HEAD

    def chunk_body(c, _):
        sl = pl.ds(pl.multiple_of(c * chunk, chunk), chunk)
        ops = [_heads_bd(tile_scr[key][sl, :], lane_head) for key in _RWKV_TILE_KEYS]
        s = s_scr[...] if carry else _state_bd(wkv0_ref.at[c])
        s_new, o_c = _rwkv_chunk(s, *ops, wc_scr[pl.ds(c * chunk, 1), :], strict_ref, incl_ref, lvl_ref, chunk)
        o_scr[sl, :] = o_c
        if carry:
            s_scr[...] = s_new
        else:
            for h in range(RW_HEADS):
                wkv1_ref[c, h] = s_new[h * RW_HEAD:(h + 1) * RW_HEAD, h * RW_HEAD:(h + 1) * RW_HEAD]
        return 0

    lax.fori_loop(0, t_tile // chunk, chunk_body, 0)
    y_ref[...] = _rwkv_post(o_scr[...], pre, rk_ref, gng_ref, gnb_ref, hsum)

    if carry:
        @pl.when(i == n_tiles - 1)
        def _():
            for h in range(RW_HEADS):
                wkv1_ref[h] = s_scr[h * RW_HEAD:(h + 1) * RW_HEAD, h * RW_HEAD:(h + 1) * RW_HEAD]


def _rwkv(z3d, col_blk, prev0, wkv0, pw, *, t_tile, chunk, t_valid, carry):
    bsz, t_len, _ = z3d.shape
    n_tiles = t_len // t_tile
    idx = np.arange(t_tile)
    same = (idx[:, None] // chunk == idx[None, :] // chunk)
    tri = jnp.asarray((same & (idx[None, :] <= idx[:, None])).astype(np.float32))
    same = jnp.asarray(same.astype(np.float32))
    hid = np.arange(C_BR) // RW_HEAD
    hsum = jnp.asarray((hid[:, None] == hid[None, :]).astype(np.float32))
    hc = RW_HEADS * chunk
    hh, tt = np.arange(hc) // chunk, np.arange(hc) % chunk
    same_head = hh[:, None] == hh[None, :]
    strict = jnp.asarray((same_head & (tt[None, :] < tt[:, None])).astype(np.float32))
    incl = jnp.asarray((same_head & (tt[None, :] <= tt[:, None])).astype(np.float32))
    lvls = []
    m = 1
    while m < chunk:
        lvls.append(same_head & (tt[:, None] // (2 * m) == tt[None, :] // (2 * m))
                    & (tt[:, None] % (2 * m) >= m) & (tt[None, :] % (2 * m) < m))
        m *= 2
    lvl = jnp.asarray(np.stack(lvls).astype(np.float32))
    vec = lambda n: _full((1, n))
    tile_scr = pltpu.VMEM((t_tile, C_BR), F32)
    if carry:
        prev_spec = pl.BlockSpec((None, 1, RW_IN), lambda b, i: (b, 0, 0))
        wkv_spec = pl.BlockSpec((None, RW_HEADS, RW_HEAD, RW_HEAD), lambda b, i: (b, 0, 0, 0))
    else:
        assert n_tiles == 1
        prev_spec = pl.BlockSpec((None, t_tile, RW_IN), lambda b, i: (b, 0, 0))
        wkv_spec = pl.BlockSpec((None, t_tile // chunk, RW_HEADS, RW_HEAD, RW_HEAD), lambda b, i: (b, 0, 0, 0, 0))
    kern = functools.partial(_rwkv_kernel, t_tile=t_tile, chunk=chunk, t_valid=t_valid, n_tiles=n_tiles, carry=carry)
    return pl.pallas_call(
        kern,
        grid=(bsz, n_tiles),
        in_specs=[pl.BlockSpec((None, t_tile, RW_IN), lambda b, i: (b, i, col_blk)), prev_spec, wkv_spec,
                  vec(RW_IN), vec(C_BR), _full((RW_LW + RW_LA, C_BR)), vec(C_BR), _full((RW_LW + RW_LA, C_BR)),
                  _full((RW_LG, C_BR)), vec(C_BR), vec(C_BR), vec(C_BR), vec(C_BR), vec(C_BR),
                  _full((C_BR, C_BR)), _full((t_tile, t_tile)), _full((t_tile, t_tile)),
                  _full((hc, hc)), _full((hc, hc)), _full(lvl.shape)],
        out_specs=[pl.BlockSpec((None, t_tile, C_BR), lambda b, i: (b, i, 0)), wkv_spec],
        out_shape=[jax.ShapeDtypeStruct((bsz, t_len, C_BR), F32), jax.ShapeDtypeStruct(wkv0.shape, F32)],
        scratch_shapes=[pltpu.VMEM((C_BR, C_BR), F32), pltpu.VMEM((1, RW_IN), F32)] + [tile_scr] * 9,
        compiler_params=_params(("parallel", "arbitrary")),
        name="rwkv7",
    )(z3d, prev0, wkv0, pw['mu'], pw['w0'], pw['w2'], pw['a0'], pw['a2'], pw['g2'], pw['kk'], pw['ka'],
      pw['rk'], pw['gn_g'], pw['gn_b'], hsum, tri, same, strict, incl, lvl)


def _s5_kernel(u_ref, h0_ref, lbr_ref, lbi_ref, bb_ref, cc_ref, d_ref, gw_ref, gb_ref,
               y_ref, h1_ref, bu_scr, h_scr, *, bsz, t_tile, n_tiles):
    i = pl.program_id(0)

    @pl.when(i == 0)
    def _():
        h_scr[...] = h0_ref[...]

    rows = bsz * t_tile
    n_lt = S5_N // LANES
    u = u_ref[...].reshape(rows, C_BR)
    bu = _mm(u, bb_ref[...])
    for j in range(2 * n_lt):
        bu_scr[j] = bu[:, j * LANES:(j + 1) * LANES]
    lane_tile = lambda ref, j: ref[:, j * LANES:(j + 1) * LANES]
    lbr = [jnp.broadcast_to(lane_tile(lbr_ref, j), (bsz, LANES)) for j in range(n_lt)]
    lbi = [jnp.broadcast_to(lane_tile(lbi_ref, j), (bsz, LANES)) for j in range(n_lt)]

    def step(t, carry):
        hr, hi = carry
        sl = pl.ds(t, bsz, stride=t_tile)
        new_r, new_i = [], []
        for j in range(n_lt):
            nr = lbr[j] * hr[j] - lbi[j] * hi[j] + bu_scr[j, sl, :]
            ni = lbr[j] * hi[j] + lbi[j] * hr[j] + bu_scr[n_lt + j, sl, :]
            bu_scr[j, sl, :] = nr
            bu_scr[n_lt + j, sl, :] = ni
            new_r.append(nr)
            new_i.append(ni)
        return tuple(new_r), tuple(new_i)

    h_init = (tuple(lane_tile(h_scr, j) for j in range(n_lt)),
              tuple(lane_tile(h_scr, n_lt + j) for j in range(n_lt)))
    hr, hi = lax.fori_loop(0, t_tile, step, h_init)
    for j in range(n_lt):
        h_scr[:, j * LANES:(j + 1) * LANES] = hr[j]
        h_scr[:, (n_lt + j) * LANES:(n_lt + j + 1) * LANES] = hi[j]

    hs = jnp.concatenate([bu_scr[j] for j in range(2 * n_lt)], axis=1)
    y = _mm(hs, cc_ref[...]) + d_ref[...] * u
    y = _gelu_tanh(y)
    y = y * _sigmoid(_mm(y, gw_ref[...]) + gb_ref[...])
    y_ref[...] = y.reshape(y_ref.shape)

    @pl.when(i == n_tiles - 1)
    def _():
        h1_ref[...] = h_scr[...]


def _s5(z, col_blk, h0, ps, *, bsz, t_len, t_tile):
    n_tiles = t_len // t_tile
    rows = bsz * t_tile
    if z.ndim == 3:
        u_spec = pl.BlockSpec((bsz, t_tile, C_BR), lambda i: (0, i, col_blk))
        y_spec = pl.BlockSpec((bsz, t_tile, C_BR), lambda i: (0, i, 0))
        y_shape = (bsz, t_len, C_BR)
    else:
        assert n_tiles == 1
        u_spec = pl.BlockSpec((rows, C_BR), lambda i: (0, col_blk))
        y_spec = pl.BlockSpec((rows, C_BR), lambda i: (0, 0))
        y_shape = (rows, C_BR)
    kern = functools.partial(_s5_kernel, bsz=bsz, t_tile=t_tile, n_tiles=n_tiles)
    return pl.pallas_call(
        kern,
        grid=(n_tiles,),
        in_specs=[u_spec, _full((bsz, 2 * S5_N)), _full((1, S5_N)), _full((1, S5_N)),
                  _full((C_BR, 2 * S5_N)), _full((2 * S5_N, C_BR)), _full((1, C_BR)),
                  _full((C_BR, C_BR)), _full((1, C_BR))],
        out_specs=[y_spec, _full((bsz, 2 * S5_N))],
        out_shape=[jax.ShapeDtypeStruct(y_shape, F32), jax.ShapeDtypeStruct((bsz, 2 * S5_N), F32)],
        scratch_shapes=[pltpu.VMEM((2 * S5_N // LANES, rows, LANES), F32), pltpu.VMEM((bsz, 2 * S5_N), F32)],
        compiler_params=_params(("arbitrary",)),
        name="s5",
    )(z, h0, ps['lb_re'], ps['lb_im'], ps['bb'], ps['cc'], ps['d'], ps['glu_w'], ps['glu_b'])


def _conv_kernel(z_ref, c0_ref, w_ref, b_ref, g_ref, be_ref, y_ref, c1_ref, full_scr, *, t_tile, n_tiles):
    i = pl.program_id(1)
    lo = CONV_HIST_PAD - CONV_HIST

    @pl.when(i == 0)
    def _():
        full_scr[0:lo, :] = jnp.zeros((lo, C_BR), F32)
        full_scr[lo:CONV_HIST_PAD, :] = c0_ref[...]

    z = z_ref[...]
    c = z[:, 0:C_BR] * _sigmoid(z[:, C_BR:2 * C_BR])
    full_scr[CONV_HIST_PAD:CONV_HIST_PAD + t_tile, :] = c
    acc = jnp.zeros((t_tile, C_BR), F32) + b_ref[...]
    for j in range(CONV_W):
        acc = acc + full_scr[lo + j:lo + j + t_tile, :] * w_ref[j:j + 1, :]
    y = _layer_norm(acc, g_ref[...], be_ref[...])
    y_ref[...] = y * _sigmoid(y)
    hist = full_scr[t_tile:t_tile + CONV_HIST_PAD, :]
    full_scr[0:CONV_HIST_PAD, :] = hist

    @pl.when(i == n_tiles - 1)
    def _():
        c1_ref[...] = hist[lo:, :]


def _conv(z3d, col_blk, conv0, pc, *, t_tile):
    bsz, t_len, _ = z3d.shape
    n_tiles = t_len // t_tile
    kern = functools.partial(_conv_kernel, t_tile=t_tile, n_tiles=n_tiles)
    return pl.pallas_call(
        kern,
        grid=(bsz, n_tiles),
        in_specs=[pl.BlockSpec((None, t_tile, 2 * C_BR), lambda b, i: (b, i, col_blk)),
                  pl.BlockSpec((None, CONV_HIST, C_BR), lambda b, i: (b, 0, 0)),
                  _full((CONV_W, C_BR)), _full((1, C_BR)), _full((1, C_BR)), _full((1, C_BR))],
        out_specs=[pl.BlockSpec((None, t_tile, C_BR), lambda b, i: (b, i, 0)),
                   pl.BlockSpec((None, CONV_HIST, C_BR), lambda b, i: (b, 0, 0))],
        out_shape=[jax.ShapeDtypeStruct((bsz, t_len, C_BR), F32),
                   jax.ShapeDtypeStruct((bsz, CONV_HIST, C_BR), F32)],
        scratch_shapes=[pltpu.VMEM((CONV_HIST_PAD + t_tile, C_BR), F32)],
        compiler_params=_params(("parallel", "arbitrary")),
        name="conv",
    )(z3d, conv0, pc['w'], pc['b'], pc['ln_g'], pc['ln_b'])


def _gmlp_kernel(z_ref, g_ref, b_ref, wm_ref, bias_ref, y_ref, v_ref):
    z = z_ref[...]
    u = z[:, 0:C_BR]
    v = _layer_norm(z[:, C_BR:2 * C_BR], g_ref[...], b_ref[...])
    v_ref[...] = v
    head = lax.broadcasted_iota(jnp.int32, (CHUNK, C_BR), 1) // GM_HEAD
    s = bias_ref[...]
    for h in range(GM_HEADS):
        s = s + jnp.where(head == h, _mm(wm_ref[h], v), 0.0)
    y_ref[...] = u * s


def _gmlp(z3d, col_blk, pg, wm, bias):
    bsz, t_len, _ = z3d.shape
    return pl.pallas_call(
        _gmlp_kernel,
        grid=(bsz, t_len // CHUNK),
        in_specs=[pl.BlockSpec((None, CHUNK, 2 * C_BR), lambda b, i: (b, i, col_blk)),
                  _full((1, C_BR)), _full((1, C_BR)), _full((GM_HEADS, CHUNK, CHUNK)), _full((CHUNK, C_BR))],
        out_specs=[pl.BlockSpec((None, CHUNK, C_BR), lambda b, i: (b, i, 0)),
                   pl.BlockSpec((None, CHUNK, C_BR), lambda b, i: (b, i, 0))],
        out_shape=[jax.ShapeDtypeStruct((bsz, t_len, C_BR), F32)] * 2,
        compiler_params=_params(("parallel", "parallel")),
        name="gmlp",
    )(z3d, pg['ln_g'], pg['ln_b'], wm, bias)


def _merge_kernel(x_ref, yrw_ref, ys5_ref, ycv_ref, ygm_ref, wg_ref, wb_ref, wo_ref, g_ref, b_ref, o_ref):
    x = x_ref[...]
    xb = x.astype(MM_DTYPE)
    merged = None
    for bidx, y_ref in enumerate((yrw_ref, ys5_ref, ycv_ref, ygm_ref)):
        gate = _sigmoid(jnp.dot(xb, wg_ref[:, bidx * D_MODEL:(bidx + 1) * D_MODEL], preferred_element_type=F32))
        term = gate * _mm(y_ref[...], wb_ref[bidx])
        merged = term if merged is None else merged + term
    o_ref[...] = _layer_norm(DN_ALPHA * x + _mm(merged, wo_ref[...]), g_ref[...], b_ref[...])


def _merge(x2d, ys, pm):
    n = x2d.shape[0]
    tm = 256
    row = lambda w: pl.BlockSpec((tm, w), lambda i: (i, 0))
    return pl.pallas_call(
        _merge_kernel,
        grid=(n // tm,),
        in_specs=[row(D_MODEL), row(C_BR), row(C_BR), row(C_BR), row(C_BR),
                  _full((D_MODEL, N_BRANCH * D_MODEL)), _full((N_BRANCH, C_BR, D_MODEL)),
                  _full((D_MODEL, D_MODEL)), _full((1, D_MODEL)), _full((1, D_MODEL))],
        out_specs=row(D_MODEL),
        out_shape=jax.ShapeDtypeStruct((n, D_MODEL), F32),
        compiler_params=_params(("parallel",)),
        name="merge",
    )(x2d, *ys, pm['w_gate'], pm['w_branch'], pm['w_out'], pm['ln1_g'], pm['ln1_b'])


def _moe_kernel(x_ref, wg_ref, bg_ref, wu_ref, wd_ref, g_ref, b_ref, o_ref, xb_scr, comb_scr, acc_scr, *, tm):
    e = pl.program_id(1)
    lane = lax.broadcasted_iota(jnp.int32, (tm, LANES), 1)

    @pl.when(e == 0)
    def _():
        x = x_ref[...]
        xb_scr[...] = x.astype(MM_DTYPE)
        acc_scr[...] = jnp.zeros_like(acc_scr)
        logits = _dot(x, wg_ref[...], PREC['route']) + bg_ref[...]
        gl = jnp.where(lane < N_GROUPS, logits, NEG_BIG)
        gmax = jnp.max(gl, axis=-1, keepdims=True)
        g_sel = jnp.min(jnp.where(gl == gmax, lane, LANES), axis=-1, keepdims=True)
        p_group = 1.0 / jnp.sum(jnp.where(lane < N_GROUPS, jnp.exp(gl - gmax), 0.0), axis=-1, keepdims=True)
        first = N_GROUPS + g_sel * E_PER_GROUP
        el = jnp.where((lane >= first) & (lane < first + E_PER_GROUP), logits, NEG_BIG)
        m1 = jnp.max(el, axis=-1, keepdims=True)
        i1 = jnp.min(jnp.where(el == m1, lane, LANES), axis=-1, keepdims=True)
        el2 = jnp.where(lane == i1, NEG_BIG, el)
        m2 = jnp.max(el2, axis=-1, keepdims=True)
        i2 = jnp.min(jnp.where(el2 == m2, lane, LANES), axis=-1, keepdims=True)
        e2 = jnp.exp(m2 - m1)
        w1 = p_group / (1.0 + e2)
        w2 = p_group * e2 / (1.0 + e2)
        comb_scr[...] = jnp.where(lane == i1, w1, 0.0) + jnp.where(lane == i2, w2, 0.0)

    comb_e = jnp.sum(jnp.where(lane == e + N_GROUPS, comb_scr[...], 0.0), axis=-1, keepdims=True)
    h = jnp.dot(xb_scr[...], wu_ref[...], preferred_element_type=F32)
    h1 = h[:, 0:D_EXPERT]
    hh = h1 * _sigmoid(h1) * h[:, D_EXPERT:2 * D_EXPERT] * comb_e
    acc_scr[...] += _mm(hh, wd_ref[...])

    @pl.when(e == N_EXPERTS - 1)
    def _():
        o_ref[...] = _layer_norm(DN_ALPHA * x_ref[...] + acc_scr[...], g_ref[...], b_ref[...])


def _moe(x2d, pe):
    n = x2d.shape[0]
    tm = 512
    kern = functools.partial(_moe_kernel, tm=tm)
    return pl.pallas_call(
        kern,
        grid=(n // tm, N_EXPERTS),
        in_specs=[pl.BlockSpec((tm, D_MODEL), lambda i, e: (i, 0)),
                  _full((D_MODEL, LANES)), _full((1, LANES)),
                  pl.BlockSpec((None, D_MODEL, 2 * D_EXPERT), lambda i, e: (e, 0, 0)),
                  pl.BlockSpec((None, D_EXPERT, D_MODEL), lambda i, e: (e, 0, 0)),
                  _full((1, D_MODEL)), _full((1, D_MODEL))],
        out_specs=pl.BlockSpec((tm, D_MODEL), lambda i, e: (i, 0)),
        out_shape=jax.ShapeDtypeStruct((n, D_MODEL), F32),
        scratch_shapes=[pltpu.VMEM((tm, D_MODEL), MM_DTYPE), pltpu.VMEM((tm, LANES), F32),
                        pltpu.VMEM((tm, D_MODEL), F32)],
        compiler_params=_params(("parallel", "arbitrary")),
        name="moe",
    )(x2d, pe['wg'], pe['bg'], pe['w_up'], pe['w_down'], pe['ln2_g'], pe['ln2_b'])


def _block_diag(blocks):
    g, m, n = blocks.shape
    eye = jnp.eye(g, dtype=blocks.dtype)
    return (eye[:, None, :, None] * blocks[:, :, None, :]).reshape(g * m, g * n)


def _prep_layer(l, p):
    w_in = p['w_in'][l]
    w_mix = jnp.concatenate([w_in[:, :OFF_S5], w_in[:, OFF_CV:OFF_GM], w_in[:, OFF_GM:OFF_GATE],
                             w_in[:, OFF_S5:OFF_CV]], axis=1).astype(MM_DTYPE)
    row = lambda a: a.reshape(1, -1).astype(F32)
    zeros_lora = jnp.zeros((RW_LW, C_BR), F32)
    rw = dict(mu=row(p['rw_mu'][l]), w0=row(p['rw_w0'][l]),
              w2=jnp.concatenate([p['rw_w2'][l], zeros_lora], axis=0).astype(MM_DTYPE),
              a0=row(p['rw_a0'][l]),
              a2=jnp.concatenate([zeros_lora, p['rw_a2'][l]], axis=0).astype(MM_DTYPE),
              g2=p['rw_g2'][l].astype(MM_DTYPE), kk=row(p['rw_kk'][l]), ka=row(p['rw_ka'][l]),
              rk=row(p['rw_rk'][l]), gn_g=row(p['rw_gn_g'][l]), gn_b=row(p['rw_gn_b'][l]))
    lr, li = p['s5_lam_re'][l].astype(F32), p['s5_lam_im'][l].astype(F32)
    dt = jnp.exp(p['s5_log_dt'][l].astype(F32))[:, None]
    mag = jnp.exp(lr * dt)
    lb_re, lb_im = mag * jnp.cos(li * dt), mag * jnp.sin(li * dt)
    den = lr * lr + li * li
    q_re = ((lb_re - 1.0) * lr + lb_im * li) / den
    q_im = (lb_im * lr - (lb_re - 1.0) * li) / den
    br, bi = p['s5_b_re'][l].astype(F32), p['s5_b_im'][l].astype(F32)
    bb_re = q_re[..., None] * br - q_im[..., None] * bi
    bb_im = q_re[..., None] * bi + q_im[..., None] * br
    bb = jnp.concatenate([_block_diag(jnp.swapaxes(bb_re, 1, 2)), _block_diag(jnp.swapaxes(bb_im, 1, 2))],
                         axis=1).astype(MM_DTYPE)
    cc = jnp.concatenate([_block_diag(jnp.swapaxes(p['s5_c_re'][l].astype(F32), 1, 2)),
                          -_block_diag(jnp.swapaxes(p['s5_c_im'][l].astype(F32), 1, 2))],
                         axis=0).astype(MM_DTYPE)
    s5 = dict(lb_re=lb_re.reshape(1, S5_N), lb_im=lb_im.reshape(1, S5_N), bb=bb, cc=cc, d=row(p['s5_d'][l]),
              glu_w=p['s5_glu_w'][l].astype(MM_DTYPE), glu_b=row(p['s5_glu_b'][l]))
    cv = dict(w=p['cv_w'][l].astype(F32), b=row(p['cv_b'][l]), ln_g=row(p['cv_ln_g'][l]), ln_b=row(p['cv_ln_b'][l]))
    causal = jnp.tril(jnp.ones((CHUNK, CHUNK), dtype=bool))
    wm = jnp.where(causal, p['gm_ws'][l], 0).astype(F32)
    bias = jnp.repeat(jnp.swapaxes(p['gm_bs'][l], 0, 1), GM_HEAD, axis=1).astype(F32)
    gm = dict(ln_g=row(p['gm_ln_g'][l]), ln_b=row(p['gm_ln_b'][l]), wm=wm, bias=bias)
    mg = dict(w_gate=w_in[:, OFF_GATE:].astype(MM_DTYPE), w_branch=p['w_branch'][l].astype(MM_DTYPE),
              w_out=p['w_out'][l].astype(MM_DTYPE), ln1_g=row(p['ln1_g'][l]), ln1_b=row(p['ln1_b'][l]))
    pad = LANES - N_GROUPS - N_EXPERTS
    wg = jnp.concatenate([p['moe_wg1'][l], p['moe_wg2'][l], jnp.zeros((D_MODEL, pad), F32)], axis=1).astype(F32)
    bg = jnp.concatenate([p['moe_bg1'][l], p['moe_bg2'][l], jnp.zeros((pad,), F32)]).reshape(1, LANES).astype(F32)
    moe = dict(wg=wg, bg=bg, w_up=p['moe_w_up'][l].astype(MM_DTYPE), w_down=p['moe_w_down'][l].astype(MM_DTYPE),
               ln2_g=row(p['ln2_g'][l]), ln2_b=row(p['ln2_b'][l]))
    return dict(w_mix=w_mix, rw=rw, s5=s5, cv=cv, gm=gm, mg=mg, moe=moe)


def _gmlp_group(z3d, pg, *, is_prompt):
    bsz, t_len, n_cols = z3d.shape
    if is_prompt:
        return _gmlp(z3d, P_GM // (2 * C_BR), pg, pg['wm'].astype(MM_DTYPE), pg['bias'])
    reps = CHUNK // t_len
    wm = jnp.stack([jnp.kron(jnp.eye(reps, dtype=F32), pg['wm'][h, :t_len, :t_len]) for h in range(GM_HEADS)])
    bias = jnp.tile(pg['bias'][:t_len], (reps, 1))
    y, v = _gmlp(z3d.reshape(1, bsz * t_len, n_cols), P_GM // (2 * C_BR), pg, wm.astype(MM_DTYPE), bias)
    return y.reshape(bsz, t_len, C_BR), v.reshape(bsz, t_len, C_BR)


def _run_group(x, wkv0, shift0, s5r0, s5i0, conv0, layers, *, is_prompt):
    bsz, t_len, _ = x.shape
    n = bsz * t_len
    x2d = x.reshape(n, D_MODEL)
    outs = []
    for l, lp in enumerate(layers):
        z2d = _inproj(x2d, lp['w_mix'])
        z3d = z2d.reshape(bsz, t_len, N_MIX)
        sh0 = shift0[:, l].reshape(bsz, 1, RW_IN)
        if is_prompt:
            y_rw, wkv1 = _rwkv(z3d, P_RW // RW_IN, sh0, wkv0[:, l], lp['rw'], t_tile=256, chunk=64,
                               t_valid=t_len, carry=True)
        else:
            pad = ((0, 0), (0, RW_PAD - t_len), (0, 0))
            grp = lambda a: a.reshape((bsz // RW_GROUP, RW_GROUP * a.shape[1]) + a.shape[2:])
            z_rw = grp(jnp.pad(z3d[:, :, P_RW:P_RW + RW_IN], pad))
            prev0 = grp(jnp.pad(sh0, ((0, 0), (0, RW_PAD - 1), (0, 0))))
            y_rw, wkv1 = _rwkv(z_rw, 0, prev0, grp(wkv0[:, l][:, None]), lp['rw'], t_tile=RW_GROUP * RW_PAD,
                               chunk=RW_PAD, t_valid=t_len, carry=False)
            y_rw = y_rw.reshape(bsz, RW_PAD, C_BR)[:, :t_len]
            wkv1 = wkv1.reshape(bsz, RW_HEADS, RW_HEAD, RW_HEAD)
        shift1 = z3d[:, t_len - 1, P_RW:P_RW + RW_IN]
        h0 = jnp.concatenate([s5r0[:, l].reshape(bsz, S5_N), s5i0[:, l].reshape(bsz, S5_N)], axis=1)
        if is_prompt:
            y_s5, h1 = _s5(z3d, P_S5 // C_BR, h0, lp['s5'], bsz=bsz, t_len=t_len, t_tile=128)
        else:
            y_s5, h1 = _s5(z2d, P_S5 // C_BR, h0, lp['s5'], bsz=bsz, t_len=t_len, t_tile=t_len)
        s5r1 = h1[:, :S5_N].reshape(bsz, S5_GROUPS, S5_STATE)
        s5i1 = h1[:, S5_N:].reshape(bsz, S5_GROUPS, S5_STATE)
        y_cv, conv1 = _conv(z3d, P_CV // (2 * C_BR), conv0[:, l], lp['cv'], t_tile=256 if is_prompt else t_len)
        y_gm, v_gm = _gmlp_group(z3d, lp['gm'], is_prompt=is_prompt)
        ys = [y.reshape(n, C_BR) for y in (y_rw, y_s5, y_cv, y_gm)]
        x2d = _merge(x2d, ys, lp['mg'])
        x2d = _moe(x2d, lp['moe'])
        outs.append((wkv1, shift1, s5r1, s5i1, conv1, v_gm))
    states = tuple(jnp.stack([o[i] for o in outs], axis=1) for i in range(6))
    return x2d.reshape(bsz, t_len, D_MODEL), states


def kernel(x_prompt, x_sample, state_rwkv_wkv, state_rwkv_shift, state_s5_re, state_s5_im, cache_conv,
           w_in, rw_mu, rw_w0, rw_w2, rw_a0, rw_a2, rw_g2, rw_kk, rw_ka, rw_rk, rw_gn_g, rw_gn_b,
           s5_lam_re, s5_lam_im, s5_log_dt, s5_b_re, s5_b_im, s5_c_re, s5_c_im, s5_d, s5_glu_w, s5_glu_b,
           cv_w, cv_b, cv_ln_g, cv_ln_b, gm_ln_g, gm_ln_b, gm_ws, gm_bs,
           w_branch, w_out, ln1_g, ln1_b,
           moe_wg1, moe_bg1, moe_wg2, moe_bg2, moe_w_up, moe_w_down, ln2_g, ln2_b):
    p = dict(w_in=w_in, rw_mu=rw_mu, rw_w0=rw_w0, rw_w2=rw_w2, rw_a0=rw_a0, rw_a2=rw_a2, rw_g2=rw_g2,
             rw_kk=rw_kk, rw_ka=rw_ka, rw_rk=rw_rk, rw_gn_g=rw_gn_g, rw_gn_b=rw_gn_b,
             s5_lam_re=s5_lam_re, s5_lam_im=s5_lam_im, s5_log_dt=s5_log_dt, s5_b_re=s5_b_re, s5_b_im=s5_b_im,
             s5_c_re=s5_c_re, s5_c_im=s5_c_im, s5_d=s5_d, s5_glu_w=s5_glu_w, s5_glu_b=s5_glu_b,
             cv_w=cv_w, cv_b=cv_b, cv_ln_g=cv_ln_g, cv_ln_b=cv_ln_b, gm_ln_g=gm_ln_g, gm_ln_b=gm_ln_b,
             gm_ws=gm_ws, gm_bs=gm_bs, w_branch=w_branch, w_out=w_out, ln1_g=ln1_g, ln1_b=ln1_b,
             moe_wg1=moe_wg1, moe_bg1=moe_bg1, moe_wg2=moe_wg2, moe_bg2=moe_bg2, moe_w_up=moe_w_up,
             moe_w_down=moe_w_down, ln2_g=ln2_g, ln2_b=ln2_b)
    layers = [_prep_layer(l, p) for l in range(DEPTH)]
    bp = x_prompt.shape[0]
    dt = x_prompt.dtype
    y_prompt, (p_wkv, p_shift, p_s5r, p_s5i, p_conv, _) = _run_group(
        x_prompt,
        jnp.zeros((bp, DEPTH, RW_HEADS, RW_HEAD, RW_HEAD), dt),
        jnp.zeros((bp, DEPTH, RW_IN), dt),
        jnp.zeros((bp, DEPTH, S5_GROUPS, S5_STATE), dt),
        jnp.zeros((bp, DEPTH, S5_GROUPS, S5_STATE), dt),
        jnp.zeros((bp, DEPTH, CONV_HIST, C_BR), dt),
        layers, is_prompt=True)
    y_sample, (s_wkv, s_shift, s_s5r, s_s5i, s_conv, s_gmv) = _run_group(
        x_sample, state_rwkv_wkv, state_rwkv_shift, state_s5_re, state_s5_im, cache_conv, layers,
        is_prompt=False)
    return (y_prompt, y_sample, p_wkv, p_shift, p_s5r, p_s5i, p_conv,
            s_wkv, s_shift, s_s5r, s_s5i, s_conv, s_gmv)
```

```python
import functools
import math

import numpy as np
import jax
import jax.numpy as jnp
from jax import lax
from jax.experimental import pallas as pl
from jax.experimental.pallas import tpu as pltpu

D_MODEL = 1024
DEPTH = 4
N_BRANCH = 4
C_BR = D_MODEL // 4
RW_HEAD = 64
RW_HEADS = C_BR // RW_HEAD
RW_LW = 64
RW_LA = 64
RW_LG = 128
RW_IN = 3 * C_BR + RW_LW + RW_LA + RW_LG
RW_GN_EPS = 64e-5
RW_PAD = 8
RW_GROUP = 8
S5_GW = 16
S5_GROUPS = C_BR // S5_GW
S5_STATE = 64
S5_N = S5_GROUPS * S5_STATE
CONV_W = 31
CONV_HIST = CONV_W - 1
CONV_HIST_PAD = 32
CV_GROUP = 16
CHUNK = 128
GM_TILE = 4 * CHUNK
GM_HEADS = 4
GM_HEAD = C_BR // GM_HEADS
N_GROUPS = 4
E_PER_GROUP = 4
N_EXPERTS = N_GROUPS * E_PER_GROUP
D_EXPERT = D_MODEL // 4
LN_EPS = 1e-5
DN_ALPHA = (2 * DEPTH) ** 0.25
OFF_S5 = RW_IN
OFF_CV = OFF_S5 + C_BR
OFF_GM = OFF_CV + 2 * C_BR
OFF_GATE = OFF_GM + 2 * C_BR
N_IN = OFF_GATE + N_BRANCH * D_MODEL
P_RW = 0
P_CV = P_RW + RW_IN
P_GM = P_CV + 2 * C_BR
P_S5 = P_GM + 2 * C_BR
N_MIX = P_S5 + C_BR

LANES = 128
VMEM_LIMIT = 56 * 1024 * 1024

F32 = jnp.float32
BF16 = jnp.bfloat16
MM_DTYPE = jnp.bfloat16
HI = lax.Precision.HIGHEST
NEG_BIG = -1e30


def _mm(a, b):
    return jnp.dot(a.astype(MM_DTYPE), b.astype(MM_DTYPE), preferred_element_type=F32)


def _split_bf16(a):
    hi = a.astype(BF16)
    return hi, (a - hi.astype(F32)).astype(BF16)


_NN = ((1,), (0,))
_NT = ((1,), (1,))
_TN = ((0,), (0,))


def _dot(a, b, mode, dims=_NN, exact=None):
    dn = (dims, ((), ()))
    if mode == 'hi':
        return lax.dot_general(a, b, dn, precision=HI, preferred_element_type=F32)
    f = lambda x, y: lax.dot_general(x, y, dn, preferred_element_type=F32)
    if mode == 'bf16':
        return f(a.astype(BF16), b.astype(BF16))
    assert mode == 'x3'
    if exact == 'a':
        b_hi, b_lo = _split_bf16(b)
        a = a.astype(BF16)
        return f(a, b_hi) + f(a, b_lo)
    if exact == 'b':
        a_hi, a_lo = _split_bf16(a)
        b = b.astype(BF16)
        return f(a_hi, b) + f(a_lo, b)
    a_hi, a_lo = _split_bf16(a)
    b_hi, b_lo = _split_bf16(b)
    return f(a_hi, b_hi) + (f(a_hi, b_lo) + f(a_lo, b_hi))


PREC = dict(cumsum='x3', headsum='x3', amat='bf16', inv='bf16', state='bf16', apply='bf16', update='bf16', route='x3')


def _sigmoid(x):
    return jax.nn.sigmoid(x)


def _softplus(x):
    return jnp.maximum(x, 0.0) + jnp.log1p(jnp.exp(-jnp.abs(x)))


def _gelu_tanh(x):
    return 0.5 * x * (1.0 + jnp.tanh(math.sqrt(2.0 / math.pi) * (x + 0.044715 * (x * x * x))))


def _layer_norm(x, g, b):
    mu = jnp.mean(x, axis=-1, keepdims=True)
    d = x - mu
    var = jnp.mean(d * d, axis=-1, keepdims=True)
    return d * lax.rsqrt(var + LN_EPS) * g + b


def _params(sem):
    return pltpu.CompilerParams(dimension_semantics=sem, vmem_limit_bytes=VMEM_LIMIT)


def _full(shape):
    nd = len(shape)
    return pl.BlockSpec(shape, lambda *_: (0,) * nd)


def _inproj_kernel(x_ref, w_ref, z_ref):
    z_ref[...] = _mm(x_ref[...], w_ref[...])


def _inproj(x2d, w_bf16):
    n = x2d.shape[0]
    tm = 512
    return pl.pallas_call(
        _inproj_kernel,
        grid=(n // tm,),
        in_specs=[pl.BlockSpec((tm, D_MODEL), lambda i: (i, 0)), _full((D_MODEL, N_MIX))],
        out_specs=pl.BlockSpec((tm, N_MIX), lambda i: (i, 0)),
        out_shape=jax.ShapeDtypeStruct((n, N_MIX), F32),
        compiler_params=_params(("parallel",)),
        name="inproj",
    )(x2d, w_bf16)


def _heads_bd(x, lane_head):
    return jnp.concatenate([jnp.where(lane_head == h, x, 0.0) for h in range(RW_HEADS)], axis=0)


def _state_bd(wkv):
    zeros_blk = jnp.zeros((RW_HEAD, RW_HEAD), F32)
    return jnp.concatenate(
        [jnp.concatenate([wkv[h] if g == h else zeros_blk for g in range(RW_HEADS)], axis=1)
         for h in range(RW_HEADS)], axis=0)


def _rwkv_prep(z, z_prev, valid, mu_ref, w0_ref, w2_ref, a0_ref, a2_ref, g2_ref, kkw_ref, kaw_ref, hsum, tri_ref,
               same_ref):
    zs = z + mu_ref[...] * (z_prev - z)
    r = zs[:, 0:C_BR]
    k = zs[:, C_BR:2 * C_BR]
    v = zs[:, 2 * C_BR:3 * C_BR]
    lwla = zs[:, 3 * C_BR:3 * C_BR + RW_LW + RW_LA]
    lg = zs[:, 3 * C_BR + RW_LW + RW_LA:]
    w_log = -_softplus(-(w0_ref[...] + _mm(jnp.tanh(lwla), w2_ref[...]))) - 0.5
    ld = -jnp.exp(w_log)
    a = _sigmoid(a0_ref[...] + _mm(lwla, a2_ref[...]))
    g = _mm(_sigmoid(lg), g2_ref[...])
    kk = k * kkw_ref[...]
    kk = kk * lax.rsqrt(jnp.maximum(_dot(kk * kk, hsum, PREC['headsum'], exact='b'), 1e-24))
    k2 = k * (1.0 + (a - 1.0) * kaw_ref[...])
    bv = kk * a
    if valid is not None:
        ld = jnp.where(valid, ld, 0.0)
        k2 = jnp.where(valid, k2, 0.0)
        v = jnp.where(valid, v, 0.0)
        bv = jnp.where(valid, bv, 0.0)
    lc = _dot(tri_ref[...], ld, PREC['cumsum'], exact='a')
    lend = _dot(same_ref[...], ld, PREC['cumsum'], exact='a')
    e_end = jnp.exp(lend - lc)
    e_neg = jnp.exp(-lc)
    return dict(r=r, k2=k2, v=v, g=g, rt=r * jnp.exp(lc), kkt=kk * jnp.exp(lc - ld), kh=k2 * e_neg, bh=bv * e_neg,
                kw=k2 * e_end, bw=bv * e_end, wc=jnp.exp(lend))


def _rwkv_chunks_local(chunks, strict_ref, incl_ref, lvl_ref, chunk):
    hc = RW_HEADS * chunk
    nk = RW_HEADS * RW_HEAD
    n = range(len(chunks))
    kkt, rt, kh, bh, vv, kw, bw = (list(x) for x in zip(*chunks))
    amat = [_dot(jnp.concatenate([kkt[c], rt[c]], axis=0), jnp.concatenate([kh[c], bh[c]], axis=0),
                 PREC['amat'], _NT) for c in n]
    strict = strict_ref[...] != 0.0
    incl = incl_ref[...] != 0.0
    a_kk = [jnp.where(strict, amat[c][0:hc, 0:hc], 0.0) for c in n]
    a_kb = [jnp.where(strict, amat[c][0:hc, hc:2 * hc], 0.0) for c in n]
    a_rk = [jnp.where(incl, amat[c][hc:2 * hc, 0:hc], 0.0) for c in n]
    a_rb = [jnp.where(incl, amat[c][hc:2 * hc, hc:2 * hc], 0.0) for c in n]
    av = [_dot(jnp.concatenate([a_kk[c], a_rk[c]], axis=0), vv[c], PREC['apply']) for c in n]
    ri = lax.broadcasted_iota(jnp.int32, (hc, hc), 0)
    cj = lax.broadcasted_iota(jnp.int32, (hc, hc), 1)
    eye = jnp.where(ri == cj, 1.0, 0.0)
    lvl0 = lvl_ref[0] != 0.0
    t_inv = [eye - jnp.where(lvl0, a_kb[c], 0.0) for c in n]
    for lv in range(1, lvl_ref.shape[0]):
        lvl = lvl_ref[lv] != 0.0
        half = [_dot(t_inv[c], jnp.where(lvl, a_kb[c], 0.0), PREC['inv']) for c in n]
        t_inv = [t_inv[c] - _dot(half[c], t_inv[c], PREC['inv']) for c in n]
    gu = [_dot(t_inv[c], jnp.concatenate([kkt[c], av[c][0:hc]], axis=1), PREC['apply']) for c in n]
    pu = [_dot(a_rb[c], gu[c], PREC['apply']) for c in n]
    mc = [_dot(gu[c][:, 0:nk], bw[c], PREC['update'], _TN) for c in n]
    nn = [_dot(jnp.concatenate([vv[c], gu[c][:, nk:2 * nk]], axis=0), jnp.concatenate([kw[c], -bw[c]], axis=0),
               PREC['update'], _TN) for c in n]
    return [(rt[c] - pu[c][:, 0:nk], av[c][hc:2 * hc] - pu[c][:, nk:2 * nk], mc[c], nn[c]) for c in n]


def _rwkv_chunk_apply(s, local, wc, chunk):
    p, o0, mc, nn = local
    o = _dot(p, s, PREC['state'], _NT) + o0
    o_c = o[0:chunk]
    for h in range(1, RW_HEADS):
        o_c = o_c + o[h * chunk:(h + 1) * chunk]
    return o_c, s * wc - _dot(s, mc, PREC['state']) + nn


def _rwkv_post(o, pre, rk_ref, gng_ref, gnb_ref, hsum):
    inv_n = 1.0 / RW_HEAD
    o_mu = _dot(o, hsum, PREC['headsum'], exact='b') * inv_n
    od = o - o_mu
    o_var = _dot(od * od, hsum, PREC['headsum'], exact='b') * inv_n
    on = od * lax.rsqrt(o_var + RW_GN_EPS) * gng_ref[...] + gnb_ref[...]
    bonus = _dot(pre['r'] * pre['k2'] * rk_ref[...], hsum, PREC['headsum'], exact='b') * pre['v']
    return (on + bonus) * pre['g']


_RWKV_LOCAL_KEYS = ('kkt', 'rt', 'kh', 'bh', 'v', 'kw', 'bw')


def _rwkv_kernel(z_ref, prev0_ref, wkv0_ref, mu_ref, w0_ref, w2_ref, a0_ref, a2_ref, g2_ref, kkw_ref, kaw_ref,
                 rk_ref, gng_ref, gnb_ref, hsum_ref, tri_ref, same_ref, strict_ref, incl_ref, lvl_ref,
                 y_ref, wkv1_ref, s_scr, prev_scr, *, t_tile, chunk, t_valid, n_tiles, carry):
    i = pl.program_id(1)
    z = z_ref[...]
    row = lax.broadcasted_iota(jnp.int32, (t_tile, 1), 0)
    if carry:
        assert t_valid == t_tile * n_tiles

        @pl.when(i == 0)
        def _():
            s_scr[...] = _state_bd(wkv0_ref)
            prev_scr[...] = prev0_ref[...]

        z_prev = jnp.where(row == 0, prev_scr[...], pltpu.roll(z, 1, 0))
        prev_scr[...] = z[t_tile - 1:t_tile, :]
        valid = None
    else:
        step = row % chunk
        z_prev = jnp.where(step == 0, prev0_ref[...], pltpu.roll(z, 1, 0))
        valid = step < t_valid
    hsum = hsum_ref[...]
    pre = _rwkv_prep(z, z_prev, valid, mu_ref, w0_ref, w2_ref, a0_ref, a2_ref, g2_ref, kkw_ref, kaw_ref, hsum,
                     tri_ref, same_ref)
    lane_head = lax.broadcasted_iota(jnp.int32, (chunk, C_BR), 1) // RW_HEAD
    n_chunks = t_tile // chunk
    local = _rwkv_chunks_local(
        [tuple(_heads_bd(pre[key][c * chunk:(c + 1) * chunk], lane_head) for key in _RWKV_LOCAL_KEYS)
         for c in range(n_chunks)], strict_ref, incl_ref, lvl_ref, chunk)
    o_rows = []
    s = s_scr[...] if carry else None
    for c in range(n_chunks):
        if not carry:
            s = _state_bd(wkv0_ref.at[c])
        o_c, s = _rwkv_chunk_apply(s, local[c], pre['wc'][c * chunk:c * chunk + 1], chunk)
        o_rows.append(o_c)
        if not carry:
            for h in range(RW_HEADS):
                wkv1_ref[c, h] = s[h * RW_HEAD:(h + 1) * RW_HEAD, h * RW_HEAD:(h + 1) * RW_HEAD]
    y_ref[...] = _rwkv_post(jnp.concatenate(o_rows, axis=0), pre, rk_ref, gng_ref, gnb_ref, hsum)

    if carry:
        s_scr[...] = s

        @pl.when(i == n_tiles - 1)
        def _():
            for h in range(RW_HEADS):
                wkv1_ref[h] = s[h * RW_HEAD:(h + 1) * RW_HEAD, h * RW_HEAD:(h + 1) * RW_HEAD]


def _rwkv(z3d, col_blk, prev0, wkv0, pw, *, t_tile, chunk, t_valid, carry):
    bsz, t_len, _ = z3d.shape
    n_tiles = t_len // t_tile
    idx = np.arange(t_tile)
    same = (idx[:, None] // chunk == idx[None, :] // chunk)
    tri = jnp.asarray((same & (idx[None, :] <= idx[:, None])).astype(np.float32))
    same = jnp.asarray(same.astype(np.float32))
    hid = np.arange(C_BR) // RW_HEAD
    hsum = jnp.asarray((hid[:, None] == hid[None, :]).astype(np.float32))
    hc = RW_HEADS * chunk
    hh, tt = np.arange(hc) // chunk, np.arange(hc) % chunk
    same_head = hh[:, None] == hh[None, :]
    strict = jnp.asarray((same_head & (tt[None, :] < tt[:, None])).astype(np.float32))
    incl = jnp.asarray((same_head & (tt[None, :] <= tt[:, None])).astype(np.float32))
    lvls = []
    m = 1
    while m < chunk:
        lvls.append(same_head & (tt[:, None] // (2 * m) == tt[None, :] // (2 * m))
                    & (tt[:, None] % (2 * m) >= m) & (tt[None, :] % (2 * m) < m))
        m *= 2
    lvl = jnp.asarray(np.stack(lvls).astype(np.float32))
    vec = lambda n: _full((1, n))
    if carry:
        prev_spec = pl.BlockSpec((None, 1, RW_IN), lambda b, i: (b, 0, 0))
        wkv_spec = pl.BlockSpec((None, RW_HEADS, RW_HEAD, RW_HEAD), lambda b, i: (b, 0, 0, 0))
    else:
        assert n_tiles == 1
        prev_spec = pl.BlockSpec((None, t_tile, RW_IN), lambda b, i: (b, 0, 0))
        wkv_spec = pl.BlockSpec((None, t_tile // chunk, RW_HEADS, RW_HEAD, RW_HEAD), lambda b, i: (b, 0, 0, 0, 0))
    kern = functools.partial(_rwkv_kernel, t_tile=t_tile, chunk=chunk, t_valid=t_valid, n_tiles=n_tiles, carry=carry)
    return pl.pallas_call(
        kern,
        grid=(bsz, n_tiles),
        in_specs=[pl.BlockSpec((None, t_tile, RW_IN), lambda b, i: (b, i, col_blk)), prev_spec, wkv_spec,
                  vec(RW_IN), vec(C_BR), _full((RW_LW + RW_LA, C_BR)), vec(C_BR), _full((RW_LW + RW_LA, C_BR)),
                  _full((RW_LG, C_BR)), vec(C_BR), vec(C_BR), vec(C_BR), vec(C_BR), vec(C_BR),
                  _full((C_BR, C_BR)), _full((t_tile, t_tile)), _full((t_tile, t_tile)),
                  _full((hc, hc)), _full((hc, hc)), _full(lvl.shape)],
        out_specs=[pl.BlockSpec((None, t_tile, C_BR), lambda b, i: (b, i, 0)), wkv_spec],
        out_shape=[jax.ShapeDtypeStruct((bsz, t_len, C_BR), F32), jax.ShapeDtypeStruct(wkv0.shape, F32)],
        scratch_shapes=[pltpu.VMEM((C_BR, C_BR), F32), pltpu.VMEM((1, RW_IN), F32)],
        compiler_params=_params(("parallel", "arbitrary")),
        name="rwkv7",
    )(z3d, prev0, wkv0, pw['mu'], pw['w0'], pw['w2'], pw['a0'], pw['a2'], pw['g2'], pw['kk'], pw['ka'],
      pw['rk'], pw['gn_g'], pw['gn_b'], hsum, tri, same, strict, incl, lvl)


def _s5_kernel(u_ref, h0_ref, lbr_ref, lbi_ref, bb_ref, cc_ref, d_ref, gw_ref, gb_ref,
               y_ref, h1_ref, bu_scr, h_scr, *relayout_scr, bsz, t_tile, n_tiles, time_major):
    i = pl.program_id(0)

    @pl.when(i == 0)
    def _():
        h_scr[...] = h0_ref[...]

    rows = bsz * t_tile
    n_lt = S5_N // LANES
    n_ut = C_BR // LANES
    lane_tile = lambda ref, j: ref[:, j * LANES:(j + 1) * LANES]
    seq_rows = lambda t: pl.ds(t, bsz, stride=t_tile)
    step_rows = lambda t: pl.ds(pl.multiple_of(t * bsz, bsz), bsz)
    u = u_ref[...].reshape(rows, C_BR)
    if time_major:
        bt_scr, tm_scr = relayout_scr
        for j in range(n_ut):
            bt_scr[j] = lane_tile(u, j)

        def to_time_major(t, _):
            for j in range(n_ut):
                tm_scr[j, step_rows(t), :] = bt_scr[j, seq_rows(t), :]
            return 0

        lax.fori_loop(0, t_tile, to_time_major, 0)
        u = jnp.concatenate([tm_scr[j] for j in range(n_ut)], axis=1)
        sl_of = step_rows
    else:
        sl_of = seq_rows
    bu = _mm(u, bb_ref[...])
    for j in range(2 * n_lt):
        bu_scr[j] = lane_tile(bu, j)
    lbr = [jnp.broadcast_to(lane_tile(lbr_ref, j), (bsz, LANES)) for j in range(n_lt)]
    lbi = [jnp.broadcast_to(lane_tile(lbi_ref, j), (bsz, LANES)) for j in range(n_lt)]

    def step(t, carry):
        hr, hi = carry
        sl = sl_of(t)
        new_r, new_i = [], []
        for j in range(n_lt):
            nr = lbr[j] * hr[j] - lbi[j] * hi[j] + bu_scr[j, sl, :]
            ni = lbr[j] * hi[j] + lbi[j] * hr[j] + bu_scr[n_lt + j, sl, :]
            bu_scr[j, sl, :] = nr
            bu_scr[n_lt + j, sl, :] = ni
            new_r.append(nr)
            new_i.append(ni)
        return tuple(new_r), tuple(new_i)

    h_init = (tuple(lane_tile(h_scr, j) for j in range(n_lt)),
              tuple(lane_tile(h_scr, n_lt + j) for j in range(n_lt)))
    hr, hi = lax.fori_loop(0, t_tile, step, h_init)
    for j in range(n_lt):
        h_scr[:, j * LANES:(j + 1) * LANES] = hr[j]
        h_scr[:, (n_lt + j) * LANES:(n_lt + j + 1) * LANES] = hi[j]

    hs = jnp.concatenate([bu_scr[j] for j in range(2 * n_lt)], axis=1)
    y = _mm(hs, cc_ref[...]) + d_ref[...] * u
    y = _gelu_tanh(y)
    y = y * _sigmoid(_mm(y, gw_ref[...]) + gb_ref[...])
    if time_major:
        for j in range(n_ut):
            tm_scr[j] = lane_tile(y, j)

        def to_seq_major(t, _):
            for j in range(n_ut):
                bt_scr[j, seq_rows(t), :] = tm_scr[j, step_rows(t), :]
            return 0

        lax.fori_loop(0, t_tile, to_seq_major, 0)
        y = jnp.concatenate([bt_scr[j] for j in range(n_ut)], axis=1)
    y_ref[...] = y.reshape(y_ref.shape)

    @pl.when(i == n_tiles - 1)
    def _():
        h1_ref[...] = h_scr[...]


def _s5(z, col_blk, h0, ps, *, bsz, t_len, t_tile):
    n_tiles = t_len // t_tile
    rows = bsz * t_tile
    if z.ndim == 3:
        u_spec = pl.BlockSpec((bsz, t_tile, C_BR), lambda i: (0, i, col_blk))
        y_spec = pl.BlockSpec((bsz, t_tile, C_BR), lambda i: (0, i, 0))
        y_shape = (bsz, t_len, C_BR)
    else:
        assert n_tiles == 1
        u_spec = pl.BlockSpec((rows, C_BR), lambda i: (0, col_blk))
        y_spec = pl.BlockSpec((rows, C_BR), lambda i: (0, 0))
        y_shape = (rows, C_BR)
    time_major = z.ndim == 3 and bsz == 8
    relayout_scr = [pltpu.VMEM((C_BR // LANES, rows, LANES), F32)] * 2 if time_major else []
    kern = functools.partial(_s5_kernel, bsz=bsz, t_tile=t_tile, n_tiles=n_tiles, time_major=time_major)
    return pl.pallas_call(
        kern,
        grid=(n_tiles,),
        in_specs=[u_spec, _full((bsz, 2 * S5_N)), _full((1, S5_N)), _full((1, S5_N)),
                  _full((C_BR, 2 * S5_N)), _full((2 * S5_N, C_BR)), _full((1, C_BR)),
                  _full((C_BR, C_BR)), _full((1, C_BR))],
        out_specs=[y_spec, _full((bsz, 2 * S5_N))],
        out_shape=[jax.ShapeDtypeStruct(y_shape, F32), jax.ShapeDtypeStruct((bsz, 2 * S5_N), F32)],
        scratch_shapes=[pltpu.VMEM((2 * S5_N // LANES, rows, LANES), F32), pltpu.VMEM((bsz, 2 * S5_N), F32)]
        + relayout_scr,
        compiler_params=_params(("arbitrary",)),
        name="s5",
    )(z, h0, ps['lb_re'], ps['lb_im'], ps['bb'], ps['cc'], ps['d'], ps['glu_w'], ps['glu_b'])


def _conv_taps(full_scr, w_ref, t_tile):
    lo = CONV_HIST_PAD - CONV_HIST
    sub = 8
    if t_tile % sub:
        acc = None
        for j in range(CONV_W):
            term = full_scr[lo + j:lo + j + t_tile, :] * w_ref[j:j + 1, :]
            acc = term if acc is None else acc + term
        return acc
    acc = None
    for rho in range(sub):
        offs = [o for o in range(rho, lo + CONV_W, sub) if o >= lo]
        rows = t_tile + (sub if rho else 0)
        part = None
        for o in offs:
            term = full_scr[o - rho:o - rho + rows, :] * w_ref[o - lo:o - lo + 1, :]
            part = term if part is None else part + term
        part = part[rho:rho + t_tile]
        acc = part if acc is None else acc + part
    return acc


def _conv_kernel(z_ref, c0_ref, w_ref, b_ref, g_ref, be_ref, y_ref, c1_ref, full_scr, *, t_tile, n_tiles, n_seq):
    i = pl.program_id(1)
    lo = CONV_HIST_PAD - CONV_HIST
    assert n_seq == 1 or n_tiles == 1

    def one_sequence(s):
        def load_history():
            full_scr[0:lo, :] = jnp.zeros((lo, C_BR), F32)
            full_scr[lo:CONV_HIST_PAD, :] = c0_ref[s]

        if n_tiles == 1:
            load_history()
        else:
            pl.when(i == 0)(load_history)
        z = z_ref[s]
        full_scr[CONV_HIST_PAD:CONV_HIST_PAD + t_tile, :] = z[:, 0:C_BR] * _sigmoid(z[:, C_BR:2 * C_BR])
        y = _layer_norm(_conv_taps(full_scr, w_ref, t_tile) + b_ref[...], g_ref[...], be_ref[...])
        y_ref[s] = y * _sigmoid(y)
        hist = full_scr[t_tile:t_tile + CONV_HIST_PAD, :]
        if n_tiles == 1:
            c1_ref[s] = hist[lo:, :]
        else:
            full_scr[0:CONV_HIST_PAD, :] = hist

            @pl.when(i == n_tiles - 1)
            def _():
                c1_ref[s] = hist[lo:, :]

    if n_seq == 1:
        one_sequence(0)
    else:
        def body(s, carry):
            one_sequence(s)
            return carry

        lax.fori_loop(0, n_seq, body, 0)


def _conv(z3d, col_blk, conv0, pc, *, t_tile, n_seq):
    bsz, t_len, _ = z3d.shape
    n_tiles = t_len // t_tile
    kern = functools.partial(_conv_kernel, t_tile=t_tile, n_tiles=n_tiles, n_seq=n_seq)
    return pl.pallas_call(
        kern,
        grid=(bsz // n_seq, n_tiles),
        in_specs=[pl.BlockSpec((n_seq, t_tile, 2 * C_BR), lambda b, i: (b, i, col_blk)),
                  pl.BlockSpec((n_seq, CONV_HIST, C_BR), lambda b, i: (b, 0, 0)),
                  _full((CONV_W, C_BR)), _full((1, C_BR)), _full((1, C_BR)), _full((1, C_BR))],
        out_specs=[pl.BlockSpec((n_seq, t_tile, C_BR), lambda b, i: (b, i, 0)),
                   pl.BlockSpec((n_seq, CONV_HIST, C_BR), lambda b, i: (b, 0, 0))],
        out_shape=[jax.ShapeDtypeStruct((bsz, t_len, C_BR), F32),
                   jax.ShapeDtypeStruct((bsz, CONV_HIST, C_BR), F32)],
        scratch_shapes=[pltpu.VMEM((CONV_HIST_PAD + t_tile, C_BR), F32)],
        compiler_params=_params(("parallel", "arbitrary")),
        name="conv",
    )(z3d, conv0, pc['w'], pc['b'], pc['ln_g'], pc['ln_b'])


def _gmlp_kernel(z_ref, g_ref, b_ref, wm_ref, bias_ref, y_ref, *v_ref):
    z = z_ref[...]
    u = z[:, 0:C_BR]
    v = _layer_norm(z[:, C_BR:2 * C_BR], g_ref[...], b_ref[...])
    if v_ref:
        v_ref[0][...] = v
    vb = v.astype(MM_DTYPE)
    head = lax.broadcasted_iota(jnp.int32, (CHUNK, C_BR), 1) // GM_HEAD
    for c in range(GM_TILE // CHUNK):
        rows = slice(c * CHUNK, (c + 1) * CHUNK)
        s = bias_ref[...]
        for h in range(GM_HEADS):
            s = s + jnp.where(head == h, jnp.dot(wm_ref[h], vb[rows], preferred_element_type=F32), 0.0)
        y_ref[rows, :] = u[rows] * s


def _gmlp(z3d, col_blk, pg, wm, bias, *, emit_v):
    bsz, t_len, _ = z3d.shape
    out_spec = pl.BlockSpec((None, GM_TILE, C_BR), lambda b, i: (b, i, 0))
    n_out = 2 if emit_v else 1
    return pl.pallas_call(
        _gmlp_kernel,
        grid=(bsz, t_len // GM_TILE),
        in_specs=[pl.BlockSpec((None, GM_TILE, 2 * C_BR), lambda b, i: (b, i, col_blk)),
                  _full((1, C_BR)), _full((1, C_BR)), _full((GM_HEADS, CHUNK, CHUNK)), _full((CHUNK, C_BR))],
        out_specs=[out_spec] * n_out,
        out_shape=[jax.ShapeDtypeStruct((bsz, t_len, C_BR), F32)] * n_out,
        compiler_params=_params(("parallel", "parallel")),
        name="gmlp",
    )(z3d, pg['ln_g'], pg['ln_b'], wm, bias)


def _merge_kernel(x_ref, yrw_ref, ys5_ref, ycv_ref, ygm_ref, wg_ref, wb_ref, wo_ref, g_ref, b_ref, o_ref):
    x = x_ref[...]
    xb = x.astype(MM_DTYPE)
    merged = None
    for bidx, y_ref in enumerate((yrw_ref, ys5_ref, ycv_ref, ygm_ref)):
        gate = _sigmoid(jnp.dot(xb, wg_ref[:, bidx * D_MODEL:(bidx + 1) * D_MODEL], preferred_element_type=F32))
        term = gate * _mm(y_ref[...], wb_ref[bidx])
        merged = term if merged is None else merged + term
    o_ref[...] = _layer_norm(DN_ALPHA * x + _mm(merged, wo_ref[...]), g_ref[...], b_ref[...])


def _merge(x2d, ys, pm):
    n = x2d.shape[0]
    tm = 256
    row = lambda w: pl.BlockSpec((tm, w), lambda i: (i, 0))
    return pl.pallas_call(
        _merge_kernel,
        grid=(n // tm,),
        in_specs=[row(D_MODEL), row(C_BR), row(C_BR), row(C_BR), row(C_BR),
                  _full((D_MODEL, N_BRANCH * D_MODEL)), _full((N_BRANCH, C_BR, D_MODEL)),
                  _full((D_MODEL, D_MODEL)), _full((1, D_MODEL)), _full((1, D_MODEL))],
        out_specs=row(D_MODEL),
        out_shape=jax.ShapeDtypeStruct((n, D_MODEL), F32),
        compiler_params=_params(("parallel",)),
        name="merge",
    )(x2d, *ys, pm['w_gate'], pm['w_branch'], pm['w_out'], pm['ln1_g'], pm['ln1_b'])


def _moe_kernel(x_ref, wg_ref, bg_ref, wu_ref, wd_ref, g_ref, b_ref, o_ref, hh_scr, *, tm):
    lane = lax.broadcasted_iota(jnp.int32, (tm, LANES), 1)
    x = x_ref[...]
    xb = x.astype(MM_DTYPE)
    logits = _dot(x, wg_ref[...], PREC['route']) + bg_ref[...]
    gl = jnp.where(lane < N_GROUPS, logits, NEG_BIG)
    gmax = jnp.max(gl, axis=-1, keepdims=True)
    g_sel = jnp.min(jnp.where(gl == gmax, lane, LANES), axis=-1, keepdims=True)
    p_group = 1.0 / jnp.sum(jnp.where(lane < N_GROUPS, jnp.exp(gl - gmax), 0.0), axis=-1, keepdims=True)
    first = N_GROUPS + g_sel * E_PER_GROUP
    el = jnp.where((lane >= first) & (lane < first + E_PER_GROUP), logits, NEG_BIG)
    m1 = jnp.max(el, axis=-1, keepdims=True)
    i1 = jnp.min(jnp.where(el == m1, lane, LANES), axis=-1, keepdims=True)
    el2 = jnp.where(lane == i1, NEG_BIG, el)
    m2 = jnp.max(el2, axis=-1, keepdims=True)
    i2 = jnp.min(jnp.where(el2 == m2, lane, LANES), axis=-1, keepdims=True)
    e2 = jnp.exp(m2 - m1)
    w1 = p_group / (1.0 + e2)
    w2 = p_group * e2 / (1.0 + e2)
    for e in range(N_EXPERTS):
        comb_e = jnp.where(i1 == e + N_GROUPS, w1, 0.0) + jnp.where(i2 == e + N_GROUPS, w2, 0.0)
        h = jnp.dot(xb, wu_ref[e], preferred_element_type=F32)
        h1 = h[:, 0:D_EXPERT]
        hh = h1 * _sigmoid(h1) * h[:, D_EXPERT:2 * D_EXPERT] * comb_e
        hh_scr[:, e * D_EXPERT:(e + 1) * D_EXPERT] = hh.astype(MM_DTYPE)
    moe = jnp.dot(hh_scr[...], wd_ref[...], preferred_element_type=F32)
    o_ref[...] = _layer_norm(DN_ALPHA * x + moe, g_ref[...], b_ref[...])


def _moe(x2d, pe):
    n = x2d.shape[0]
    tm = 512
    kern = functools.partial(_moe_kernel, tm=tm)
    resident = lambda shape: pl.BlockSpec(shape, lambda i: (0,) * len(shape), pipeline_mode=pl.Buffered(1))
    return pl.pallas_call(
        kern,
        grid=(n // tm,),
        in_specs=[pl.BlockSpec((tm, D_MODEL), lambda i: (i, 0)),
                  _full((D_MODEL, LANES)), _full((1, LANES)),
                  resident((N_EXPERTS, D_MODEL, 2 * D_EXPERT)), resident((N_EXPERTS * D_EXPERT, D_MODEL)),
                  _full((1, D_MODEL)), _full((1, D_MODEL))],
        out_specs=pl.BlockSpec((tm, D_MODEL), lambda i: (i, 0)),
        out_shape=jax.ShapeDtypeStruct((n, D_MODEL), F32),
        scratch_shapes=[pltpu.VMEM((tm, N_EXPERTS * D_EXPERT), MM_DTYPE)],
        compiler_params=_params(("parallel",)),
        name="moe",
    )(x2d, pe['wg'], pe['bg'], pe['w_up'], pe['w_down'].reshape(N_EXPERTS * D_EXPERT, D_MODEL), pe['ln2_g'],
      pe['ln2_b'])


def _block_diag(blocks):
    g, m, n = blocks.shape
    eye = jnp.eye(g, dtype=blocks.dtype)
    return (eye[:, None, :, None] * blocks[:, :, None, :]).reshape(g * m, g * n)


def _prep_layer(l, p):
    w_in = p['w_in'][l]
    w_mix = jnp.concatenate([w_in[:, :OFF_S5], w_in[:, OFF_CV:OFF_GM], w_in[:, OFF_GM:OFF_GATE],
                             w_in[:, OFF_S5:OFF_CV]], axis=1).astype(MM_DTYPE)
    row = lambda a: a.reshape(1, -1).astype(F32)
    zeros_lora = jnp.zeros((RW_LW, C_BR), F32)
    rw = dict(mu=row(p['rw_mu'][l]), w0=row(p['rw_w0'][l]),
              w2=jnp.concatenate([p['rw_w2'][l], zeros_lora], axis=0).astype(MM_DTYPE),
              a0=row(p['rw_a0'][l]),
              a2=jnp.concatenate([zeros_lora, p['rw_a2'][l]], axis=0).astype(MM_DTYPE),
              g2=p['rw_g2'][l].astype(MM_DTYPE), kk=row(p['rw_kk'][l]), ka=row(p['rw_ka'][l]),
              rk=row(p['rw_rk'][l]), gn_g=row(p['rw_gn_g'][l]), gn_b=row(p['rw_gn_b'][l]))
    lr, li = p['s5_lam_re'][l].astype(F32), p['s5_lam_im'][l].astype(F32)
    dt = jnp.exp(p['s5_log_dt'][l].astype(F32))[:, None]
    mag = jnp.exp(lr * dt)
    lb_re, lb_im = mag * jnp.cos(li * dt), mag * jnp.sin(li * dt)
    den = lr * lr + li * li
    q_re = ((lb_re - 1.0) * lr + lb_im * li) / den
    q_im = (lb_im * lr - (lb_re - 1.0) * li) / den
    br, bi = p['s5_b_re'][l].astype(F32), p['s5_b_im'][l].astype(F32)
    bb_re = q_re[..., None] * br - q_im[..., None] * bi
    bb_im = q_re[..., None] * bi + q_im[..., None] * br
    bb = jnp.concatenate([_block_diag(jnp.swapaxes(bb_re, 1, 2)), _block_diag(jnp.swapaxes(bb_im, 1, 2))],
                         axis=1).astype(MM_DTYPE)
    cc = jnp.concatenate([_block_diag(jnp.swapaxes(p['s5_c_re'][l].astype(F32), 1, 2)),
                          -_block_diag(jnp.swapaxes(p['s5_c_im'][l].astype(F32), 1, 2))],
                         axis=0).astype(MM_DTYPE)
    s5 = dict(lb_re=lb_re.reshape(1, S5_N), lb_im=lb_im.reshape(1, S5_N), bb=bb, cc=cc, d=row(p['s5_d'][l]),
              glu_w=p['s5_glu_w'][l].astype(MM_DTYPE), glu_b=row(p['s5_glu_b'][l]))
    cv = dict(w=p['cv_w'][l].astype(F32), b=row(p['cv_b'][l]), ln_g=row(p['cv_ln_g'][l]), ln_b=row(p['cv_ln_b'][l]))
    causal = jnp.tril(jnp.ones((CHUNK, CHUNK), dtype=bool))
    wm = jnp.where(causal, p['gm_ws'][l], 0).astype(F32)
    bias = jnp.repeat(jnp.swapaxes(p['gm_bs'][l], 0, 1), GM_HEAD, axis=1).astype(F32)
    gm = dict(ln_g=row(p['gm_ln_g'][l]), ln_b=row(p['gm_ln_b'][l]), wm=wm, bias=bias)
    mg = dict(w_gate=w_in[:, OFF_GATE:].astype(MM_DTYPE), w_branch=p['w_branch'][l].astype(MM_DTYPE),
              w_out=p['w_out'][l].astype(MM_DTYPE), ln1_g=row(p['ln1_g'][l]), ln1_b=row(p['ln1_b'][l]))
    pad = LANES - N_GROUPS - N_EXPERTS
    wg = jnp.concatenate([p['moe_wg1'][l], p['moe_wg2'][l], jnp.zeros((D_MODEL, pad), F32)], axis=1).astype(F32)
    bg = jnp.concatenate([p['moe_bg1'][l], p['moe_bg2'][l], jnp.zeros((pad,), F32)]).reshape(1, LANES).astype(F32)
    moe = dict(wg=wg, bg=bg, w_up=p['moe_w_up'][l].astype(MM_DTYPE), w_down=p['moe_w_down'][l].astype(MM_DTYPE),
               ln2_g=row(p['ln2_g'][l]), ln2_b=row(p['ln2_b'][l]))
    return dict(w_mix=w_mix, rw=rw, s5=s5, cv=cv, gm=gm, mg=mg, moe=moe)


def _gmlp_group(z3d, pg, *, is_prompt):
    bsz, t_len, n_cols = z3d.shape
    if is_prompt:
        y, = _gmlp(z3d, P_GM // (2 * C_BR), pg, pg['wm'].astype(MM_DTYPE), pg['bias'], emit_v=False)
        return y, None
    reps = CHUNK // t_len
    wm = jnp.stack([jnp.kron(jnp.eye(reps, dtype=F32), pg['wm'][h, :t_len, :t_len]) for h in range(GM_HEADS)])
    bias = jnp.tile(pg['bias'][:t_len], (reps, 1))
    y, v = _gmlp(z3d.reshape(1, bsz * t_len, n_cols), P_GM // (2 * C_BR), pg, wm.astype(MM_DTYPE), bias,
                 emit_v=True)
    return y.reshape(bsz, t_len, C_BR), v.reshape(bsz, t_len, C_BR)


def _run_group(x, wkv0, shift0, s5r0, s5i0, conv0, layers, *, is_prompt):
    bsz, t_len, _ = x.shape
    n = bsz * t_len
    x2d = x.reshape(n, D_MODEL)
    outs = []
    for l, lp in enumerate(layers):
        z2d = _inproj(x2d, lp['w_mix'])
        z3d = z2d.reshape(bsz, t_len, N_MIX)
        sh0 = shift0[:, l].reshape(bsz, 1, RW_IN)
        if is_prompt:
            y_rw, wkv1 = _rwkv(z3d, P_RW // RW_IN, sh0, wkv0[:, l], lp['rw'], t_tile=256, chunk=64,
                               t_valid=t_len, carry=True)
        else:
            pad = ((0, 0), (0, RW_PAD - t_len), (0, 0))
            grp = lambda a: a.reshape((bsz // RW_GROUP, RW_GROUP * a.shape[1]) + a.shape[2:])
            z_rw = grp(jnp.pad(z3d[:, :, P_RW:P_RW + RW_IN], pad))
            prev0 = grp(jnp.pad(sh0, ((0, 0), (0, RW_PAD - 1), (0, 0))))
            y_rw, wkv1 = _rwkv(z_rw, 0, prev0, grp(wkv0[:, l][:, None]), lp['rw'], t_tile=RW_GROUP * RW_PAD,
                               chunk=RW_PAD, t_valid=t_len, carry=False)
            y_rw = y_rw.reshape(bsz, RW_PAD, C_BR)[:, :t_len]
            wkv1 = wkv1.reshape(bsz, RW_HEADS, RW_HEAD, RW_HEAD)
        shift1 = z3d[:, t_len - 1, P_RW:P_RW + RW_IN]
        h0 = jnp.concatenate([s5r0[:, l].reshape(bsz, S5_N), s5i0[:, l].reshape(bsz, S5_N)], axis=1)
        if is_prompt:
            y_s5, h1 = _s5(z3d, P_S5 // C_BR, h0, lp['s5'], bsz=bsz, t_len=t_len, t_tile=128)
        else:
            y_s5, h1 = _s5(z2d, P_S5 // C_BR, h0, lp['s5'], bsz=bsz, t_len=t_len, t_tile=t_len)
        s5r1 = h1[:, :S5_N].reshape(bsz, S5_GROUPS, S5_STATE)
        s5i1 = h1[:, S5_N:].reshape(bsz, S5_GROUPS, S5_STATE)
        y_cv, conv1 = _conv(z3d, P_CV // (2 * C_BR), conv0[:, l], lp['cv'], t_tile=256 if is_prompt else t_len,
                            n_seq=1 if is_prompt else CV_GROUP)
        y_gm, v_gm = _gmlp_group(z3d, lp['gm'], is_prompt=is_prompt)
        ys = [y.reshape(n, C_BR) for y in (y_rw, y_s5, y_cv, y_gm)]
        x2d = _merge(x2d, ys, lp['mg'])
        x2d = _moe(x2d, lp['moe'])
        outs.append((wkv1, shift1, s5r1, s5i1, conv1, v_gm))
    states = tuple(None if outs[0][i] is None else jnp.stack([o[i] for o in outs], axis=1) for i in range(6))
    return x2d.reshape(bsz, t_len, D_MODEL), states


def kernel(x_prompt, x_sample, state_rwkv_wkv, state_rwkv_shift, state_s5_re, state_s5_im, cache_conv,
           w_in, rw_mu, rw_w0, rw_w2, rw_a0, rw_a2, rw_g2, rw_kk, rw_ka, rw_rk, rw_gn_g, rw_gn_b,
           s5_lam_re, s5_lam_im, s5_log_dt, s5_b_re, s5_b_im, s5_c_re, s5_c_im, s5_d, s5_glu_w, s5_glu_b,
           cv_w, cv_b, cv_ln_g, cv_ln_b, gm_ln_g, gm_ln_b, gm_ws, gm_bs,
           w_branch, w_out, ln1_g, ln1_b,
           moe_wg1, moe_bg1, moe_wg2, moe_bg2, moe_w_up, moe_w_down, ln2_g, ln2_b):
    p = dict(w_in=w_in, rw_mu=rw_mu, rw_w0=rw_w0, rw_w2=rw_w2, rw_a0=rw_a0, rw_a2=rw_a2, rw_g2=rw_g2,
             rw_kk=rw_kk, rw_ka=rw_ka, rw_rk=rw_rk, rw_gn_g=rw_gn_g, rw_gn_b=rw_gn_b,
             s5_lam_re=s5_lam_re, s5_lam_im=s5_lam_im, s5_log_dt=s5_log_dt, s5_b_re=s5_b_re, s5_b_im=s5_b_im,
             s5_c_re=s5_c_re, s5_c_im=s5_c_im, s5_d=s5_d, s5_glu_w=s5_glu_w, s5_glu_b=s5_glu_b,
             cv_w=cv_w, cv_b=cv_b, cv_ln_g=cv_ln_g, cv_ln_b=cv_ln_b, gm_ln_g=gm_ln_g, gm_ln_b=gm_ln_b,
             gm_ws=gm_ws, gm_bs=gm_bs, w_branch=w_branch, w_out=w_out, ln1_g=ln1_g, ln1_b=ln1_b,
             moe_wg1=moe_wg1, moe_bg1=moe_bg1, moe_wg2=moe_wg2, moe_bg2=moe_bg2, moe_w_up=moe_w_up,
             moe_w_down=moe_w_down, ln2_g=ln2_g, ln2_b=ln2_b)
    layers = [_prep_layer(l, p) for l in range(DEPTH)]
    bp = x_prompt.shape[0]
    dt = x_prompt.dtype
    y_prompt, (p_wkv, p_shift, p_s5r, p_s5i, p_conv, _) = _run_group(
        x_prompt,
        jnp.zeros((bp, DEPTH, RW_HEADS, RW_HEAD, RW_HEAD), dt),
        jnp.zeros((bp, DEPTH, RW_IN), dt),
        jnp.zeros((bp, DEPTH, S5_GROUPS, S5_STATE), dt),
        jnp.zeros((bp, DEPTH, S5_GROUPS, S5_STATE), dt),
        jnp.zeros((bp, DEPTH, CONV_HIST, C_BR), dt),
        layers, is_prompt=True)
    y_sample, (s_wkv, s_shift, s_s5r, s_s5i, s_conv, s_gmv) = _run_group(
        x_sample, state_rwkv_wkv, state_rwkv_shift, state_s5_re, state_s5_im, cache_conv, layers,
        is_prompt=False)
    return (y_prompt, y_sample, p_wkv, p_shift, p_s5r, p_s5i, p_conv,
            s_wkv, s_shift, s_s5r, s_s5i, s_conv, s_gmv)
```

```python
import functools
import math

import numpy as np
import jax
import jax.numpy as jnp
from jax import lax
from jax.experimental import pallas as pl
from jax.experimental.pallas import tpu as pltpu

D_MODEL = 1024
DEPTH = 4
N_BRANCH = 4
C_BR = D_MODEL // 4
RW_HEAD = 64
RW_HEADS = C_BR // RW_HEAD
RW_LW = 64
RW_LA = 64
RW_LG = 128
RW_IN = 3 * C_BR + RW_LW + RW_LA + RW_LG
RW_GN_EPS = 64e-5
RW_PAD = 8
RW_GROUP = 8
S5_GW = 16
S5_GROUPS = C_BR // S5_GW
S5_STATE = 64
S5_N = S5_GROUPS * S5_STATE
CONV_W = 31
CONV_HIST = CONV_W - 1
CONV_HIST_PAD = 32
CV_GROUP = 16
CHUNK = 128
GM_TILE = 4 * CHUNK
GM_HEADS = 4
GM_HEAD = C_BR // GM_HEADS
N_GROUPS = 4
E_PER_GROUP = 4
N_EXPERTS = N_GROUPS * E_PER_GROUP
D_EXPERT = D_MODEL // 4
LN_EPS = 1e-5
DN_ALPHA = (2 * DEPTH) ** 0.25
OFF_S5 = RW_IN
OFF_CV = OFF_S5 + C_BR
OFF_GM = OFF_CV + 2 * C_BR
OFF_GATE = OFF_GM + 2 * C_BR
N_IN = OFF_GATE + N_BRANCH * D_MODEL
P_RW = 0
P_CV = P_RW + RW_IN
P_GM = P_CV + 2 * C_BR
P_S5 = P_GM + 2 * C_BR
N_MIX = P_S5 + C_BR

LANES = 128
VMEM_LIMIT = 56 * 1024 * 1024

F32 = jnp.float32
BF16 = jnp.bfloat16
MM_DTYPE = jnp.bfloat16
HI = lax.Precision.HIGHEST
NEG_BIG = -1e30


def _mm(a, b):
    return jnp.dot(a.astype(MM_DTYPE), b.astype(MM_DTYPE), preferred_element_type=F32)


def _split_bf16(a):
    hi = a.astype(BF16)
    return hi, (a - hi.astype(F32)).astype(BF16)


_NN = ((1,), (0,))
_NT = ((1,), (1,))
_TN = ((0,), (0,))


def _dot(a, b, mode, dims=_NN, exact=None):
    dn = (dims, ((), ()))
    if mode == 'hi':
        return lax.dot_general(a, b, dn, precision=HI, preferred_element_type=F32)
    f = lambda x, y: lax.dot_general(x, y, dn, preferred_element_type=F32)
    if mode == 'bf16':
        return f(a.astype(BF16), b.astype(BF16))
    assert mode == 'x3'
    if exact == 'a':
        b_hi, b_lo = _split_bf16(b)
        a = a.astype(BF16)
        return f(a, b_hi) + f(a, b_lo)
    if exact == 'b':
        a_hi, a_lo = _split_bf16(a)
        b = b.astype(BF16)
        return f(a_hi, b) + f(a_lo, b)
    a_hi, a_lo = _split_bf16(a)
    b_hi, b_lo = _split_bf16(b)
    return f(a_hi, b_hi) + (f(a_hi, b_lo) + f(a_lo, b_hi))


PREC = dict(cumsum='x3', headsum='x3', amat='bf16', inv='bf16', state='bf16', apply='bf16', update='bf16', route='x3')
PREC_SHORT = dict(PREC, update='x3')


def _sigmoid(x):
    return jax.nn.sigmoid(x)


def _softplus(x):
    return jnp.maximum(x, 0.0) + jnp.log1p(jnp.exp(-jnp.abs(x)))


def _gelu_tanh(x):
    return 0.5 * x * (1.0 + jnp.tanh(math.sqrt(2.0 / math.pi) * (x + 0.044715 * (x * x * x))))


def _layer_norm(x, g, b):
    mu = jnp.mean(x, axis=-1, keepdims=True)
    d = x - mu
    var = jnp.mean(d * d, axis=-1, keepdims=True)
    return d * lax.rsqrt(var + LN_EPS) * g + b


def _params(sem):
    return pltpu.CompilerParams(dimension_semantics=sem, vmem_limit_bytes=VMEM_LIMIT)


def _full(shape):
    nd = len(shape)
    return pl.BlockSpec(shape, lambda *_: (0,) * nd)


def _layer_block(shape, layer, **kwargs):
    nd = len(shape)
    return pl.BlockSpec((None,) + tuple(shape), lambda *_: (layer,) + (0,) * nd, **kwargs)


def _inproj_kernel(x_ref, w_ref, z_ref):
    z_ref[...] = _mm(x_ref[...], w_ref[...])


def _inproj(x2d, w_bf16, layer):
    n = x2d.shape[0]
    tm = 512
    return pl.pallas_call(
        _inproj_kernel,
        grid=(n // tm,),
        in_specs=[pl.BlockSpec((tm, D_MODEL), lambda i: (i, 0)), _layer_block((D_MODEL, N_MIX), layer)],
        out_specs=pl.BlockSpec((tm, N_MIX), lambda i: (i, 0)),
        out_shape=jax.ShapeDtypeStruct((n, N_MIX), F32),
        compiler_params=_params(("parallel",)),
        name="inproj",
    )(x2d, w_bf16)


def _heads_bd(x, lane_head):
    return jnp.concatenate([jnp.where(lane_head == h, x, 0.0) for h in range(RW_HEADS)], axis=0)


def _state_bd(wkv):
    zeros_blk = jnp.zeros((RW_HEAD, RW_HEAD), F32)
    return jnp.concatenate(
        [jnp.concatenate([wkv[h] if g == h else zeros_blk for g in range(RW_HEADS)], axis=1)
         for h in range(RW_HEADS)], axis=0)


def _rwkv_prep(z, z_prev, valid, mu_ref, w0_ref, w2_ref, a0_ref, a2_ref, g2_ref, kkw_ref, kaw_ref, hsum, tri_ref,
               chunk, prec):
    zs = z + mu_ref[...] * (z_prev - z)
    r = zs[:, 0:C_BR]
    k = zs[:, C_BR:2 * C_BR]
    v = zs[:, 2 * C_BR:3 * C_BR]
    lwla = zs[:, 3 * C_BR:3 * C_BR + RW_LW + RW_LA]
    lg = zs[:, 3 * C_BR + RW_LW + RW_LA:]
    w_log = -_softplus(-(w0_ref[...] + _mm(jnp.tanh(lwla), w2_ref[...]))) - 0.5
    ld = -jnp.exp(w_log)
    a = _sigmoid(a0_ref[...] + _mm(lwla, a2_ref[...]))
    g = _mm(_sigmoid(lg), g2_ref[...])
    kk = k * kkw_ref[...]
    kk = kk * lax.rsqrt(jnp.maximum(_dot(kk * kk, hsum, prec['headsum'], exact='b'), 1e-24))
    k2 = k * (1.0 + (a - 1.0) * kaw_ref[...])
    bv = kk * a
    if valid is not None:
        ld = jnp.where(valid, ld, 0.0)
        k2 = jnp.where(valid, k2, 0.0)
        v = jnp.where(valid, v, 0.0)
        bv = jnp.where(valid, bv, 0.0)
    t_tile = z.shape[0]
    n = tri_ref.shape[0]
    lc = jnp.concatenate([_dot(tri_ref[...], ld[i * n:(i + 1) * n], prec['cumsum'], exact='a')
                          for i in range(t_tile // n)], axis=0)
    lend = jnp.concatenate([jnp.broadcast_to(lc[(c + 1) * chunk - 1:(c + 1) * chunk], (chunk, C_BR))
                            for c in range(t_tile // chunk)], axis=0)
    e_end = jnp.exp(lend - lc)
    e_neg = jnp.exp(-lc)
    return dict(r=r, k2=k2, v=v, g=g, rt=r * jnp.exp(lc), kkt=kk * jnp.exp(lc - ld), kh=k2 * e_neg, bh=bv * e_neg,
                kw=k2 * e_end, bw=bv * e_end, wc=jnp.exp(lend))


def _rwkv_chunks_local(chunks, strict_ref, incl_ref, lvl_ref, chunk, prec):
    hc = RW_HEADS * chunk
    nk = RW_HEADS * RW_HEAD
    n = range(len(chunks))
    kkt, rt, kh, bh, vv, kw, bw = (list(x) for x in zip(*chunks))
    amat = [_dot(jnp.concatenate([kkt[c], rt[c]], axis=0), jnp.concatenate([kh[c], bh[c]], axis=0),
                 prec['amat'], _NT) for c in n]
    strict = strict_ref[...] != 0.0
    incl = incl_ref[...] != 0.0
    a_kk = [jnp.where(strict, amat[c][0:hc, 0:hc], 0.0) for c in n]
    a_kb = [jnp.where(strict, amat[c][0:hc, hc:2 * hc], 0.0) for c in n]
    a_rk = [jnp.where(incl, amat[c][hc:2 * hc, 0:hc], 0.0) for c in n]
    a_rb = [jnp.where(incl, amat[c][hc:2 * hc, hc:2 * hc], 0.0) for c in n]
    av = [_dot(jnp.concatenate([a_kk[c], a_rk[c]], axis=0), vv[c], prec['apply']) for c in n]
    ri = lax.broadcasted_iota(jnp.int32, (hc, hc), 0)
    cj = lax.broadcasted_iota(jnp.int32, (hc, hc), 1)
    eye = jnp.where(ri == cj, 1.0, 0.0)
    lvl0 = lvl_ref[0] != 0.0
    t_inv = [eye - jnp.where(lvl0, a_kb[c], 0.0) for c in n]
    for lv in range(1, lvl_ref.shape[0]):
        lvl = lvl_ref[lv] != 0.0
        half = [_dot(t_inv[c], jnp.where(lvl, a_kb[c], 0.0), prec['inv']) for c in n]
        t_inv = [t_inv[c] - _dot(half[c], t_inv[c], prec['inv']) for c in n]
    gu = [_dot(t_inv[c], jnp.concatenate([kkt[c], av[c][0:hc]], axis=1), prec['apply']) for c in n]
    pu = [_dot(a_rb[c], gu[c], prec['apply']) for c in n]
    mc = [_dot(gu[c][:, 0:nk], bw[c], prec['update'], _TN) for c in n]
    nn = [_dot(jnp.concatenate([vv[c], gu[c][:, nk:2 * nk]], axis=0), jnp.concatenate([kw[c], -bw[c]], axis=0),
               prec['update'], _TN) for c in n]
    return [(rt[c] - pu[c][:, 0:nk], av[c][hc:2 * hc] - pu[c][:, nk:2 * nk], mc[c], nn[c]) for c in n]


def _rwkv_chunk_apply(s, local, wc, chunk, prec):
    p, o0, mc, nn = local
    o = _dot(p, s, prec['state'], _NT) + o0
    o_c = o[0:chunk]
    for h in range(1, RW_HEADS):
        o_c = o_c + o[h * chunk:(h + 1) * chunk]
    return o_c, s * wc - _dot(s, mc, prec['state']) + nn


def _rwkv_post(o, pre, rk_ref, gng_ref, gnb_ref, hsum, prec):
    inv_n = 1.0 / RW_HEAD
    o_mu = _dot(o, hsum, prec['headsum'], exact='b') * inv_n
    od = o - o_mu
    o_var = _dot(od * od, hsum, prec['headsum'], exact='b') * inv_n
    on = od * lax.rsqrt(o_var + RW_GN_EPS) * gng_ref[...] + gnb_ref[...]
    bonus = _dot(pre['r'] * pre['k2'] * rk_ref[...], hsum, prec['headsum'], exact='b') * pre['v']
    return (on + bonus) * pre['g']


_RWKV_LOCAL_KEYS = ('kkt', 'rt', 'kh', 'bh', 'v', 'kw', 'bw')


def _rwkv_kernel(z_ref, prev0_ref, wkv0_ref, mu_ref, w0_ref, w2_ref, a0_ref, a2_ref, g2_ref, kkw_ref, kaw_ref,
                 rk_ref, gng_ref, gnb_ref, hsum_ref, tri_ref, strict_ref, incl_ref, lvl_ref, *rest,
                 t_tile, chunk, t_valid, n_tiles, carry, prec):
    y_ref, wkv1_ref, s_scr, prev_scr = rest[-4:]
    i = pl.program_id(1)
    z = z_ref[...]
    row = lax.broadcasted_iota(jnp.int32, (t_tile, 1), 0)
    if carry:
        assert t_valid == t_tile * n_tiles

        @pl.when(i == 0)
        def _():
            s_scr[...] = _state_bd(wkv0_ref)
            prev_scr[...] = prev0_ref[...]

        z_prev = jnp.where(row == 0, prev_scr[...], pltpu.roll(z, 1, 0))
        prev_scr[...] = z[t_tile - 1:t_tile, :]
        valid = None
    else:
        step = row % chunk
        z_prev = jnp.where(step == 0, prev0_ref[...], pltpu.roll(z, 1, 0))
        valid = step < t_valid
    hsum = hsum_ref[...]
    pre = _rwkv_prep(z, z_prev, valid, mu_ref, w0_ref, w2_ref, a0_ref, a2_ref, g2_ref, kkw_ref, kaw_ref, hsum,
                     tri_ref, chunk, prec)
    lane_head = lax.broadcasted_iota(jnp.int32, (chunk, C_BR), 1) // RW_HEAD
    n_chunks = t_tile // chunk
    local = _rwkv_chunks_local(
        [tuple(_heads_bd(pre[key][c * chunk:(c + 1) * chunk], lane_head) for key in _RWKV_LOCAL_KEYS)
         for c in range(n_chunks)], strict_ref, incl_ref, lvl_ref, chunk, prec)
    o_rows = []
    s = s_scr[...] if carry else None
    for c in range(n_chunks):
        if not carry:
            s = _state_bd(wkv0_ref.at[c])
        o_c, s = _rwkv_chunk_apply(s, local[c], pre['wc'][c * chunk:c * chunk + 1], chunk, prec)
        o_rows.append(o_c)
        if not carry:
            for h in range(RW_HEADS):
                wkv1_ref[c, h] = s[h * RW_HEAD:(h + 1) * RW_HEAD, h * RW_HEAD:(h + 1) * RW_HEAD]
    y_ref[...] = _rwkv_post(jnp.concatenate(o_rows, axis=0), pre, rk_ref, gng_ref, gnb_ref, hsum, prec)

    if carry:
        s_scr[...] = s

        @pl.when(i == n_tiles - 1)
        def _():
            for h in range(RW_HEADS):
                wkv1_ref[h] = s[h * RW_HEAD:(h + 1) * RW_HEAD, h * RW_HEAD:(h + 1) * RW_HEAD]


def _rwkv(z3d, col_blk, prev0, wkv0, pw, *, t_tile, chunk, t_valid, carry, prec, layer=None, wkv_out=None):
    bsz, t_len, _ = z3d.shape
    n_tiles = t_len // t_tile
    cs = min(t_tile, max(chunk, 64))
    assert t_tile % cs == 0 and cs % chunk == 0
    idx = np.arange(cs)
    tri = jnp.asarray(((idx[:, None] // chunk == idx[None, :] // chunk)
                       & (idx[None, :] <= idx[:, None])).astype(np.float32))
    hid = np.arange(C_BR) // RW_HEAD
    hsum = jnp.asarray((hid[:, None] == hid[None, :]).astype(np.float32))
    hc = RW_HEADS * chunk
    hh, tt = np.arange(hc) // chunk, np.arange(hc) % chunk
    same_head = hh[:, None] == hh[None, :]
    strict = jnp.asarray((same_head & (tt[None, :] < tt[:, None])).astype(np.float32))
    incl = jnp.asarray((same_head & (tt[None, :] <= tt[:, None])).astype(np.float32))
    lvls = []
    m = 1
    while m < chunk:
        lvls.append(same_head & (tt[:, None] // (2 * m) == tt[None, :] // (2 * m))
                    & (tt[:, None] % (2 * m) >= m) & (tt[None, :] % (2 * m) < m))
        m *= 2
    lvl = jnp.asarray(np.stack(lvls).astype(np.float32))
    vec = lambda n: _full((1, n))
    if carry:
        prev_spec = pl.BlockSpec((None, 1, RW_IN), lambda b, i: (b, 0, 0))
        wkv_spec = pl.BlockSpec((None, RW_HEADS, RW_HEAD, RW_HEAD), lambda b, i: (b, 0, 0, 0))
        extra_in, extra_specs = [], []
    else:
        assert n_tiles == 1
        prev_spec = pl.BlockSpec((None, t_tile, RW_IN), lambda b, i: (b, 0, 0))
        wkv_spec = pl.BlockSpec((None, t_tile // chunk, None, RW_HEADS, RW_HEAD, RW_HEAD),
                                lambda b, i: (b, 0, layer, 0, 0, 0))
        extra_in, extra_specs = [wkv_out], [pl.BlockSpec(memory_space=pl.ANY)]
    operands = [z3d, prev0, wkv0, pw['mu'], pw['w0'], pw['w2'], pw['a0'], pw['a2'], pw['g2'], pw['kk'], pw['ka'],
                pw['rk'], pw['gn_g'], pw['gn_b'], hsum, tri, strict, incl, lvl] + extra_in
    aliases = {} if carry else {len(operands) - 1: 1}
    kern = functools.partial(_rwkv_kernel, t_tile=t_tile, chunk=chunk, t_valid=t_valid, n_tiles=n_tiles, carry=carry,
                             prec=prec)
    return pl.pallas_call(
        kern,
        grid=(bsz, n_tiles),
        in_specs=[pl.BlockSpec((None, t_tile, RW_IN), lambda b, i: (b, i, col_blk)), prev_spec, wkv_spec,
                  vec(RW_IN), vec(C_BR), _full((RW_LW + RW_LA, C_BR)), vec(C_BR), _full((RW_LW + RW_LA, C_BR)),
                  _full((RW_LG, C_BR)), vec(C_BR), vec(C_BR), vec(C_BR), vec(C_BR), vec(C_BR),
                  _full((C_BR, C_BR)), _full((cs, cs)), _full((hc, hc)), _full((hc, hc)), _full(lvl.shape)]
        + extra_specs,
        out_specs=[pl.BlockSpec((None, t_tile, C_BR), lambda b, i: (b, i, 0)), wkv_spec],
        out_shape=[jax.ShapeDtypeStruct((bsz, t_len, C_BR), F32), jax.ShapeDtypeStruct(wkv0.shape, F32)],
        input_output_aliases=aliases,
        scratch_shapes=[pltpu.VMEM((C_BR, C_BR), F32), pltpu.VMEM((1, RW_IN), F32)],
        compiler_params=_params(("parallel", "arbitrary")),
        name="rwkv7",
    )(*operands)


def _s5_kernel(u_ref, h0_ref, lbr_ref, lbi_ref, bb_ref, cc_ref, d_ref, gw_ref, gb_ref,
               y_ref, h1_ref, bu_scr, h_scr, *relayout_scr, bsz, t_tile, n_tiles, time_major):
    i = pl.program_id(0)

    @pl.when(i == 0)
    def _():
        h_scr[...] = h0_ref[...]

    rows = bsz * t_tile
    n_lt = S5_N // LANES
    n_ut = C_BR // LANES
    lane_tile = lambda ref, j: ref[:, j * LANES:(j + 1) * LANES]
    seq_rows = lambda t: pl.ds(t, bsz, stride=t_tile)
    step_rows = lambda t: pl.ds(pl.multiple_of(t * bsz, bsz), bsz)
    u = u_ref[...].reshape(rows, C_BR)
    if time_major:
        bt_scr, tm_scr = relayout_scr
        for j in range(n_ut):
            bt_scr[j] = lane_tile(u, j)

        def to_time_major(t, _):
            for j in range(n_ut):
                tm_scr[j, step_rows(t), :] = bt_scr[j, seq_rows(t), :]
            return 0

        lax.fori_loop(0, t_tile, to_time_major, 0)
        u = jnp.concatenate([tm_scr[j] for j in range(n_ut)], axis=1)
        sl_of = step_rows
    else:
        sl_of = seq_rows
    bu = _mm(u, bb_ref[...])
    for j in range(2 * n_lt):
        bu_scr[j] = lane_tile(bu, j)
    lbr = [jnp.broadcast_to(lane_tile(lbr_ref, j), (bsz, LANES)) for j in range(n_lt)]
    lbi = [jnp.broadcast_to(lane_tile(lbi_ref, j), (bsz, LANES)) for j in range(n_lt)]

    def step(t, carry):
        hr, hi = carry
        sl = sl_of(t)
        new_r, new_i = [], []
        for j in range(n_lt):
            nr = lbr[j] * hr[j] - lbi[j] * hi[j] + bu_scr[j, sl, :]
            ni = lbr[j] * hi[j] + lbi[j] * hr[j] + bu_scr[n_lt + j, sl, :]
            bu_scr[j, sl, :] = nr
            bu_scr[n_lt + j, sl, :] = ni
            new_r.append(nr)
            new_i.append(ni)
        return tuple(new_r), tuple(new_i)

    h_init = (tuple(lane_tile(h_scr, j) for j in range(n_lt)),
              tuple(lane_tile(h_scr, n_lt + j) for j in range(n_lt)))
    hr, hi = lax.fori_loop(0, t_tile, step, h_init)
    for j in range(n_lt):
        h_scr[:, j * LANES:(j + 1) * LANES] = hr[j]
        h_scr[:, (n_lt + j) * LANES:(n_lt + j + 1) * LANES] = hi[j]

    hs = jnp.concatenate([bu_scr[j] for j in range(2 * n_lt)], axis=1)
    y = _mm(hs, cc_ref[...]) + d_ref[...] * u
    y = _gelu_tanh(y)
    y = y * _sigmoid(_mm(y, gw_ref[...]) + gb_ref[...])
    if time_major:
        for j in range(n_ut):
            tm_scr[j] = lane_tile(y, j)

        def to_seq_major(t, _):
            for j in range(n_ut):
                bt_scr[j, seq_rows(t), :] = tm_scr[j, step_rows(t), :]
            return 0

        lax.fori_loop(0, t_tile, to_seq_major, 0)
        y = jnp.concatenate([bt_scr[j] for j in range(n_ut)], axis=1)
    y_ref[...] = y.reshape(y_ref.shape)

    @pl.when(i == n_tiles - 1)
    def _():
        h1_ref[...] = h_scr[...]


def _s5(z, col_blk, h0, ps, *, bsz, t_len, t_tile):
    n_tiles = t_len // t_tile
    rows = bsz * t_tile
    if z.ndim == 3:
        u_spec = pl.BlockSpec((bsz, t_tile, C_BR), lambda i: (0, i, col_blk))
        y_spec = pl.BlockSpec((bsz, t_tile, C_BR), lambda i: (0, i, 0))
        y_shape = (bsz, t_len, C_BR)
    else:
        assert n_tiles == 1
        u_spec = pl.BlockSpec((rows, C_BR), lambda i: (0, col_blk))
        y_spec = pl.BlockSpec((rows, C_BR), lambda i: (0, 0))
        y_shape = (rows, C_BR)
    time_major = z.ndim == 3 and bsz == 8
    relayout_scr = [pltpu.VMEM((C_BR // LANES, rows, LANES), F32)] * 2 if time_major else []
    kern = functools.partial(_s5_kernel, bsz=bsz, t_tile=t_tile, n_tiles=n_tiles, time_major=time_major)
    return pl.pallas_call(
        kern,
        grid=(n_tiles,),
        in_specs=[u_spec, _full((bsz, 2 * S5_N)), _full((1, S5_N)), _full((1, S5_N)),
                  _full((C_BR, 2 * S5_N)), _full((2 * S5_N, C_BR)), _full((1, C_BR)),
                  _full((C_BR, C_BR)), _full((1, C_BR))],
        out_specs=[y_spec, _full((bsz, 2 * S5_N))],
        out_shape=[jax.ShapeDtypeStruct(y_shape, F32), jax.ShapeDtypeStruct((bsz, 2 * S5_N), F32)],
        scratch_shapes=[pltpu.VMEM((2 * S5_N // LANES, rows, LANES), F32), pltpu.VMEM((bsz, 2 * S5_N), F32)]
        + relayout_scr,
        compiler_params=_params(("arbitrary",)),
        name="s5",
    )(z, h0, ps['lb_re'], ps['lb_im'], ps['bb'], ps['cc'], ps['d'], ps['glu_w'], ps['glu_b'])


def _conv_taps(full_scr, w_ref, t_tile):
    lo = CONV_HIST_PAD - CONV_HIST
    sub = 8
    if t_tile % sub:
        acc = None
        for j in range(CONV_W):
            term = full_scr[lo + j:lo + j + t_tile, :] * w_ref[j:j + 1, :]
            acc = term if acc is None else acc + term
        return acc
    acc = None
    for rho in range(sub):
        offs = [o for o in range(rho, lo + CONV_W, sub) if o >= lo]
        rows = t_tile + (sub if rho else 0)
        part = None
        for o in offs:
            term = full_scr[o - rho:o - rho + rows, :] * w_ref[o - lo:o - lo + 1, :]
            part = term if part is None else part + term
        part = part[rho:rho + t_tile]
        acc = part if acc is None else acc + part
    return acc


def _conv_kernel(z_ref, c0_ref, w_ref, b_ref, g_ref, be_ref, _state_buf, y_ref, c1_ref, full_scr,
                 *, t_tile, n_tiles, n_seq):
    i = pl.program_id(1)
    lo = CONV_HIST_PAD - CONV_HIST
    assert n_seq == 1 or n_tiles == 1

    def one_sequence(s):
        def load_history():
            full_scr[0:lo, :] = jnp.zeros((lo, C_BR), F32)
            full_scr[lo:CONV_HIST_PAD, :] = c0_ref[s]

        if n_tiles == 1:
            load_history()
        else:
            pl.when(i == 0)(load_history)
        z = z_ref[s]
        full_scr[CONV_HIST_PAD:CONV_HIST_PAD + t_tile, :] = z[:, 0:C_BR] * _sigmoid(z[:, C_BR:2 * C_BR])
        y = _layer_norm(_conv_taps(full_scr, w_ref, t_tile) + b_ref[...], g_ref[...], be_ref[...])
        y_ref[s] = y * _sigmoid(y)
        hist = full_scr[t_tile:t_tile + CONV_HIST_PAD, :]
        if n_tiles == 1:
            c1_ref[s] = hist[lo:, :]
        else:
            full_scr[0:CONV_HIST_PAD, :] = hist

            @pl.when(i == n_tiles - 1)
            def _():
                c1_ref[s] = hist[lo:, :]

    if n_seq == 1:
        one_sequence(0)
    else:
        def body(s, carry):
            one_sequence(s)
            return carry

        lax.fori_loop(0, n_seq, body, 0)


def _conv(z3d, col_blk, conv0, conv_out, layer, pc, *, t_tile, n_seq):
    bsz, t_len, _ = z3d.shape
    n_tiles = t_len // t_tile
    state_spec = pl.BlockSpec((n_seq, None, CONV_HIST, C_BR), lambda b, i: (b, layer, 0, 0))
    kern = functools.partial(_conv_kernel, t_tile=t_tile, n_tiles=n_tiles, n_seq=n_seq)
    return pl.pallas_call(
        kern,
        grid=(bsz // n_seq, n_tiles),
        in_specs=[pl.BlockSpec((n_seq, t_tile, 2 * C_BR), lambda b, i: (b, i, col_blk)), state_spec,
                  _full((CONV_W, C_BR)), _full((1, C_BR)), _full((1, C_BR)), _full((1, C_BR)),
                  pl.BlockSpec(memory_space=pl.ANY)],
        out_specs=[pl.BlockSpec((n_seq, t_tile, C_BR), lambda b, i: (b, i, 0)), state_spec],
        out_shape=[jax.ShapeDtypeStruct((bsz, t_len, C_BR), F32), jax.ShapeDtypeStruct(conv_out.shape, F32)],
        input_output_aliases={6: 1},
        scratch_shapes=[pltpu.VMEM((CONV_HIST_PAD + t_tile, C_BR), F32)],
        compiler_params=_params(("parallel", "arbitrary")),
        name="conv",
    )(z3d, conv0, pc['w'], pc['b'], pc['ln_g'], pc['ln_b'], conv_out)


def _gmlp_kernel(z_ref, g_ref, b_ref, wm_ref, bias_ref, y_ref, *v_ref):
    z = z_ref[...]
    u = z[:, 0:C_BR]
    v = _layer_norm(z[:, C_BR:2 * C_BR], g_ref[...], b_ref[...])
    if v_ref:
        v_ref[0][...] = v
    vb = v.astype(MM_DTYPE)
    head = lax.broadcasted_iota(jnp.int32, (CHUNK, C_BR), 1) // GM_HEAD
    for c in range(GM_TILE // CHUNK):
        rows = slice(c * CHUNK, (c + 1) * CHUNK)
        s = bias_ref[...]
        for h in range(GM_HEADS):
            s = s + jnp.where(head == h, jnp.dot(wm_ref[h], vb[rows], preferred_element_type=F32), 0.0)
        y_ref[rows, :] = u[rows] * s


def _gmlp(z3d, col_blk, pg, wm, bias, *, emit_v):
    bsz, t_len, _ = z3d.shape
    out_spec = pl.BlockSpec((None, GM_TILE, C_BR), lambda b, i: (b, i, 0))
    n_out = 2 if emit_v else 1
    return pl.pallas_call(
        _gmlp_kernel,
        grid=(bsz, t_len // GM_TILE),
        in_specs=[pl.BlockSpec((None, GM_TILE, 2 * C_BR), lambda b, i: (b, i, col_blk)),
                  _full((1, C_BR)), _full((1, C_BR)), _full((GM_HEADS, CHUNK, CHUNK)), _full((CHUNK, C_BR))],
        out_specs=[out_spec] * n_out,
        out_shape=[jax.ShapeDtypeStruct((bsz, t_len, C_BR), F32)] * n_out,
        compiler_params=_params(("parallel", "parallel")),
        name="gmlp",
    )(z3d, pg['ln_g'], pg['ln_b'], wm, bias)


def _merge_kernel(x_ref, yrw_ref, ys5_ref, ycv_ref, ygm_ref, wg_ref, wb_ref, wo_ref, g_ref, b_ref, o_ref):
    x = x_ref[...]
    xb = x.astype(MM_DTYPE)
    merged = None
    for bidx, y_ref in enumerate((yrw_ref, ys5_ref, ycv_ref, ygm_ref)):
        gate = _sigmoid(jnp.dot(xb, wg_ref[:, bidx * D_MODEL:(bidx + 1) * D_MODEL], preferred_element_type=F32))
        term = gate * _mm(y_ref[...], wb_ref[bidx])
        merged = term if merged is None else merged + term
    o_ref[...] = _layer_norm(DN_ALPHA * x + _mm(merged, wo_ref[...]), g_ref[...], b_ref[...])


def _merge(x2d, ys, big, pm, layer):
    n = x2d.shape[0]
    tm = 256
    row = lambda w: pl.BlockSpec((tm, w), lambda i: (i, 0))
    return pl.pallas_call(
        _merge_kernel,
        grid=(n // tm,),
        in_specs=[row(D_MODEL), row(C_BR), row(C_BR), row(C_BR), row(C_BR),
                  _layer_block((D_MODEL, N_BRANCH * D_MODEL), layer), _layer_block((N_BRANCH, C_BR, D_MODEL), layer),
                  _layer_block((D_MODEL, D_MODEL), layer), _full((1, D_MODEL)), _full((1, D_MODEL))],
        out_specs=row(D_MODEL),
        out_shape=jax.ShapeDtypeStruct((n, D_MODEL), F32),
        compiler_params=_params(("parallel",)),
        name="merge",
    )(x2d, *ys, big['w_gate'], big['w_branch'], big['w_out'], pm['ln1_g'], pm['ln1_b'])


def _moe_kernel(x_ref, wg_ref, bg_ref, wu_ref, wd_ref, g_ref, b_ref, o_ref, hh_scr, *, tm):
    lane = lax.broadcasted_iota(jnp.int32, (tm, LANES), 1)
    x = x_ref[...]
    xb = x.astype(MM_DTYPE)
    logits = _dot(x, wg_ref[...], PREC['route']) + bg_ref[...]
    gl = jnp.where(lane < N_GROUPS, logits, NEG_BIG)
    gmax = jnp.max(gl, axis=-1, keepdims=True)
    g_sel = jnp.min(jnp.where(gl == gmax, lane, LANES), axis=-1, keepdims=True)
    p_group = 1.0 / jnp.sum(jnp.where(lane < N_GROUPS, jnp.exp(gl - gmax), 0.0), axis=-1, keepdims=True)
    first = N_GROUPS + g_sel * E_PER_GROUP
    el = jnp.where((lane >= first) & (lane < first + E_PER_GROUP), logits, NEG_BIG)
    m1 = jnp.max(el, axis=-1, keepdims=True)
    i1 = jnp.min(jnp.where(el == m1, lane, LANES), axis=-1, keepdims=True)
    el2 = jnp.where(lane == i1, NEG_BIG, el)
    m2 = jnp.max(el2, axis=-1, keepdims=True)
    i2 = jnp.min(jnp.where(el2 == m2, lane, LANES), axis=-1, keepdims=True)
    e2 = jnp.exp(m2 - m1)
    w1 = p_group / (1.0 + e2)
    w2 = p_group * e2 / (1.0 + e2)
    for e in range(N_EXPERTS):
        comb_e = jnp.where(i1 == e + N_GROUPS, w1, 0.0) + jnp.where(i2 == e + N_GROUPS, w2, 0.0)
        h = jnp.dot(xb, wu_ref[e], preferred_element_type=F32)
        h1 = h[:, 0:D_EXPERT]
        hh = h1 * _sigmoid(h1) * h[:, D_EXPERT:2 * D_EXPERT] * comb_e
        hh_scr[:, e * D_EXPERT:(e + 1) * D_EXPERT] = hh.astype(MM_DTYPE)
    moe = jnp.dot(hh_scr[...], wd_ref[...], preferred_element_type=F32)
    o_ref[...] = _layer_norm(DN_ALPHA * x + moe, g_ref[...], b_ref[...])


def _moe(x2d, big, pe, layer):
    n = x2d.shape[0]
    tm = 512
    kern = functools.partial(_moe_kernel, tm=tm)
    resident = lambda shape: _layer_block(shape, layer, pipeline_mode=pl.Buffered(1))
    return pl.pallas_call(
        kern,
        grid=(n // tm,),
        in_specs=[pl.BlockSpec((tm, D_MODEL), lambda i: (i, 0)),
                  _full((D_MODEL, LANES)), _full((1, LANES)),
                  resident((N_EXPERTS, D_MODEL, 2 * D_EXPERT)), resident((N_EXPERTS * D_EXPERT, D_MODEL)),
                  _full((1, D_MODEL)), _full((1, D_MODEL))],
        out_specs=pl.BlockSpec((tm, D_MODEL), lambda i: (i, 0)),
        out_shape=jax.ShapeDtypeStruct((n, D_MODEL), F32),
        scratch_shapes=[pltpu.VMEM((tm, N_EXPERTS * D_EXPERT), MM_DTYPE)],
        compiler_params=_params(("parallel",)),
        name="moe",
    )(x2d, pe['wg'], pe['bg'], big['w_up'], big['w_down'], pe['ln2_g'], pe['ln2_b'])


def _block_diag(blocks):
    g, m, n = blocks.shape
    eye = jnp.eye(g, dtype=blocks.dtype)
    return (eye[:, None, :, None] * blocks[:, :, None, :]).reshape(g * m, g * n)


def _prep_stacked(p):
    w_in = p['w_in']
    w_mix = jnp.concatenate([w_in[..., :OFF_S5], w_in[..., OFF_CV:OFF_GM], w_in[..., OFF_GM:OFF_GATE],
                             w_in[..., OFF_S5:OFF_CV]], axis=-1).astype(MM_DTYPE)
    return dict(w_mix=w_mix, w_gate=w_in[..., OFF_GATE:].astype(MM_DTYPE),
                w_branch=p['w_branch'].astype(MM_DTYPE), w_out=p['w_out'].astype(MM_DTYPE),
                w_up=p['moe_w_up'].astype(MM_DTYPE),
                w_down=p['moe_w_down'].astype(MM_DTYPE).reshape(DEPTH, N_EXPERTS * D_EXPERT, D_MODEL))


def _prep_layer(l, p):
    row = lambda a: a.reshape(1, -1).astype(F32)
    zeros_lora = jnp.zeros((RW_LW, C_BR), F32)
    rw = dict(mu=row(p['rw_mu'][l]), w0=row(p['rw_w0'][l]),
              w2=jnp.concatenate([p['rw_w2'][l], zeros_lora], axis=0).astype(MM_DTYPE),
              a0=row(p['rw_a0'][l]),
              a2=jnp.concatenate([zeros_lora, p['rw_a2'][l]], axis=0).astype(MM_DTYPE),
              g2=p['rw_g2'][l].astype(MM_DTYPE), kk=row(p['rw_kk'][l]), ka=row(p['rw_ka'][l]),
              rk=row(p['rw_rk'][l]), gn_g=row(p['rw_gn_g'][l]), gn_b=row(p['rw_gn_b'][l]))
    lr, li = p['s5_lam_re'][l].astype(F32), p['s5_lam_im'][l].astype(F32)
    dt = jnp.exp(p['s5_log_dt'][l].astype(F32))[:, None]
    mag = jnp.exp(lr * dt)
    lb_re, lb_im = mag * jnp.cos(li * dt), mag * jnp.sin(li * dt)
    den = lr * lr + li * li
    q_re = ((lb_re - 1.0) * lr + lb_im * li) / den
    q_im = (lb_im * lr - (lb_re - 1.0) * li) / den
    br, bi = p['s5_b_re'][l].astype(F32), p['s5_b_im'][l].astype(F32)
    bb_re = q_re[..., None] * br - q_im[..., None] * bi
    bb_im = q_re[..., None] * bi + q_im[..., None] * br
    bb = jnp.concatenate([_block_diag(jnp.swapaxes(bb_re, 1, 2)), _block_diag(jnp.swapaxes(bb_im, 1, 2))],
                         axis=1).astype(MM_DTYPE)
    cc = jnp.concatenate([_block_diag(jnp.swapaxes(p['s5_c_re'][l].astype(F32), 1, 2)),
                          -_block_diag(jnp.swapaxes(p['s5_c_im'][l].astype(F32), 1, 2))],
                         axis=0).astype(MM_DTYPE)
    s5 = dict(lb_re=lb_re.reshape(1, S5_N), lb_im=lb_im.reshape(1, S5_N), bb=bb, cc=cc, d=row(p['s5_d'][l]),
              glu_w=p['s5_glu_w'][l].astype(MM_DTYPE), glu_b=row(p['s5_glu_b'][l]))
    cv = dict(w=p['cv_w'][l].astype(F32), b=row(p['cv_b'][l]), ln_g=row(p['cv_ln_g'][l]), ln_b=row(p['cv_ln_b'][l]))
    causal = jnp.tril(jnp.ones((CHUNK, CHUNK), dtype=bool))
    wm = jnp.where(causal, p['gm_ws'][l], 0).astype(F32)
    bias = jnp.repeat(jnp.swapaxes(p['gm_bs'][l], 0, 1), GM_HEAD, axis=1).astype(F32)
    gm = dict(ln_g=row(p['gm_ln_g'][l]), ln_b=row(p['gm_ln_b'][l]), wm=wm, bias=bias)
    mg = dict(ln1_g=row(p['ln1_g'][l]), ln1_b=row(p['ln1_b'][l]))
    pad = LANES - N_GROUPS - N_EXPERTS
    wg = jnp.concatenate([p['moe_wg1'][l], p['moe_wg2'][l], jnp.zeros((D_MODEL, pad), F32)], axis=1).astype(F32)
    bg = jnp.concatenate([p['moe_bg1'][l], p['moe_bg2'][l], jnp.zeros((pad,), F32)]).reshape(1, LANES).astype(F32)
    moe = dict(wg=wg, bg=bg, ln2_g=row(p['ln2_g'][l]), ln2_b=row(p['ln2_b'][l]))
    return dict(rw=rw, s5=s5, cv=cv, gm=gm, mg=mg, moe=moe)


def _gmlp_group(z3d, pg, *, is_prompt):
    bsz, t_len, n_cols = z3d.shape
    if is_prompt:
        y, = _gmlp(z3d, P_GM // (2 * C_BR), pg, pg['wm'].astype(MM_DTYPE), pg['bias'], emit_v=False)
        return y, None
    reps = CHUNK // t_len
    wm = jnp.stack([jnp.kron(jnp.eye(reps, dtype=F32), pg['wm'][h, :t_len, :t_len]) for h in range(GM_HEADS)])
    bias = jnp.tile(pg['bias'][:t_len], (reps, 1))
    y, v = _gmlp(z3d.reshape(1, bsz * t_len, n_cols), P_GM // (2 * C_BR), pg, wm.astype(MM_DTYPE), bias,
                 emit_v=True)
    return y.reshape(bsz, t_len, C_BR), v.reshape(bsz, t_len, C_BR)


def _run_group(x, wkv0, shift0, s5r0, s5i0, conv0, big, layers, *, is_prompt):
    bsz, t_len, _ = x.shape
    n = bsz * t_len
    x2d = x.reshape(n, D_MODEL)
    outs = []
    conv_buf = jnp.zeros_like(conv0)
    if not is_prompt:
        wkv0_g = wkv0.reshape((bsz // RW_GROUP, RW_GROUP) + wkv0.shape[1:])
        wkv_buf = jnp.zeros_like(wkv0_g)
        grp_rows = lambda a: a.reshape((bsz // RW_GROUP, RW_GROUP * a.shape[1]) + a.shape[2:])
    for l, lp in enumerate(layers):
        z2d = _inproj(x2d, big['w_mix'], l)
        z3d = z2d.reshape(bsz, t_len, N_MIX)
        sh0 = shift0[:, l].reshape(bsz, 1, RW_IN)
        if is_prompt:
            y_rw, wkv1 = _rwkv(z3d, P_RW // RW_IN, sh0, wkv0[:, l], lp['rw'], t_tile=512, chunk=64,
                               t_valid=t_len, carry=True, prec=PREC)
        else:
            z_rw = grp_rows(jnp.pad(z3d[:, :, P_RW:P_RW + RW_IN], ((0, 0), (0, RW_PAD - t_len), (0, 0))))
            prev0 = grp_rows(jnp.pad(sh0, ((0, 0), (0, RW_PAD - 1), (0, 0))))
            y_rw, wkv_buf = _rwkv(z_rw, 0, prev0, wkv0_g, lp['rw'], t_tile=RW_GROUP * RW_PAD, chunk=RW_PAD,
                                  t_valid=t_len, carry=False, prec=PREC_SHORT, layer=l, wkv_out=wkv_buf)
            y_rw = y_rw.reshape(bsz, RW_PAD, C_BR)[:, :t_len]
            wkv1 = None
        shift1 = z3d[:, t_len - 1, P_RW:P_RW + RW_IN]
        h0 = jnp.concatenate([s5r0[:, l].reshape(bsz, S5_N), s5i0[:, l].reshape(bsz, S5_N)], axis=1)
        if is_prompt:
            y_s5, h1 = _s5(z3d, P_S5 // C_BR, h0, lp['s5'], bsz=bsz, t_len=t_len, t_tile=128)
        else:
            y_s5, h1 = _s5(z2d, P_S5 // C_BR, h0, lp['s5'], bsz=bsz, t_len=t_len, t_tile=t_len)
        s5r1 = h1[:, :S5_N].reshape(bsz, S5_GROUPS, S5_STATE)
        s5i1 = h1[:, S5_N:].reshape(bsz, S5_GROUPS, S5_STATE)
        y_cv, conv_buf = _conv(z3d, P_CV // (2 * C_BR), conv0, conv_buf, l, lp['cv'],
                               t_tile=256 if is_prompt else t_len, n_seq=1 if is_prompt else CV_GROUP)
        y_gm, v_gm = _gmlp_group(z3d, lp['gm'], is_prompt=is_prompt)
        ys = [y.reshape(n, C_BR) for y in (y_rw, y_s5, y_cv, y_gm)]
        x2d = _merge(x2d, ys, big, lp['mg'], l)
        x2d = _moe(x2d, big, lp['moe'], l)
        outs.append((wkv1, shift1, s5r1, s5i1, v_gm))
    stack = lambda i: None if outs[0][i] is None else jnp.stack([o[i] for o in outs], axis=1)
    wkv = stack(0) if is_prompt else wkv_buf.reshape(wkv0.shape)
    return x2d.reshape(bsz, t_len, D_MODEL), (wkv, stack(1), stack(2), stack(3), conv_buf, stack(4))


def kernel(x_prompt, x_sample, state_rwkv_wkv, state_rwkv_shift, state_s5_re, state_s5_im, cache_conv,
           w_in, rw_mu, rw_w0, rw_w2, rw_a0, rw_a2, rw_g2, rw_kk, rw_ka, rw_rk, rw_gn_g, rw_gn_b,
           s5_lam_re, s5_lam_im, s5_log_dt, s5_b_re, s5_b_im, s5_c_re, s5_c_im, s5_d, s5_glu_w, s5_glu_b,
           cv_w, cv_b, cv_ln_g, cv_ln_b, gm_ln_g, gm_ln_b, gm_ws, gm_bs,
           w_branch, w_out, ln1_g, ln1_b,
           moe_wg1, moe_bg1, moe_wg2, moe_bg2, moe_w_up, moe_w_down, ln2_g, ln2_b):
    p = dict(w_in=w_in, rw_mu=rw_mu, rw_w0=rw_w0, rw_w2=rw_w2, rw_a0=rw_a0, rw_a2=rw_a2, rw_g2=rw_g2,
             rw_kk=rw_kk, rw_ka=rw_ka, rw_rk=rw_rk, rw_gn_g=rw_gn_g, rw_gn_b=rw_gn_b,
             s5_lam_re=s5_lam_re, s5_lam_im=s5_lam_im, s5_log_dt=s5_log_dt, s5_b_re=s5_b_re, s5_b_im=s5_b_im,
             s5_c_re=s5_c_re, s5_c_im=s5_c_im, s5_d=s5_d, s5_glu_w=s5_glu_w, s5_glu_b=s5_glu_b,
             cv_w=cv_w, cv_b=cv_b, cv_ln_g=cv_ln_g, cv_ln_b=cv_ln_b, gm_ln_g=gm_ln_g, gm_ln_b=gm_ln_b,
             gm_ws=gm_ws, gm_bs=gm_bs, w_branch=w_branch, w_out=w_out, ln1_g=ln1_g, ln1_b=ln1_b,
             moe_wg1=moe_wg1, moe_bg1=moe_bg1, moe_wg2=moe_wg2, moe_bg2=moe_bg2, moe_w_up=moe_w_up,
             moe_w_down=moe_w_down, ln2_g=ln2_g, ln2_b=ln2_b)
    layers = [_prep_layer(l, p) for l in range(DEPTH)]
    big = _prep_stacked(p)
    bp = x_prompt.shape[0]
    dt = x_prompt.dtype
    y_prompt, (p_wkv, p_shift, p_s5r, p_s5i, p_conv, _) = _run_group(
        x_prompt,
        jnp.zeros((bp, DEPTH, RW_HEADS, RW_HEAD, RW_HEAD), dt),
        jnp.zeros((bp, DEPTH, RW_IN), dt),
        jnp.zeros((bp, DEPTH, S5_GROUPS, S5_STATE), dt),
        jnp.zeros((bp, DEPTH, S5_GROUPS, S5_STATE), dt),
        jnp.zeros((bp, DEPTH, CONV_HIST, C_BR), dt),
        big, layers, is_prompt=True)
    y_sample, (s_wkv, s_shift, s_s5r, s_s5i, s_conv, s_gmv) = _run_group(
        x_sample, state_rwkv_wkv, state_rwkv_shift, state_s5_re, state_s5_im, cache_conv, big, layers,
        is_prompt=False)
    return (y_prompt, y_sample, p_wkv, p_shift, p_s5r, p_s5i, p_conv,
            s_wkv, s_shift, s_s5r, s_s5i, s_conv, s_gmv)
```

```python
import functools
import math

import numpy as np
import jax
import jax.numpy as jnp
from jax import lax
from jax.experimental import pallas as pl
from jax.experimental.pallas import tpu as pltpu

D_MODEL = 1024
DEPTH = 4
N_BRANCH = 4
C_BR = D_MODEL // 4
RW_HEAD = 64
RW_HEADS = C_BR // RW_HEAD
RW_LW = 64
RW_LA = 64
RW_LG = 128
RW_IN = 3 * C_BR + RW_LW + RW_LA + RW_LG
RW_GN_EPS = 64e-5
RW_PAD = 8
RW_GROUP = 16
S5_GW = 16
S5_GROUPS = C_BR // S5_GW
S5_STATE = 64
S5_N = S5_GROUPS * S5_STATE
CONV_W = 31
CONV_HIST = CONV_W - 1
CONV_HIST_PAD = 32
CHUNK = 128
GM_TILE = 4 * CHUNK
GM_HEADS = 4
GM_HEAD = C_BR // GM_HEADS
N_GROUPS = 4
E_PER_GROUP = 4
N_EXPERTS = N_GROUPS * E_PER_GROUP
D_EXPERT = D_MODEL // 4
LN_EPS = 1e-5
DN_ALPHA = (2 * DEPTH) ** 0.25
OFF_S5 = RW_IN
OFF_CV = OFF_S5 + C_BR
OFF_GM = OFF_CV + 2 * C_BR
OFF_GATE = OFF_GM + 2 * C_BR
N_IN = OFF_GATE + N_BRANCH * D_MODEL
P_RW = 0
P_CV = P_RW + RW_IN
P_GM = P_CV + 2 * C_BR
P_S5 = P_GM + 2 * C_BR
N_MIX = P_S5 + C_BR

LANES = 128
VMEM_LIMIT = 56 * 1024 * 1024

F32 = jnp.float32
BF16 = jnp.bfloat16
MM_DTYPE = jnp.bfloat16
HI = lax.Precision.HIGHEST
NEG_BIG = -1e30


def _mm(a, b):
    return jnp.dot(a.astype(MM_DTYPE), b.astype(MM_DTYPE), preferred_element_type=F32)


def _split_bf16(a):
    hi = a.astype(BF16)
    return hi, (a - hi.astype(F32)).astype(BF16)


_NN = ((1,), (0,))
_NT = ((1,), (1,))
_TN = ((0,), (0,))


def _dot(a, b, mode, dims=_NN, exact=None):
    dn = (dims, ((), ()))
    if mode == 'hi':
        return lax.dot_general(a, b, dn, precision=HI, preferred_element_type=F32)
    f = lambda x, y: lax.dot_general(x, y, dn, preferred_element_type=F32)
    if mode == 'bf16':
        return f(a.astype(BF16), b.astype(BF16))
    assert mode == 'x3'
    if exact == 'a':
        b_hi, b_lo = _split_bf16(b)
        a = a.astype(BF16)
        return f(a, b_hi) + f(a, b_lo)
    if exact == 'b':
        a_hi, a_lo = _split_bf16(a)
        b = b.astype(BF16)
        return f(a_hi, b) + f(a_lo, b)
    a_hi, a_lo = _split_bf16(a)
    b_hi, b_lo = _split_bf16(b)
    return f(a_hi, b_hi) + (f(a_hi, b_lo) + f(a_lo, b_hi))


PREC = dict(cumsum='x3', headsum='x3', amat='bf16', inv='bf16', state='bf16', apply='bf16', update='bf16', route='x3')
PREC_SHORT = dict(PREC, update='x3')


def _sigmoid(x):
    return jax.nn.sigmoid(x)


def _softplus(x):
    return jnp.maximum(x, 0.0) + jnp.log1p(jnp.exp(-jnp.abs(x)))


def _gelu_tanh(x):
    return 0.5 * x * (1.0 + jnp.tanh(math.sqrt(2.0 / math.pi) * (x + 0.044715 * (x * x * x))))


def _layer_norm(x, g, b):
    mu = jnp.mean(x, axis=-1, keepdims=True)
    d = x - mu
    var = jnp.mean(d * d, axis=-1, keepdims=True)
    return d * lax.rsqrt(var + LN_EPS) * g + b


def _params(sem):
    return pltpu.CompilerParams(dimension_semantics=sem, vmem_limit_bytes=VMEM_LIMIT)


def _full(shape):
    nd = len(shape)
    return pl.BlockSpec(shape, lambda *_: (0,) * nd)


def _layer_block(shape, layer, **kwargs):
    nd = len(shape)
    return pl.BlockSpec((None,) + tuple(shape), lambda *_: (layer,) + (0,) * nd, **kwargs)


def _inproj_kernel(x_ref, w_ref, z_ref):
    z_ref[...] = _mm(x_ref[...], w_ref[...])


def _inproj(x2d, w_bf16, layer):
    n = x2d.shape[0]
    tm = 512
    return pl.pallas_call(
        _inproj_kernel,
        grid=(n // tm,),
        in_specs=[pl.BlockSpec((tm, D_MODEL), lambda i: (i, 0)), _layer_block((D_MODEL, N_MIX), layer)],
        out_specs=pl.BlockSpec((tm, N_MIX), lambda i: (i, 0)),
        out_shape=jax.ShapeDtypeStruct((n, N_MIX), F32),
        compiler_params=_params(("parallel",)),
        name="inproj",
    )(x2d, w_bf16)


def _heads_bd(x, lane_head):
    return jnp.concatenate([jnp.where(lane_head == h, x, 0.0) for h in range(RW_HEADS)], axis=0)


def _state_bd(wkv):
    zeros_blk = jnp.zeros((RW_HEAD, RW_HEAD), F32)
    return jnp.concatenate(
        [jnp.concatenate([wkv[h] if g == h else zeros_blk for g in range(RW_HEADS)], axis=1)
         for h in range(RW_HEADS)], axis=0)


def _rwkv_prep(z, z_prev, valid, mu_ref, w0_ref, w2_ref, a0_ref, a2_ref, g2_ref, kkw_ref, kaw_ref, hsum, tri_ref,
               chunk, prec):
    zs = z + mu_ref[...] * (z_prev - z)
    r = zs[:, 0:C_BR]
    k = zs[:, C_BR:2 * C_BR]
    v = zs[:, 2 * C_BR:3 * C_BR]
    lwla = zs[:, 3 * C_BR:3 * C_BR + RW_LW + RW_LA]
    lg = zs[:, 3 * C_BR + RW_LW + RW_LA:]
    w_log = -_softplus(-(w0_ref[...] + _mm(jnp.tanh(lwla), w2_ref[...]))) - 0.5
    ld = -jnp.exp(w_log)
    a = _sigmoid(a0_ref[...] + _mm(lwla, a2_ref[...]))
    g = _mm(_sigmoid(lg), g2_ref[...])
    kk = k * kkw_ref[...]
    kk = kk * lax.rsqrt(jnp.maximum(_dot(kk * kk, hsum, prec['headsum'], exact='b'), 1e-24))
    k2 = k * (1.0 + (a - 1.0) * kaw_ref[...])
    bv = kk * a
    if valid is not None:
        ld = jnp.where(valid, ld, 0.0)
        k2 = jnp.where(valid, k2, 0.0)
        v = jnp.where(valid, v, 0.0)
        bv = jnp.where(valid, bv, 0.0)
    t_tile = z.shape[0]
    n = tri_ref.shape[0]
    lc = jnp.concatenate([_dot(tri_ref[...], ld[i * n:(i + 1) * n], prec['cumsum'], exact='a')
                          for i in range(t_tile // n)], axis=0)
    lend = jnp.concatenate([jnp.broadcast_to(lc[(c + 1) * chunk - 1:(c + 1) * chunk], (chunk, C_BR))
                            for c in range(t_tile // chunk)], axis=0)
    e_end = jnp.exp(lend - lc)
    e_neg = jnp.exp(-lc)
    return dict(r=r, k2=k2, v=v, g=g, rt=r * jnp.exp(lc), kkt=kk * jnp.exp(lc - ld), kh=k2 * e_neg, bh=bv * e_neg,
                kw=k2 * e_end, bw=bv * e_end, wc=jnp.exp(lend))


def _rwkv_chunks_local(chunks, strict_ref, incl_ref, lvl_ref, chunk, prec):
    hc = RW_HEADS * chunk
    nk = RW_HEADS * RW_HEAD
    n = range(len(chunks))
    kkt, rt, kh, bh, vv, kw, bw = (list(x) for x in zip(*chunks))
    amat = [_dot(jnp.concatenate([kkt[c], rt[c]], axis=0), jnp.concatenate([kh[c], bh[c]], axis=0),
                 prec['amat'], _NT) for c in n]
    strict = strict_ref[...] != 0.0
    incl = incl_ref[...] != 0.0
    a_kk = [jnp.where(strict, amat[c][0:hc, 0:hc], 0.0) for c in n]
    a_kb = [jnp.where(strict, amat[c][0:hc, hc:2 * hc], 0.0) for c in n]
    a_rk = [jnp.where(incl, amat[c][hc:2 * hc, 0:hc], 0.0) for c in n]
    a_rb = [jnp.where(incl, amat[c][hc:2 * hc, hc:2 * hc], 0.0) for c in n]
    av = [_dot(jnp.concatenate([a_kk[c], a_rk[c]], axis=0), vv[c], prec['apply']) for c in n]
    ri = lax.broadcasted_iota(jnp.int32, (hc, hc), 0)
    cj = lax.broadcasted_iota(jnp.int32, (hc, hc), 1)
    eye = jnp.where(ri == cj, 1.0, 0.0)
    lvl0 = lvl_ref[0] != 0.0
    t_inv = [eye - jnp.where(lvl0, a_kb[c], 0.0) for c in n]
    for lv in range(1, lvl_ref.shape[0]):
        lvl = lvl_ref[lv] != 0.0
        half = [_dot(t_inv[c], jnp.where(lvl, a_kb[c], 0.0), prec['inv']) for c in n]
        t_inv = [t_inv[c] - _dot(half[c], t_inv[c], prec['inv']) for c in n]
    gu = [_dot(t_inv[c], jnp.concatenate([kkt[c], av[c][0:hc]], axis=1), prec['apply']) for c in n]
    pu = [_dot(a_rb[c], gu[c], prec['apply']) for c in n]
    mc = [_dot(gu[c][:, 0:nk], bw[c], prec['update'], _TN) for c in n]
    nn = [_dot(jnp.concatenate([vv[c], gu[c][:, nk:2 * nk]], axis=0), jnp.concatenate([kw[c], -bw[c]], axis=0),
               prec['update'], _TN) for c in n]
    return [(rt[c] - pu[c][:, 0:nk], av[c][hc:2 * hc] - pu[c][:, nk:2 * nk], mc[c], nn[c]) for c in n]


def _rwkv_chunks_direct(chunks, states, wcs, strict_ref, incl_ref, lvl_ref, chunk, prec):
    hc = RW_HEADS * chunk
    n = range(len(chunks))
    kkt, rt, kh, bh, vv, kw, bw = (list(x) for x in zip(*chunks))
    lhs = [jnp.concatenate([kkt[c], rt[c]], axis=0) for c in n]
    amat = [_dot(lhs[c], jnp.concatenate([kh[c], bh[c]], axis=0), prec['amat'], _NT) for c in n]
    ls = [_dot(lhs[c], states[c], prec['state'], _NT) for c in n]
    strict = strict_ref[...] != 0.0
    incl = incl_ref[...] != 0.0
    a_kk = [jnp.where(strict, amat[c][0:hc, 0:hc], 0.0) for c in n]
    a_kb = [jnp.where(strict, amat[c][0:hc, hc:2 * hc], 0.0) for c in n]
    a_rk = [jnp.where(incl, amat[c][hc:2 * hc, 0:hc], 0.0) for c in n]
    a_rb = [jnp.where(incl, amat[c][hc:2 * hc, hc:2 * hc], 0.0) for c in n]
    av = [_dot(jnp.concatenate([a_kk[c], a_rk[c]], axis=0), vv[c], prec['apply']) for c in n]
    ri = lax.broadcasted_iota(jnp.int32, (hc, hc), 0)
    cj = lax.broadcasted_iota(jnp.int32, (hc, hc), 1)
    eye = jnp.where(ri == cj, 1.0, 0.0)
    lvl0 = lvl_ref[0] != 0.0
    t_inv = [eye - jnp.where(lvl0, a_kb[c], 0.0) for c in n]
    for lv in range(1, lvl_ref.shape[0]):
        lvl = lvl_ref[lv] != 0.0
        half = [_dot(t_inv[c], jnp.where(lvl, a_kb[c], 0.0), prec['inv']) for c in n]
        t_inv = [t_inv[c] - _dot(half[c], t_inv[c], prec['inv']) for c in n]
    u = [_dot(t_inv[c], ls[c][0:hc] + av[c][0:hc], prec['apply']) for c in n]
    o = [ls[c][hc:2 * hc] + av[c][hc:2 * hc] - _dot(a_rb[c], u[c], prec['apply']) for c in n]
    s_new = [states[c] * wcs[c] + _dot(jnp.concatenate([vv[c], u[c]], axis=0),
                                       jnp.concatenate([kw[c], -bw[c]], axis=0), prec['update'], _TN) for c in n]
    out = []
    for c in n:
        o_c = o[c][0:chunk]
        for h in range(1, RW_HEADS):
            o_c = o_c + o[c][h * chunk:(h + 1) * chunk]
        out.append((o_c, s_new[c]))
    return out


def _rwkv_chunk_apply(s, local, wc, chunk, prec):
    p, o0, mc, nn = local
    o = _dot(p, s, prec['state'], _NT) + o0
    o_c = o[0:chunk]
    for h in range(1, RW_HEADS):
        o_c = o_c + o[h * chunk:(h + 1) * chunk]
    return o_c, s * wc - _dot(s, mc, prec['state']) + nn


def _rwkv_post(o, pre, rk_ref, gng_ref, gnb_ref, hsum, prec):
    inv_n = 1.0 / RW_HEAD
    o_mu = _dot(o, hsum, prec['headsum'], exact='b') * inv_n
    od = o - o_mu
    o_var = _dot(od * od, hsum, prec['headsum'], exact='b') * inv_n
    on = od * lax.rsqrt(o_var + RW_GN_EPS) * gng_ref[...] + gnb_ref[...]
    bonus = _dot(pre['r'] * pre['k2'] * rk_ref[...], hsum, prec['headsum'], exact='b') * pre['v']
    return (on + bonus) * pre['g']


_RWKV_LOCAL_KEYS = ('kkt', 'rt', 'kh', 'bh', 'v', 'kw', 'bw')


def _rwkv_kernel(z_ref, prev0_ref, wkv0_ref, mu_ref, w0_ref, w2_ref, a0_ref, a2_ref, g2_ref, kkw_ref, kaw_ref,
                 rk_ref, gng_ref, gnb_ref, hsum_ref, tri_ref, strict_ref, incl_ref, lvl_ref, *rest,
                 t_tile, chunk, t_valid, n_tiles, carry, prec):
    y_ref, wkv1_ref, s_scr, prev_scr = rest[-4:]
    i = pl.program_id(1)
    z = z_ref[...]
    row = lax.broadcasted_iota(jnp.int32, (t_tile, 1), 0)
    if carry:
        assert t_valid == t_tile * n_tiles

        @pl.when(i == 0)
        def _():
            s_scr[...] = _state_bd(wkv0_ref)
            prev_scr[...] = prev0_ref[...]

        z_prev = jnp.where(row == 0, prev_scr[...], pltpu.roll(z, 1, 0))
        prev_scr[...] = z[t_tile - 1:t_tile, :]
        valid = None
    else:
        step = row % chunk
        z_prev = jnp.where(step == 0, prev0_ref[...], pltpu.roll(z, 1, 0))
        valid = step < t_valid
    hsum = hsum_ref[...]
    pre = _rwkv_prep(z, z_prev, valid, mu_ref, w0_ref, w2_ref, a0_ref, a2_ref, g2_ref, kkw_ref, kaw_ref, hsum,
                     tri_ref, chunk, prec)
    lane_head = lax.broadcasted_iota(jnp.int32, (chunk, C_BR), 1) // RW_HEAD
    n_chunks = t_tile // chunk
    chunks = [tuple(_heads_bd(pre[key][c * chunk:(c + 1) * chunk], lane_head) for key in _RWKV_LOCAL_KEYS)
              for c in range(n_chunks)]
    wcs = [pre['wc'][c * chunk:c * chunk + 1] for c in range(n_chunks)]
    o_rows = []
    if carry:
        local = _rwkv_chunks_local(chunks, strict_ref, incl_ref, lvl_ref, chunk, prec)
        s = s_scr[...]
        for c in range(n_chunks):
            o_c, s = _rwkv_chunk_apply(s, local[c], wcs[c], chunk, prec)
            o_rows.append(o_c)
    else:
        states = [_state_bd(wkv0_ref.at[c]) for c in range(n_chunks)]
        for c, (o_c, s_c) in enumerate(_rwkv_chunks_direct(chunks, states, wcs, strict_ref, incl_ref, lvl_ref,
                                                           chunk, prec)):
            o_rows.append(o_c)
            for h in range(RW_HEADS):
                wkv1_ref[c, h] = s_c[h * RW_HEAD:(h + 1) * RW_HEAD, h * RW_HEAD:(h + 1) * RW_HEAD]
    y_ref[...] = _rwkv_post(jnp.concatenate(o_rows, axis=0), pre, rk_ref, gng_ref, gnb_ref, hsum, prec)

    if carry:
        s_scr[...] = s

        @pl.when(i == n_tiles - 1)
        def _():
            for h in range(RW_HEADS):
                wkv1_ref[h] = s[h * RW_HEAD:(h + 1) * RW_HEAD, h * RW_HEAD:(h + 1) * RW_HEAD]


def _rwkv(z3d, col_blk, prev0, wkv0, pw, *, t_tile, chunk, t_valid, carry, prec, layer=None, wkv_out=None):
    bsz, t_len, _ = z3d.shape
    n_tiles = t_len // t_tile
    cs = min(t_tile, max(chunk, 64))
    assert t_tile % cs == 0 and cs % chunk == 0
    idx = np.arange(cs)
    tri = jnp.asarray(((idx[:, None] // chunk == idx[None, :] // chunk)
                       & (idx[None, :] <= idx[:, None])).astype(np.float32))
    hid = np.arange(C_BR) // RW_HEAD
    hsum = jnp.asarray((hid[:, None] == hid[None, :]).astype(np.float32))
    hc = RW_HEADS * chunk
    hh, tt = np.arange(hc) // chunk, np.arange(hc) % chunk
    same_head = hh[:, None] == hh[None, :]
    strict = jnp.asarray((same_head & (tt[None, :] < tt[:, None])).astype(np.float32))
    incl = jnp.asarray((same_head & (tt[None, :] <= tt[:, None])).astype(np.float32))
    lvls = []
    m = 1
    while m < chunk:
        lvls.append(same_head & (tt[:, None] // (2 * m) == tt[None, :] // (2 * m))
                    & (tt[:, None] % (2 * m) >= m) & (tt[None, :] % (2 * m) < m))
        m *= 2
    lvl = jnp.asarray(np.stack(lvls).astype(np.float32))
    vec = lambda n: _full((1, n))
    if carry:
        prev_spec = pl.BlockSpec((None, 1, RW_IN), lambda b, i: (b, 0, 0))
        wkv_spec = pl.BlockSpec((None, RW_HEADS, RW_HEAD, RW_HEAD), lambda b, i: (b, 0, 0, 0))
        extra_in, extra_specs = [], []
    else:
        assert n_tiles == 1
        prev_spec = pl.BlockSpec((None, t_tile, RW_IN), lambda b, i: (b, 0, 0))
        wkv_spec = pl.BlockSpec((None, t_tile // chunk, None, RW_HEADS, RW_HEAD, RW_HEAD),
                                lambda b, i: (b, 0, layer, 0, 0, 0))
        extra_in, extra_specs = [wkv_out], [pl.BlockSpec(memory_space=pl.ANY)]
    operands = [z3d, prev0, wkv0, pw['mu'], pw['w0'], pw['w2'], pw['a0'], pw['a2'], pw['g2'], pw['kk'], pw['ka'],
                pw['rk'], pw['gn_g'], pw['gn_b'], hsum, tri, strict, incl, lvl] + extra_in
    aliases = {} if carry else {len(operands) - 1: 1}
    kern = functools.partial(_rwkv_kernel, t_tile=t_tile, chunk=chunk, t_valid=t_valid, n_tiles=n_tiles, carry=carry,
                             prec=prec)
    return pl.pallas_call(
        kern,
        grid=(bsz, n_tiles),
        in_specs=[pl.BlockSpec((None, t_tile, RW_IN), lambda b, i: (b, i, col_blk)), prev_spec, wkv_spec,
                  vec(RW_IN), vec(C_BR), _full((RW_LW + RW_LA, C_BR)), vec(C_BR), _full((RW_LW + RW_LA, C_BR)),
                  _full((RW_LG, C_BR)), vec(C_BR), vec(C_BR), vec(C_BR), vec(C_BR), vec(C_BR),
                  _full((C_BR, C_BR)), _full((cs, cs)), _full((hc, hc)), _full((hc, hc)), _full(lvl.shape)]
        + extra_specs,
        out_specs=[pl.BlockSpec((None, t_tile, C_BR), lambda b, i: (b, i, 0)), wkv_spec],
        out_shape=[jax.ShapeDtypeStruct((bsz, t_len, C_BR), F32), jax.ShapeDtypeStruct(wkv0.shape, F32)],
        input_output_aliases=aliases,
        scratch_shapes=[pltpu.VMEM((C_BR, C_BR), F32), pltpu.VMEM((1, RW_IN), F32)],
        compiler_params=_params(("parallel", "arbitrary")),
        name="rwkv7",
    )(*operands)


def _s5_kernel(u_ref, h0_ref, lbr_ref, lbi_ref, bb_ref, cc_ref, d_ref, gw_ref, gb_ref,
               y_ref, h1_ref, bu_scr, h_scr, *relayout_scr, bsz, t_tile, n_tiles, time_major):
    i = pl.program_id(0)

    @pl.when(i == 0)
    def _():
        h_scr[...] = h0_ref[...]

    rows = bsz * t_tile
    n_lt = S5_N // LANES
    n_ut = C_BR // LANES
    lane_tile = lambda ref, j: ref[:, j * LANES:(j + 1) * LANES]
    seq_rows = lambda t: pl.ds(t, bsz, stride=t_tile)
    step_rows = lambda t: pl.ds(pl.multiple_of(t * bsz, bsz), bsz)
    u = u_ref[...].reshape(rows, C_BR)
    if time_major:
        bt_scr, tm_scr = relayout_scr
        for j in range(n_ut):
            bt_scr[j] = lane_tile(u, j)

        def to_time_major(t, _):
            for j in range(n_ut):
                tm_scr[j, step_rows(t), :] = bt_scr[j, seq_rows(t), :]
            return 0

        lax.fori_loop(0, t_tile, to_time_major, 0, unroll=4)
        u = jnp.concatenate([tm_scr[j] for j in range(n_ut)], axis=1)
        sl_of = step_rows
    else:
        sl_of = seq_rows
    bu = _mm(u, bb_ref[...])
    for j in range(2 * n_lt):
        bu_scr[j] = lane_tile(bu, j)
    lbr = [jnp.broadcast_to(lane_tile(lbr_ref, j), (bsz, LANES)) for j in range(n_lt)]
    lbi = [jnp.broadcast_to(lane_tile(lbi_ref, j), (bsz, LANES)) for j in range(n_lt)]

    def step(t, carry):
        hr, hi = carry
        sl = sl_of(t)
        new_r, new_i = [], []
        for j in range(n_lt):
            nr = lbr[j] * hr[j] - lbi[j] * hi[j] + bu_scr[j, sl, :]
            ni = lbr[j] * hi[j] + lbi[j] * hr[j] + bu_scr[n_lt + j, sl, :]
            bu_scr[j, sl, :] = nr
            bu_scr[n_lt + j, sl, :] = ni
            new_r.append(nr)
            new_i.append(ni)
        return tuple(new_r), tuple(new_i)

    h_init = (tuple(lane_tile(h_scr, j) for j in range(n_lt)),
              tuple(lane_tile(h_scr, n_lt + j) for j in range(n_lt)))
    hr, hi = lax.fori_loop(0, t_tile, step, h_init)
    for j in range(n_lt):
        h_scr[:, j * LANES:(j + 1) * LANES] = hr[j]
        h_scr[:, (n_lt + j) * LANES:(n_lt + j + 1) * LANES] = hi[j]

    hs = jnp.concatenate([bu_scr[j] for j in range(2 * n_lt)], axis=1)
    y = _mm(hs, cc_ref[...]) + d_ref[...] * u
    y = _gelu_tanh(y)
    y = y * _sigmoid(_mm(y, gw_ref[...]) + gb_ref[...])
    if time_major:
        for j in range(n_ut):
            tm_scr[j] = lane_tile(y, j)

        def to_seq_major(t, _):
            for j in range(n_ut):
                bt_scr[j, seq_rows(t), :] = tm_scr[j, step_rows(t), :]
            return 0

        lax.fori_loop(0, t_tile, to_seq_major, 0, unroll=4)
        y = jnp.concatenate([bt_scr[j] for j in range(n_ut)], axis=1)
    y_ref[...] = y.reshape(y_ref.shape)

    @pl.when(i == n_tiles - 1)
    def _():
        h1_ref[...] = h_scr[...]


def _s5(z, col_blk, h0, ps, *, bsz, t_len, t_tile):
    n_tiles = t_len // t_tile
    rows = bsz * t_tile
    if z.ndim == 3:
        u_spec = pl.BlockSpec((bsz, t_tile, C_BR), lambda i: (0, i, col_blk))
        y_spec = pl.BlockSpec((bsz, t_tile, C_BR), lambda i: (0, i, 0))
        y_shape = (bsz, t_len, C_BR)
    else:
        assert n_tiles == 1
        u_spec = pl.BlockSpec((rows, C_BR), lambda i: (0, col_blk))
        y_spec = pl.BlockSpec((rows, C_BR), lambda i: (0, 0))
        y_shape = (rows, C_BR)
    time_major = z.ndim == 3 and bsz == 8
    relayout_scr = [pltpu.VMEM((C_BR // LANES, rows, LANES), F32)] * 2 if time_major else []
    kern = functools.partial(_s5_kernel, bsz=bsz, t_tile=t_tile, n_tiles=n_tiles, time_major=time_major)
    return pl.pallas_call(
        kern,
        grid=(n_tiles,),
        in_specs=[u_spec, _full((bsz, 2 * S5_N)), _full((1, S5_N)), _full((1, S5_N)),
                  _full((C_BR, 2 * S5_N)), _full((2 * S5_N, C_BR)), _full((1, C_BR)),
                  _full((C_BR, C_BR)), _full((1, C_BR))],
        out_specs=[y_spec, _full((bsz, 2 * S5_N))],
        out_shape=[jax.ShapeDtypeStruct(y_shape, F32), jax.ShapeDtypeStruct((bsz, 2 * S5_N), F32)],
        scratch_shapes=[pltpu.VMEM((2 * S5_N // LANES, rows, LANES), F32), pltpu.VMEM((bsz, 2 * S5_N), F32)]
        + relayout_scr,
        compiler_params=_params(("arbitrary",)),
        name="s5",
    )(z, h0, ps['lb_re'], ps['lb_im'], ps['bb'], ps['cc'], ps['d'], ps['glu_w'], ps['glu_b'])


def _conv_taps(full_scr, w_ref, t_tile):
    lo = CONV_HIST_PAD - CONV_HIST
    sub = 8
    assert t_tile % sub == 0
    acc = None
    for rho in range(sub):
        offs = [o for o in range(rho, lo + CONV_W, sub) if o >= lo]
        rows = t_tile + (sub if rho else 0)
        part = None
        for o in offs:
            term = full_scr[o - rho:o - rho + rows, :] * w_ref[o - lo:o - lo + 1, :]
            part = term if part is None else part + term
        part = part[rho:rho + t_tile]
        acc = part if acc is None else acc + part
    return acc


def _conv_kernel(z_ref, c0_ref, w_ref, b_ref, g_ref, be_ref, _state_buf, y_ref, c1_ref, full_scr, *, t_tile, n_tiles):
    i = pl.program_id(1)
    lo = CONV_HIST_PAD - CONV_HIST

    @pl.when(i == 0)
    def _():
        full_scr[0:lo, :] = jnp.zeros((lo, C_BR), F32)
        full_scr[lo:CONV_HIST_PAD, :] = c0_ref[...]

    z = z_ref[...]
    full_scr[CONV_HIST_PAD:CONV_HIST_PAD + t_tile, :] = z[:, 0:C_BR] * _sigmoid(z[:, C_BR:2 * C_BR])
    y = _layer_norm(_conv_taps(full_scr, w_ref, t_tile) + b_ref[...], g_ref[...], be_ref[...])
    y_ref[...] = y * _sigmoid(y)
    hist = full_scr[t_tile:t_tile + CONV_HIST_PAD, :]
    full_scr[0:CONV_HIST_PAD, :] = hist

    @pl.when(i == n_tiles - 1)
    def _():
        c1_ref[...] = hist[lo:, :]


def _conv(z3d, col_blk, conv0, conv_out, layer, pc, *, t_tile):
    bsz, t_len, _ = z3d.shape
    n_tiles = t_len // t_tile
    state_spec = pl.BlockSpec((None, None, CONV_HIST, C_BR), lambda b, i: (b, layer, 0, 0))
    kern = functools.partial(_conv_kernel, t_tile=t_tile, n_tiles=n_tiles)
    return pl.pallas_call(
        kern,
        grid=(bsz, n_tiles),
        in_specs=[pl.BlockSpec((None, t_tile, 2 * C_BR), lambda b, i: (b, i, col_blk)), state_spec,
                  _full((CONV_W, C_BR)), _full((1, C_BR)), _full((1, C_BR)), _full((1, C_BR)),
                  pl.BlockSpec(memory_space=pl.ANY)],
        out_specs=[pl.BlockSpec((None, t_tile, C_BR), lambda b, i: (b, i, 0)), state_spec],
        out_shape=[jax.ShapeDtypeStruct((bsz, t_len, C_BR), F32), jax.ShapeDtypeStruct(conv_out.shape, F32)],
        input_output_aliases={6: 1},
        scratch_shapes=[pltpu.VMEM((CONV_HIST_PAD + t_tile, C_BR), F32)],
        compiler_params=_params(("parallel", "arbitrary")),
        name="conv",
    )(z3d, conv0, pc['w'], pc['b'], pc['ln_g'], pc['ln_b'], conv_out)


def _conv_short_kernel(z_ref, c0_ref, w_ref, b_ref, g_ref, be_ref, _state_buf, y_ref, c1_ref, in_scr, out_scr,
                       *, bsz, t_len):
    n_lt = C_BR // LANES
    z = z_ref[...]
    c = z[:, 0:C_BR] * _sigmoid(z[:, C_BR:2 * C_BR])
    for j in range(n_lt):
        in_scr[j] = c[:, j * LANES:(j + 1) * LANES]
    step_rows = lambda t: pl.ds(t, bsz, stride=t_len)
    hist = lambda r: c0_ref[:, r * C_BR:(r + 1) * C_BR]
    new = [jnp.concatenate([in_scr[j, step_rows(t), :] for j in range(n_lt)], axis=1) for t in range(t_len)]
    full = lambda r: hist(r) if r < CONV_HIST else new[r - CONV_HIST]
    for t in range(t_len):
        acc = b_ref[...] + full(t) * w_ref[0:1, :]
        for j in range(1, CONV_W):
            acc = acc + full(t + j) * w_ref[j:j + 1, :]
        y = _layer_norm(acc, g_ref[...], be_ref[...])
        y = y * _sigmoid(y)
        for j in range(n_lt):
            out_scr[j, step_rows(t), :] = y[:, j * LANES:(j + 1) * LANES]
    y_ref[...] = jnp.concatenate([out_scr[j] for j in range(n_lt)], axis=1)
    for r in range(CONV_HIST):
        c1_ref[:, r * C_BR:(r + 1) * C_BR] = full(r + t_len)


def _conv_short(z2d, col_blk, conv0, conv_out, layer, pc, *, bsz, t_len):
    rows = bsz * t_len
    width = CONV_HIST * C_BR
    state_spec = pl.BlockSpec((bsz, width), lambda i: (0, layer))
    kern = functools.partial(_conv_short_kernel, bsz=bsz, t_len=t_len)
    return pl.pallas_call(
        kern,
        grid=(1,),
        in_specs=[pl.BlockSpec((rows, 2 * C_BR), lambda i: (0, col_blk)), state_spec,
                  _full((CONV_W, C_BR)), _full((1, C_BR)), _full((1, C_BR)), _full((1, C_BR)),
                  pl.BlockSpec(memory_space=pl.ANY)],
        out_specs=[pl.BlockSpec((rows, C_BR), lambda i: (0, 0)), state_spec],
        out_shape=[jax.ShapeDtypeStruct((rows, C_BR), F32), jax.ShapeDtypeStruct(conv_out.shape, F32)],
        input_output_aliases={6: 1},
        scratch_shapes=[pltpu.VMEM((C_BR // LANES, rows, LANES), F32)] * 2,
        compiler_params=_params(("arbitrary",)),
        name="conv_short",
    )(z2d, conv0, pc['w'], pc['b'], pc['ln_g'], pc['ln_b'], conv_out)


def _gmlp_kernel(z_ref, g_ref, b_ref, wm_ref, bias_ref, y_ref, *v_ref):
    z = z_ref[...]
    u = z[:, 0:C_BR]
    v = _layer_norm(z[:, C_BR:2 * C_BR], g_ref[...], b_ref[...])
    if v_ref:
        v_ref[0][...] = v
    vb = v.astype(MM_DTYPE)
    head = lax.broadcasted_iota(jnp.int32, (CHUNK, C_BR), 1) // GM_HEAD
    for c in range(GM_TILE // CHUNK):
        rows = slice(c * CHUNK, (c + 1) * CHUNK)
        s = bias_ref[...]
        for h in range(GM_HEADS):
            s = s + jnp.where(head == h, jnp.dot(wm_ref[h], vb[rows], preferred_element_type=F32), 0.0)
        y_ref[rows, :] = u[rows] * s


def _gmlp(z3d, col_blk, pg, wm, bias, *, emit_v):
    bsz, t_len, _ = z3d.shape
    out_spec = pl.BlockSpec((None, GM_TILE, C_BR), lambda b, i: (b, i, 0))
    n_out = 2 if emit_v else 1
    return pl.pallas_call(
        _gmlp_kernel,
        grid=(bsz, t_len // GM_TILE),
        in_specs=[pl.BlockSpec((None, GM_TILE, 2 * C_BR), lambda b, i: (b, i, col_blk)),
                  _full((1, C_BR)), _full((1, C_BR)), _full((GM_HEADS, CHUNK, CHUNK)), _full((CHUNK, C_BR))],
        out_specs=[out_spec] * n_out,
        out_shape=[jax.ShapeDtypeStruct((bsz, t_len, C_BR), F32)] * n_out,
        compiler_params=_params(("parallel", "parallel")),
        name="gmlp",
    )(z3d, pg['ln_g'], pg['ln_b'], wm, bias)


def _merge_kernel(x_ref, yrw_ref, ys5_ref, ycv_ref, ygm_ref, wg_ref, wb_ref, wo_ref, g_ref, b_ref, o_ref):
    x = x_ref[...]
    xb = x.astype(MM_DTYPE)
    merged = None
    for bidx, y_ref in enumerate((yrw_ref, ys5_ref, ycv_ref, ygm_ref)):
        gate = _sigmoid(jnp.dot(xb, wg_ref[:, bidx * D_MODEL:(bidx + 1) * D_MODEL], preferred_element_type=F32))
        term = gate * _mm(y_ref[...], wb_ref[bidx])
        merged = term if merged is None else merged + term
    o_ref[...] = _layer_norm(DN_ALPHA * x + _mm(merged, wo_ref[...]), g_ref[...], b_ref[...])


def _merge(x2d, ys, big, pm, layer):
    n = x2d.shape[0]
    tm = 256
    row = lambda w: pl.BlockSpec((tm, w), lambda i: (i, 0))
    return pl.pallas_call(
        _merge_kernel,
        grid=(n // tm,),
        in_specs=[row(D_MODEL), row(C_BR), row(C_BR), row(C_BR), row(C_BR),
                  _layer_block((D_MODEL, N_BRANCH * D_MODEL), layer), _layer_block((N_BRANCH, C_BR, D_MODEL), layer),
                  _layer_block((D_MODEL, D_MODEL), layer), _full((1, D_MODEL)), _full((1, D_MODEL))],
        out_specs=row(D_MODEL),
        out_shape=jax.ShapeDtypeStruct((n, D_MODEL), F32),
        compiler_params=_params(("parallel",)),
        name="merge",
    )(x2d, *ys, big['w_gate'], big['w_branch'], big['w_out'], pm['ln1_g'], pm['ln1_b'])


def _moe_kernel(x_ref, wg_ref, bg_ref, wu_ref, wd_ref, g_ref, b_ref, o_ref, hh_scr, *, tm):
    lane = lax.broadcasted_iota(jnp.int32, (tm, LANES), 1)
    x = x_ref[...]
    xb = x.astype(MM_DTYPE)
    logits = _dot(x, wg_ref[...], PREC['route']) + bg_ref[...]
    gl = jnp.where(lane < N_GROUPS, logits, NEG_BIG)
    gmax = jnp.max(gl, axis=-1, keepdims=True)
    g_sel = jnp.min(jnp.where(gl == gmax, lane, LANES), axis=-1, keepdims=True)
    p_group = 1.0 / jnp.sum(jnp.where(lane < N_GROUPS, jnp.exp(gl - gmax), 0.0), axis=-1, keepdims=True)
    first = N_GROUPS + g_sel * E_PER_GROUP
    el = jnp.where((lane >= first) & (lane < first + E_PER_GROUP), logits, NEG_BIG)
    m1 = jnp.max(el, axis=-1, keepdims=True)
    i1 = jnp.min(jnp.where(el == m1, lane, LANES), axis=-1, keepdims=True)
    el2 = jnp.where(lane == i1, NEG_BIG, el)
    m2 = jnp.max(el2, axis=-1, keepdims=True)
    i2 = jnp.min(jnp.where(el2 == m2, lane, LANES), axis=-1, keepdims=True)
    e2 = jnp.exp(m2 - m1)
    w1 = p_group / (1.0 + e2)
    w2 = p_group * e2 / (1.0 + e2)
    for e in range(N_EXPERTS):
        comb_e = jnp.where(i1 == e + N_GROUPS, w1, 0.0) + jnp.where(i2 == e + N_GROUPS, w2, 0.0)
        h = jnp.dot(xb, wu_ref[e], preferred_element_type=F32)
        h1 = h[:, 0:D_EXPERT]
        hh = h1 * _sigmoid(h1) * h[:, D_EXPERT:2 * D_EXPERT] * comb_e
        hh_scr[:, e * D_EXPERT:(e + 1) * D_EXPERT] = hh.astype(MM_DTYPE)
    moe = jnp.dot(hh_scr[...], wd_ref[...], preferred_element_type=F32)
    o_ref[...] = _layer_norm(DN_ALPHA * x + moe, g_ref[...], b_ref[...])


def _moe(x2d, big, pe, layer):
    n = x2d.shape[0]
    tm = 512
    kern = functools.partial(_moe_kernel, tm=tm)
    resident = lambda shape: _layer_block(shape, layer, pipeline_mode=pl.Buffered(1))
    return pl.pallas_call(
        kern,
        grid=(n // tm,),
        in_specs=[pl.BlockSpec((tm, D_MODEL), lambda i: (i, 0)),
                  _full((D_MODEL, LANES)), _full((1, LANES)),
                  resident((N_EXPERTS, D_MODEL, 2 * D_EXPERT)), resident((N_EXPERTS * D_EXPERT, D_MODEL)),
                  _full((1, D_MODEL)), _full((1, D_MODEL))],
        out_specs=pl.BlockSpec((tm, D_MODEL), lambda i: (i, 0)),
        out_shape=jax.ShapeDtypeStruct((n, D_MODEL), F32),
        scratch_shapes=[pltpu.VMEM((tm, N_EXPERTS * D_EXPERT), MM_DTYPE)],
        compiler_params=_params(("parallel",)),
        name="moe",
    )(x2d, pe['wg'], pe['bg'], big['w_up'], big['w_down'], pe['ln2_g'], pe['ln2_b'])


def _block_diag(blocks):
    g, m, n = blocks.shape
    eye = jnp.eye(g, dtype=blocks.dtype)
    return (eye[:, None, :, None] * blocks[:, :, None, :]).reshape(g * m, g * n)


def _prep_stacked(p):
    w_in = p['w_in']
    w_mix = jnp.concatenate([w_in[..., :OFF_S5], w_in[..., OFF_CV:OFF_GM], w_in[..., OFF_GM:OFF_GATE],
                             w_in[..., OFF_S5:OFF_CV]], axis=-1).astype(MM_DTYPE)
    return dict(w_mix=w_mix, w_gate=w_in[..., OFF_GATE:].astype(MM_DTYPE),
                w_branch=p['w_branch'].astype(MM_DTYPE), w_out=p['w_out'].astype(MM_DTYPE),
                w_up=p['moe_w_up'].astype(MM_DTYPE),
                w_down=p['moe_w_down'].astype(MM_DTYPE).reshape(DEPTH, N_EXPERTS * D_EXPERT, D_MODEL))


def _prep_layer(l, p):
    row = lambda a: a.reshape(1, -1).astype(F32)
    zeros_lora = jnp.zeros((RW_LW, C_BR), F32)
    rw = dict(mu=row(p['rw_mu'][l]), w0=row(p['rw_w0'][l]),
              w2=jnp.concatenate([p['rw_w2'][l], zeros_lora], axis=0).astype(MM_DTYPE),
              a0=row(p['rw_a0'][l]),
              a2=jnp.concatenate([zeros_lora, p['rw_a2'][l]], axis=0).astype(MM_DTYPE),
              g2=p['rw_g2'][l].astype(MM_DTYPE), kk=row(p['rw_kk'][l]), ka=row(p['rw_ka'][l]),
              rk=row(p['rw_rk'][l]), gn_g=row(p['rw_gn_g'][l]), gn_b=row(p['rw_gn_b'][l]))
    lr, li = p['s5_lam_re'][l].astype(F32), p['s5_lam_im'][l].astype(F32)
    dt = jnp.exp(p['s5_log_dt'][l].astype(F32))[:, None]
    mag = jnp.exp(lr * dt)
    lb_re, lb_im = mag * jnp.cos(li * dt), mag * jnp.sin(li * dt)
    den = lr * lr + li * li
    q_re = ((lb_re - 1.0) * lr + lb_im * li) / den
    q_im = (lb_im * lr - (lb_re - 1.0) * li) / den
    br, bi = p['s5_b_re'][l].astype(F32), p['s5_b_im'][l].astype(F32)
    bb_re = q_re[..., None] * br - q_im[..., None] * bi
    bb_im = q_re[..., None] * bi + q_im[..., None] * br
    bb = jnp.concatenate([_block_diag(jnp.swapaxes(bb_re, 1, 2)), _block_diag(jnp.swapaxes(bb_im, 1, 2))],
                         axis=1).astype(MM_DTYPE)
    cc = jnp.concatenate([_block_diag(jnp.swapaxes(p['s5_c_re'][l].astype(F32), 1, 2)),
                          -_block_diag(jnp.swapaxes(p['s5_c_im'][l].astype(F32), 1, 2))],
                         axis=0).astype(MM_DTYPE)
    s5 = dict(lb_re=lb_re.reshape(1, S5_N), lb_im=lb_im.reshape(1, S5_N), bb=bb, cc=cc, d=row(p['s5_d'][l]),
              glu_w=p['s5_glu_w'][l].astype(MM_DTYPE), glu_b=row(p['s5_glu_b'][l]))
    cv = dict(w=p['cv_w'][l].astype(F32), b=row(p['cv_b'][l]), ln_g=row(p['cv_ln_g'][l]), ln_b=row(p['cv_ln_b'][l]))
    causal = jnp.tril(jnp.ones((CHUNK, CHUNK), dtype=bool))
    wm = jnp.where(causal, p['gm_ws'][l], 0).astype(F32)
    bias = jnp.repeat(jnp.swapaxes(p['gm_bs'][l], 0, 1), GM_HEAD, axis=1).astype(F32)
    gm = dict(ln_g=row(p['gm_ln_g'][l]), ln_b=row(p['gm_ln_b'][l]), wm=wm, bias=bias)
    mg = dict(ln1_g=row(p['ln1_g'][l]), ln1_b=row(p['ln1_b'][l]))
    pad = LANES - N_GROUPS - N_EXPERTS
    wg = jnp.concatenate([p['moe_wg1'][l], p['moe_wg2'][l], jnp.zeros((D_MODEL, pad), F32)], axis=1).astype(F32)
    bg = jnp.concatenate([p['moe_bg1'][l], p['moe_bg2'][l], jnp.zeros((pad,), F32)]).reshape(1, LANES).astype(F32)
    moe = dict(wg=wg, bg=bg, ln2_g=row(p['ln2_g'][l]), ln2_b=row(p['ln2_b'][l]))
    return dict(rw=rw, s5=s5, cv=cv, gm=gm, mg=mg, moe=moe)


def _gmlp_group(z3d, pg, *, is_prompt):
    bsz, t_len, n_cols = z3d.shape
    if is_prompt:
        y, = _gmlp(z3d, P_GM // (2 * C_BR), pg, pg['wm'].astype(MM_DTYPE), pg['bias'], emit_v=False)
        return y, None
    reps = CHUNK // t_len
    wm = jnp.stack([jnp.kron(jnp.eye(reps, dtype=F32), pg['wm'][h, :t_len, :t_len]) for h in range(GM_HEADS)])
    bias = jnp.tile(pg['bias'][:t_len], (reps, 1))
    y, v = _gmlp(z3d.reshape(1, bsz * t_len, n_cols), P_GM // (2 * C_BR), pg, wm.astype(MM_DTYPE), bias,
                 emit_v=True)
    return y.reshape(bsz, t_len, C_BR), v.reshape(bsz, t_len, C_BR)


def _run_group(x, wkv0, shift0, s5r0, s5i0, conv0, big, layers, *, is_prompt):
    bsz, t_len, _ = x.shape
    n = bsz * t_len
    x2d = x.reshape(n, D_MODEL)
    outs = []
    conv_shape = conv0.shape
    if not is_prompt:
        conv0 = conv0.reshape(bsz, DEPTH * CONV_HIST * C_BR)
    conv_buf = jnp.zeros_like(conv0)
    if not is_prompt:
        wkv0_g = wkv0.reshape((bsz // RW_GROUP, RW_GROUP) + wkv0.shape[1:])
        wkv_buf = jnp.zeros_like(wkv0_g)
        grp_rows = lambda a: a.reshape((bsz // RW_GROUP, RW_GROUP * a.shape[1]) + a.shape[2:])
    for l, lp in enumerate(layers):
        z2d = _inproj(x2d, big['w_mix'], l)
        z3d = z2d.reshape(bsz, t_len, N_MIX)
        sh0 = shift0[:, l].reshape(bsz, 1, RW_IN)
        if is_prompt:
            y_rw, wkv1 = _rwkv(z3d, P_RW // RW_IN, sh0, wkv0[:, l], lp['rw'], t_tile=512, chunk=64,
                               t_valid=t_len, carry=True, prec=PREC)
        else:
            z_rw = grp_rows(jnp.pad(z3d[:, :, P_RW:P_RW + RW_IN], ((0, 0), (0, RW_PAD - t_len), (0, 0))))
            prev0 = grp_rows(jnp.pad(sh0, ((0, 0), (0, RW_PAD - 1), (0, 0))))
            y_rw, wkv_buf = _rwkv(z_rw, 0, prev0, wkv0_g, lp['rw'], t_tile=RW_GROUP * RW_PAD, chunk=RW_PAD,
                                  t_valid=t_len, carry=False, prec=PREC_SHORT, layer=l, wkv_out=wkv_buf)
            y_rw = y_rw.reshape(bsz, RW_PAD, C_BR)[:, :t_len]
            wkv1 = None
        shift1 = z3d[:, t_len - 1, P_RW:P_RW + RW_IN]
        h0 = jnp.concatenate([s5r0[:, l].reshape(bsz, S5_N), s5i0[:, l].reshape(bsz, S5_N)], axis=1)
        if is_prompt:
            y_s5, h1 = _s5(z3d, P_S5 // C_BR, h0, lp['s5'], bsz=bsz, t_len=t_len, t_tile=128)
        else:
            y_s5, h1 = _s5(z2d, P_S5 // C_BR, h0, lp['s5'], bsz=bsz, t_len=t_len, t_tile=t_len)
        s5r1 = h1[:, :S5_N].reshape(bsz, S5_GROUPS, S5_STATE)
        s5i1 = h1[:, S5_N:].reshape(bsz, S5_GROUPS, S5_STATE)
        if is_prompt:
            y_cv, conv_buf = _conv(z3d, P_CV // (2 * C_BR), conv0, conv_buf, l, lp['cv'], t_tile=256)
        else:
            y_cv, conv_buf = _conv_short(z2d, P_CV // (2 * C_BR), conv0, conv_buf, l, lp['cv'], bsz=bsz,
                                         t_len=t_len)
        y_gm, v_gm = _gmlp_group(z3d, lp['gm'], is_prompt=is_prompt)
        ys = [y.reshape(n, C_BR) for y in (y_rw, y_s5, y_cv, y_gm)]
        x2d = _merge(x2d, ys, big, lp['mg'], l)
        x2d = _moe(x2d, big, lp['moe'], l)
        outs.append((wkv1, shift1, s5r1, s5i1, v_gm))
    stack = lambda i: None if outs[0][i] is None else jnp.stack([o[i] for o in outs], axis=1)
    wkv = stack(0) if is_prompt else wkv_buf.reshape(wkv0.shape)
    return (x2d.reshape(bsz, t_len, D_MODEL),
            (wkv, stack(1), stack(2), stack(3), conv_buf.reshape(conv_shape), stack(4)))


def kernel(x_prompt, x_sample, state_rwkv_wkv, state_rwkv_shift, state_s5_re, state_s5_im, cache_conv,
           w_in, rw_mu, rw_w0, rw_w2, rw_a0, rw_a2, rw_g2, rw_kk, rw_ka, rw_rk, rw_gn_g, rw_gn_b,
           s5_lam_re, s5_lam_im, s5_log_dt, s5_b_re, s5_b_im, s5_c_re, s5_c_im, s5_d, s5_glu_w, s5_glu_b,
           cv_w, cv_b, cv_ln_g, cv_ln_b, gm_ln_g, gm_ln_b, gm_ws, gm_bs,
           w_branch, w_out, ln1_g, ln1_b,
           moe_wg1, moe_bg1, moe_wg2, moe_bg2, moe_w_up, moe_w_down, ln2_g, ln2_b):
    p = dict(w_in=w_in, rw_mu=rw_mu, rw_w0=rw_w0, rw_w2=rw_w2, rw_a0=rw_a0, rw_a2=rw_a2, rw_g2=rw_g2,
             rw_kk=rw_kk, rw_ka=rw_ka, rw_rk=rw_rk, rw_gn_g=rw_gn_g, rw_gn_b=rw_gn_b,
             s5_lam_re=s5_lam_re, s5_lam_im=s5_lam_im, s5_log_dt=s5_log_dt, s5_b_re=s5_b_re, s5_b_im=s5_b_im,
             s5_c_re=s5_c_re, s5_c_im=s5_c_im, s5_d=s5_d, s5_glu_w=s5_glu_w, s5_glu_b=s5_glu_b,
             cv_w=cv_w, cv_b=cv_b, cv_ln_g=cv_ln_g, cv_ln_b=cv_ln_b, gm_ln_g=gm_ln_g, gm_ln_b=gm_ln_b,
             gm_ws=gm_ws, gm_bs=gm_bs, w_branch=w_branch, w_out=w_out, ln1_g=ln1_g, ln1_b=ln1_b,
             moe_wg1=moe_wg1, moe_bg1=moe_bg1, moe_wg2=moe_wg2, moe_bg2=moe_bg2, moe_w_up=moe_w_up,
             moe_w_down=moe_w_down, ln2_g=ln2_g, ln2_b=ln2_b)
    layers = [_prep_layer(l, p) for l in range(DEPTH)]
    big = _prep_stacked(p)
    bp = x_prompt.shape[0]
    dt = x_prompt.dtype
    y_prompt, (p_wkv, p_shift, p_s5r, p_s5i, p_conv, _) = _run_group(
        x_prompt,
        jnp.zeros((bp, DEPTH, RW_HEADS, RW_HEAD, RW_HEAD), dt),
        jnp.zeros((bp, DEPTH, RW_IN), dt),
        jnp.zeros((bp, DEPTH, S5_GROUPS, S5_STATE), dt),
        jnp.zeros((bp, DEPTH, S5_GROUPS, S5_STATE), dt),
        jnp.zeros((bp, DEPTH, CONV_HIST, C_BR), dt),
        big, layers, is_prompt=True)
    y_sample, (s_wkv, s_shift, s_s5r, s_s5i, s_conv, s_gmv) = _run_group(
        x_sample, state_rwkv_wkv, state_rwkv_shift, state_s5_re, state_s5_im, cache_conv, big, layers,
        is_prompt=False)
    return (y_prompt, y_sample, p_wkv, p_shift, p_s5r, p_s5i, p_conv,
            s_wkv, s_shift, s_s5r, s_s5i, s_conv, s_gmv)
```

```python
import functools
import math

import numpy as np
import jax
import jax.numpy as jnp
from jax import lax
from jax.experimental import pallas as pl
from jax.experimental.pallas import tpu as pltpu

D_MODEL = 1024
DEPTH = 4
N_BRANCH = 4
C_BR = D_MODEL // 4
RW_HEAD = 64
RW_HEADS = C_BR // RW_HEAD
RW_LW = 64
RW_LA = 64
RW_LG = 128
RW_IN = 3 * C_BR + RW_LW + RW_LA + RW_LG
RW_GN_EPS = 64e-5
RW_PAD = 8
RW_GROUP = 16
S5_GW = 16
S5_GROUPS = C_BR // S5_GW
S5_STATE = 64
S5_N = S5_GROUPS * S5_STATE
CONV_W = 31
CONV_HIST = CONV_W - 1
CONV_HIST_PAD = 32
CHUNK = 128
GM_TILE = 4 * CHUNK
GM_HEADS = 4
GM_HEAD = C_BR // GM_HEADS
N_GROUPS = 4
E_PER_GROUP = 4
N_EXPERTS = N_GROUPS * E_PER_GROUP
D_EXPERT = D_MODEL // 4
LN_EPS = 1e-5
DN_ALPHA = (2 * DEPTH) ** 0.25
OFF_S5 = RW_IN
OFF_CV = OFF_S5 + C_BR
OFF_GM = OFF_CV + 2 * C_BR
OFF_GATE = OFF_GM + 2 * C_BR
N_IN = OFF_GATE + N_BRANCH * D_MODEL
P_RW = 0
P_CV = P_RW + RW_IN
P_GM = P_CV + 2 * C_BR
P_S5 = P_GM + 2 * C_BR
N_MIX = P_S5 + C_BR

LANES = 128
VMEM_LIMIT = 56 * 1024 * 1024

F32 = jnp.float32
BF16 = jnp.bfloat16
MM_DTYPE = jnp.bfloat16
HI = lax.Precision.HIGHEST
NEG_BIG = -1e30


def _mm(a, b):
    return jnp.dot(a.astype(MM_DTYPE), b.astype(MM_DTYPE), preferred_element_type=F32)


def _split_bf16(a):
    hi = a.astype(BF16)
    return hi, (a - hi.astype(F32)).astype(BF16)


_NN = ((1,), (0,))
_NT = ((1,), (1,))
_TN = ((0,), (0,))


def _dot(a, b, mode, dims=_NN, exact=None):
    dn = (dims, ((), ()))
    if mode == 'hi':
        return lax.dot_general(a, b, dn, precision=HI, preferred_element_type=F32)
    f = lambda x, y: lax.dot_general(x, y, dn, preferred_element_type=F32)
    if mode == 'bf16':
        return f(a.astype(BF16), b.astype(BF16))
    assert mode == 'x3'
    if exact == 'a':
        b_hi, b_lo = _split_bf16(b)
        a = a.astype(BF16)
        return f(a, b_hi) + f(a, b_lo)
    if exact == 'b':
        a_hi, a_lo = _split_bf16(a)
        b = b.astype(BF16)
        return f(a_hi, b) + f(a_lo, b)
    a_hi, a_lo = _split_bf16(a)
    b_hi, b_lo = _split_bf16(b)
    return f(a_hi, b_hi) + (f(a_hi, b_lo) + f(a_lo, b_hi))


PREC = dict(cumsum='x3', headsum='x3', amat='bf16', inv='bf16', state='bf16', apply='bf16', update='bf16', route='x3')
PREC_SHORT = dict(PREC, update='x3')


def _sigmoid(x):
    return jax.nn.sigmoid(x)


def _softplus(x):
    return jnp.maximum(x, 0.0) + jnp.log1p(jnp.exp(-jnp.abs(x)))


def _gelu_tanh(x):
    return 0.5 * x * (1.0 + jnp.tanh(math.sqrt(2.0 / math.pi) * (x + 0.044715 * (x * x * x))))


def _layer_norm(x, g, b):
    mu = jnp.mean(x, axis=-1, keepdims=True)
    d = x - mu
    var = jnp.mean(d * d, axis=-1, keepdims=True)
    return d * lax.rsqrt(var + LN_EPS) * g + b


def _params(sem):
    return pltpu.CompilerParams(dimension_semantics=sem, vmem_limit_bytes=VMEM_LIMIT)


def _full(shape):
    nd = len(shape)
    return pl.BlockSpec(shape, lambda *_: (0,) * nd)


def _layer_block(shape, layer, **kwargs):
    nd = len(shape)
    return pl.BlockSpec((None,) + tuple(shape), lambda *_: (layer,) + (0,) * nd, **kwargs)


def _inproj_kernel(x_ref, w_ref, z_ref):
    z_ref[...] = _mm(x_ref[...], w_ref[...])


def _inproj(x2d, w_bf16, layer):
    n = x2d.shape[0]
    tm = 512
    return pl.pallas_call(
        _inproj_kernel,
        grid=(n // tm,),
        in_specs=[pl.BlockSpec((tm, D_MODEL), lambda i: (i, 0)), _layer_block((D_MODEL, N_MIX), layer)],
        out_specs=pl.BlockSpec((tm, N_MIX), lambda i: (i, 0)),
        out_shape=jax.ShapeDtypeStruct((n, N_MIX), F32),
        compiler_params=_params(("parallel",)),
        name="inproj",
    )(x2d, w_bf16)


def _heads_bd(x, lane_head):
    return jnp.concatenate([jnp.where(lane_head == h, x, 0.0) for h in range(RW_HEADS)], axis=0)


def _state_bd(wkv):
    zeros_blk = jnp.zeros((RW_HEAD, RW_HEAD), F32)
    return jnp.concatenate(
        [jnp.concatenate([wkv[h] if g == h else zeros_blk for g in range(RW_HEADS)], axis=1)
         for h in range(RW_HEADS)], axis=0)


def _rwkv_prep(z, z_prev, valid, mu_ref, w0_ref, w2_ref, a0_ref, a2_ref, g2_ref, kkw_ref, kaw_ref, hsum, tri_ref,
               chunk, prec):
    zs = z + mu_ref[...] * (z_prev - z)
    r = zs[:, 0:C_BR]
    k = zs[:, C_BR:2 * C_BR]
    v = zs[:, 2 * C_BR:3 * C_BR]
    lwla = zs[:, 3 * C_BR:3 * C_BR + RW_LW + RW_LA]
    lg = zs[:, 3 * C_BR + RW_LW + RW_LA:]
    w_log = -_softplus(-(w0_ref[...] + _mm(jnp.tanh(lwla), w2_ref[...]))) - 0.5
    ld = -jnp.exp(w_log)
    a = _sigmoid(a0_ref[...] + _mm(lwla, a2_ref[...]))
    g = _mm(_sigmoid(lg), g2_ref[...])
    kk = k * kkw_ref[...]
    kk = kk * lax.rsqrt(jnp.maximum(_dot(kk * kk, hsum, prec['headsum'], exact='b'), 1e-24))
    k2 = k * (1.0 + (a - 1.0) * kaw_ref[...])
    bv = kk * a
    if valid is not None:
        ld = jnp.where(valid, ld, 0.0)
        k2 = jnp.where(valid, k2, 0.0)
        v = jnp.where(valid, v, 0.0)
        bv = jnp.where(valid, bv, 0.0)
    t_tile = z.shape[0]
    n = tri_ref.shape[0]
    lc = jnp.concatenate([_dot(tri_ref[...], ld[i * n:(i + 1) * n], prec['cumsum'], exact='a')
                          for i in range(t_tile // n)], axis=0)
    lend = jnp.concatenate([jnp.broadcast_to(lc[(c + 1) * chunk - 1:(c + 1) * chunk], (chunk, C_BR))
                            for c in range(t_tile // chunk)], axis=0)
    e_end = jnp.exp(lend - lc)
    e_neg = jnp.exp(-lc)
    return dict(r=r, k2=k2, v=v, g=g, rt=r * jnp.exp(lc), kkt=kk * jnp.exp(lc - ld), kh=k2 * e_neg, bh=bv * e_neg,
                kw=k2 * e_end, bw=bv * e_end, wc=jnp.exp(lend))


def _rwkv_chunks_local(chunks, strict_ref, incl_ref, lvl_ref, chunk, prec):
    hc = RW_HEADS * chunk
    nk = RW_HEADS * RW_HEAD
    n = range(len(chunks))
    kkt, rt, kh, bh, vv, kw, bw = (list(x) for x in zip(*chunks))
    amat = [_dot(jnp.concatenate([kkt[c], rt[c]], axis=0), jnp.concatenate([kh[c], bh[c]], axis=0),
                 prec['amat'], _NT) for c in n]
    strict = strict_ref[...] != 0.0
    incl = incl_ref[...] != 0.0
    a_kk = [jnp.where(strict, amat[c][0:hc, 0:hc], 0.0) for c in n]
    a_kb = [jnp.where(strict, amat[c][0:hc, hc:2 * hc], 0.0) for c in n]
    a_rk = [jnp.where(incl, amat[c][hc:2 * hc, 0:hc], 0.0) for c in n]
    a_rb = [jnp.where(incl, amat[c][hc:2 * hc, hc:2 * hc], 0.0) for c in n]
    av = [_dot(jnp.concatenate([a_kk[c], a_rk[c]], axis=0), vv[c], prec['apply']) for c in n]
    ri = lax.broadcasted_iota(jnp.int32, (hc, hc), 0)
    cj = lax.broadcasted_iota(jnp.int32, (hc, hc), 1)
    eye = jnp.where(ri == cj, 1.0, 0.0)
    lvl0 = lvl_ref[0] != 0.0
    t_inv = [eye - jnp.where(lvl0, a_kb[c], 0.0) for c in n]
    for lv in range(1, lvl_ref.shape[0]):
        lvl = lvl_ref[lv] != 0.0
        half = [_dot(t_inv[c], jnp.where(lvl, a_kb[c], 0.0), prec['inv']) for c in n]
        t_inv = [t_inv[c] - _dot(half[c], t_inv[c], prec['inv']) for c in n]
    gu = [_dot(t_inv[c], jnp.concatenate([kkt[c], av[c][0:hc]], axis=1), prec['apply']) for c in n]
    pu = [_dot(a_rb[c], gu[c], prec['apply']) for c in n]
    mc = [_dot(gu[c][:, 0:nk], bw[c], prec['update'], _TN) for c in n]
    nn = [_dot(jnp.concatenate([vv[c], gu[c][:, nk:2 * nk]], axis=0), jnp.concatenate([kw[c], -bw[c]], axis=0),
               prec['update'], _TN) for c in n]
    return [(rt[c] - pu[c][:, 0:nk], av[c][hc:2 * hc] - pu[c][:, nk:2 * nk], mc[c], nn[c]) for c in n]


def _rwkv_chunks_direct(chunks, states, wcs, strict_ref, incl_ref, lvl_ref, chunk, prec):
    hc = RW_HEADS * chunk
    n = range(len(chunks))
    kkt, rt, kh, bh, vv, kw, bw = (list(x) for x in zip(*chunks))
    lhs = [jnp.concatenate([kkt[c], rt[c]], axis=0) for c in n]
    amat = [_dot(lhs[c], jnp.concatenate([kh[c], bh[c]], axis=0), prec['amat'], _NT) for c in n]
    ls = [_dot(lhs[c], states[c], prec['state'], _NT) for c in n]
    strict = strict_ref[...] != 0.0
    incl = incl_ref[...] != 0.0
    a_kk = [jnp.where(strict, amat[c][0:hc, 0:hc], 0.0) for c in n]
    a_kb = [jnp.where(strict, amat[c][0:hc, hc:2 * hc], 0.0) for c in n]
    a_rk = [jnp.where(incl, amat[c][hc:2 * hc, 0:hc], 0.0) for c in n]
    a_rb = [jnp.where(incl, amat[c][hc:2 * hc, hc:2 * hc], 0.0) for c in n]
    av = [_dot(jnp.concatenate([a_kk[c], a_rk[c]], axis=0), vv[c], prec['apply']) for c in n]
    ri = lax.broadcasted_iota(jnp.int32, (hc, hc), 0)
    cj = lax.broadcasted_iota(jnp.int32, (hc, hc), 1)
    eye = jnp.where(ri == cj, 1.0, 0.0)
    lvl0 = lvl_ref[0] != 0.0
    t_inv = [eye - jnp.where(lvl0, a_kb[c], 0.0) for c in n]
    for lv in range(1, lvl_ref.shape[0]):
        lvl = lvl_ref[lv] != 0.0
        half = [_dot(t_inv[c], jnp.where(lvl, a_kb[c], 0.0), prec['inv']) for c in n]
        t_inv = [t_inv[c] - _dot(half[c], t_inv[c], prec['inv']) for c in n]
    u = [_dot(t_inv[c], ls[c][0:hc] + av[c][0:hc], prec['apply']) for c in n]
    o = [ls[c][hc:2 * hc] + av[c][hc:2 * hc] - _dot(a_rb[c], u[c], prec['apply']) for c in n]
    s_new = [states[c] * wcs[c] + _dot(jnp.concatenate([vv[c], u[c]], axis=0),
                                       jnp.concatenate([kw[c], -bw[c]], axis=0), prec['update'], _TN) for c in n]
    out = []
    for c in n:
        o_c = o[c][0:chunk]
        for h in range(1, RW_HEADS):
            o_c = o_c + o[c][h * chunk:(h + 1) * chunk]
        out.append((o_c, s_new[c]))
    return out


def _rwkv_chunk_apply(s, local, wc, chunk, prec):
    p, o0, mc, nn = local
    o = _dot(p, s, prec['state'], _NT) + o0
    o_c = o[0:chunk]
    for h in range(1, RW_HEADS):
        o_c = o_c + o[h * chunk:(h + 1) * chunk]
    return o_c, s * wc - _dot(s, mc, prec['state']) + nn


def _rwkv_post(o, pre, rk_ref, gng_ref, gnb_ref, hsum, prec):
    inv_n = 1.0 / RW_HEAD
    o_mu = _dot(o, hsum, prec['headsum'], exact='b') * inv_n
    od = o - o_mu
    o_var = _dot(od * od, hsum, prec['headsum'], exact='b') * inv_n
    on = od * lax.rsqrt(o_var + RW_GN_EPS) * gng_ref[...] + gnb_ref[...]
    bonus = _dot(pre['r'] * pre['k2'] * rk_ref[...], hsum, prec['headsum'], exact='b') * pre['v']
    return (on + bonus) * pre['g']


_RWKV_LOCAL_KEYS = ('kkt', 'rt', 'kh', 'bh', 'v', 'kw', 'bw')


def _rwkv_kernel(z_ref, prev0_ref, wkv0_ref, mu_ref, w0_ref, w2_ref, a0_ref, a2_ref, g2_ref, kkw_ref, kaw_ref,
                 rk_ref, gng_ref, gnb_ref, hsum_ref, tri_ref, strict_ref, incl_ref, lvl_ref, *rest,
                 t_tile, chunk, t_valid, n_tiles, carry, prec):
    y_ref, wkv1_ref, s_scr, prev_scr = rest[-4:]
    i = pl.program_id(1)
    z = z_ref[...]
    row = lax.broadcasted_iota(jnp.int32, (t_tile, 1), 0)
    if carry:
        assert t_valid == t_tile * n_tiles

        @pl.when(i == 0)
        def _():
            s_scr[...] = _state_bd(wkv0_ref)
            prev_scr[...] = prev0_ref[...]

        z_prev = jnp.where(row == 0, prev_scr[...], pltpu.roll(z, 1, 0))
        prev_scr[...] = z[t_tile - 1:t_tile, :]
        valid = None
    else:
        step = row % chunk
        z_prev = jnp.where(step == 0, prev0_ref[...], pltpu.roll(z, 1, 0))
        valid = step < t_valid
    hsum = hsum_ref[...]
    pre = _rwkv_prep(z, z_prev, valid, mu_ref, w0_ref, w2_ref, a0_ref, a2_ref, g2_ref, kkw_ref, kaw_ref, hsum,
                     tri_ref, chunk, prec)
    lane_head = lax.broadcasted_iota(jnp.int32, (chunk, C_BR), 1) // RW_HEAD
    n_chunks = t_tile // chunk
    chunks = [tuple(_heads_bd(pre[key][c * chunk:(c + 1) * chunk], lane_head) for key in _RWKV_LOCAL_KEYS)
              for c in range(n_chunks)]
    wcs = [pre['wc'][c * chunk:c * chunk + 1] for c in range(n_chunks)]
    o_rows = []
    if carry:
        local = _rwkv_chunks_local(chunks, strict_ref, incl_ref, lvl_ref, chunk, prec)
        s = s_scr[...]
        for c in range(n_chunks):
            o_c, s = _rwkv_chunk_apply(s, local[c], wcs[c], chunk, prec)
            o_rows.append(o_c)
    else:
        states = [_state_bd(wkv0_ref.at[c]) for c in range(n_chunks)]
        for c, (o_c, s_c) in enumerate(_rwkv_chunks_direct(chunks, states, wcs, strict_ref, incl_ref, lvl_ref,
                                                           chunk, prec)):
            o_rows.append(o_c)
            for h in range(RW_HEADS):
                wkv1_ref[c, h] = s_c[h * RW_HEAD:(h + 1) * RW_HEAD, h * RW_HEAD:(h + 1) * RW_HEAD]
    y_ref[...] = _rwkv_post(jnp.concatenate(o_rows, axis=0), pre, rk_ref, gng_ref, gnb_ref, hsum, prec)

    if carry:
        s_scr[...] = s

        @pl.when(i == n_tiles - 1)
        def _():
            for h in range(RW_HEADS):
                wkv1_ref[h] = s[h * RW_HEAD:(h + 1) * RW_HEAD, h * RW_HEAD:(h + 1) * RW_HEAD]


def _rwkv(z3d, col_blk, prev0, wkv0, pw, layer, *, t_tile, chunk, t_valid, carry, prec, wkv_out=None):
    bsz, t_len, _ = z3d.shape
    n_tiles = t_len // t_tile
    cs = min(t_tile, max(chunk, 64))
    assert t_tile % cs == 0 and cs % chunk == 0
    idx = np.arange(cs)
    tri = jnp.asarray(((idx[:, None] // chunk == idx[None, :] // chunk)
                       & (idx[None, :] <= idx[:, None])).astype(np.float32))
    hid = np.arange(C_BR) // RW_HEAD
    hsum = jnp.asarray((hid[:, None] == hid[None, :]).astype(np.float32))
    hc = RW_HEADS * chunk
    hh, tt = np.arange(hc) // chunk, np.arange(hc) % chunk
    same_head = hh[:, None] == hh[None, :]
    strict = jnp.asarray((same_head & (tt[None, :] < tt[:, None])).astype(np.float32))
    incl = jnp.asarray((same_head & (tt[None, :] <= tt[:, None])).astype(np.float32))
    lvls = []
    m = 1
    while m < chunk:
        lvls.append(same_head & (tt[:, None] // (2 * m) == tt[None, :] // (2 * m))
                    & (tt[:, None] % (2 * m) >= m) & (tt[None, :] % (2 * m) < m))
        m *= 2
    lvl = jnp.asarray(np.stack(lvls).astype(np.float32))
    lb = lambda shape: _layer_block(shape, layer)
    vec = lambda n: lb((1, n))
    if carry:
        prev_spec = pl.BlockSpec((None, 1, RW_IN), lambda b, i: (b, 0, 0))
        wkv_spec = pl.BlockSpec((None, RW_HEADS, RW_HEAD, RW_HEAD), lambda b, i: (b, 0, 0, 0))
        extra_in, extra_specs = [], []
    else:
        assert n_tiles == 1
        prev_spec = pl.BlockSpec((None, t_tile, RW_IN), lambda b, i: (b, 0, 0))
        wkv_spec = pl.BlockSpec((None, t_tile // chunk, None, RW_HEADS, RW_HEAD, RW_HEAD),
                                lambda b, i: (b, 0, layer, 0, 0, 0))
        extra_in, extra_specs = [wkv_out], [pl.BlockSpec(memory_space=pl.ANY)]
    operands = [z3d, prev0, wkv0, pw['mu'], pw['w0'], pw['w2'], pw['a0'], pw['a2'], pw['g2'], pw['kk'], pw['ka'],
                pw['rk'], pw['gn_g'], pw['gn_b'], hsum, tri, strict, incl, lvl] + extra_in
    aliases = {} if carry else {len(operands) - 1: 1}
    kern = functools.partial(_rwkv_kernel, t_tile=t_tile, chunk=chunk, t_valid=t_valid, n_tiles=n_tiles, carry=carry,
                             prec=prec)
    return pl.pallas_call(
        kern,
        grid=(bsz, n_tiles),
        in_specs=[pl.BlockSpec((None, t_tile, RW_IN), lambda b, i: (b, i, col_blk)), prev_spec, wkv_spec,
                  vec(RW_IN), vec(C_BR), lb((RW_LW + RW_LA, C_BR)), vec(C_BR), lb((RW_LW + RW_LA, C_BR)),
                  lb((RW_LG, C_BR)), vec(C_BR), vec(C_BR), vec(C_BR), vec(C_BR), vec(C_BR),
                  _full((C_BR, C_BR)), _full((cs, cs)), _full((hc, hc)), _full((hc, hc)), _full(lvl.shape)]
        + extra_specs,
        out_specs=[pl.BlockSpec((None, t_tile, C_BR), lambda b, i: (b, i, 0)), wkv_spec],
        out_shape=[jax.ShapeDtypeStruct((bsz, t_len, C_BR), F32), jax.ShapeDtypeStruct(wkv0.shape, F32)],
        input_output_aliases=aliases,
        scratch_shapes=[pltpu.VMEM((C_BR, C_BR), F32), pltpu.VMEM((1, RW_IN), F32)],
        compiler_params=_params(("parallel", "arbitrary")),
        name="rwkv7",
    )(*operands)


def _s5_kernel(u_ref, h0_ref, lbr_ref, lbi_ref, bb_ref, cc_ref, d_ref, gw_ref, gb_ref,
               y_ref, h1_ref, bu_scr, h_scr, *relayout_scr, bsz, t_tile, n_tiles, time_major):
    i = pl.program_id(0)

    @pl.when(i == 0)
    def _():
        h_scr[...] = h0_ref[...]

    rows = bsz * t_tile
    n_lt = S5_N // LANES
    n_ut = C_BR // LANES
    lane_tile = lambda ref, j: ref[:, j * LANES:(j + 1) * LANES]
    seq_rows = lambda t: pl.ds(t, bsz, stride=t_tile)
    step_rows = lambda t: pl.ds(pl.multiple_of(t * bsz, bsz), bsz)
    u = u_ref[...].reshape(rows, C_BR)
    if time_major:
        bt_scr, tm_scr = relayout_scr
        for j in range(n_ut):
            bt_scr[j] = lane_tile(u, j)

        def to_time_major(t, _):
            for j in range(n_ut):
                tm_scr[j, step_rows(t), :] = bt_scr[j, seq_rows(t), :]
            return 0

        lax.fori_loop(0, t_tile, to_time_major, 0, unroll=4)
        u = jnp.concatenate([tm_scr[j] for j in range(n_ut)], axis=1)
        sl_of = step_rows
    else:
        sl_of = seq_rows
    bu = _mm(u, bb_ref[...])
    for j in range(2 * n_lt):
        bu_scr[j] = lane_tile(bu, j)
    lbr = [jnp.broadcast_to(lane_tile(lbr_ref, j), (bsz, LANES)) for j in range(n_lt)]
    lbi = [jnp.broadcast_to(lane_tile(lbi_ref, j), (bsz, LANES)) for j in range(n_lt)]

    def step(t, carry):
        hr, hi = carry
        sl = sl_of(t)
        new_r, new_i = [], []
        for j in range(n_lt):
            nr = lbr[j] * hr[j] - lbi[j] * hi[j] + bu_scr[j, sl, :]
            ni = lbr[j] * hi[j] + lbi[j] * hr[j] + bu_scr[n_lt + j, sl, :]
            bu_scr[j, sl, :] = nr
            bu_scr[n_lt + j, sl, :] = ni
            new_r.append(nr)
            new_i.append(ni)
        return tuple(new_r), tuple(new_i)

    h_init = (tuple(lane_tile(h_scr, j) for j in range(n_lt)),
              tuple(lane_tile(h_scr, n_lt + j) for j in range(n_lt)))
    hr, hi = lax.fori_loop(0, t_tile, step, h_init)
    for j in range(n_lt):
        h_scr[:, j * LANES:(j + 1) * LANES] = hr[j]
        h_scr[:, (n_lt + j) * LANES:(n_lt + j + 1) * LANES] = hi[j]

    hs = jnp.concatenate([bu_scr[j] for j in range(2 * n_lt)], axis=1)
    y = _mm(hs, cc_ref[...]) + d_ref[...] * u
    y = _gelu_tanh(y)
    y = y * _sigmoid(_mm(y, gw_ref[...]) + gb_ref[...])
    if time_major:
        for j in range(n_ut):
            tm_scr[j] = lane_tile(y, j)

        def to_seq_major(t, _):
            for j in range(n_ut):
                bt_scr[j, seq_rows(t), :] = tm_scr[j, step_rows(t), :]
            return 0

        lax.fori_loop(0, t_tile, to_seq_major, 0, unroll=4)
        y = jnp.concatenate([bt_scr[j] for j in range(n_ut)], axis=1)
    y_ref[...] = y.reshape(y_ref.shape)

    @pl.when(i == n_tiles - 1)
    def _():
        h1_ref[...] = h_scr[...]


def _s5(z, col_blk, h0, ps, layer, *, bsz, t_len, t_tile):
    lb = lambda shape: _layer_block(shape, layer)
    n_tiles = t_len // t_tile
    rows = bsz * t_tile
    if z.ndim == 3:
        u_spec = pl.BlockSpec((bsz, t_tile, C_BR), lambda i: (0, i, col_blk))
        y_spec = pl.BlockSpec((bsz, t_tile, C_BR), lambda i: (0, i, 0))
        y_shape = (bsz, t_len, C_BR)
    else:
        assert n_tiles == 1
        u_spec = pl.BlockSpec((rows, C_BR), lambda i: (0, col_blk))
        y_spec = pl.BlockSpec((rows, C_BR), lambda i: (0, 0))
        y_shape = (rows, C_BR)
    time_major = z.ndim == 3 and bsz == 8
    relayout_scr = [pltpu.VMEM((C_BR // LANES, rows, LANES), F32)] * 2 if time_major else []
    kern = functools.partial(_s5_kernel, bsz=bsz, t_tile=t_tile, n_tiles=n_tiles, time_major=time_major)
    return pl.pallas_call(
        kern,
        grid=(n_tiles,),
        in_specs=[u_spec, _full((bsz, 2 * S5_N)), lb((1, S5_N)), lb((1, S5_N)),
                  lb((C_BR, 2 * S5_N)), lb((2 * S5_N, C_BR)), lb((1, C_BR)), lb((C_BR, C_BR)), lb((1, C_BR))],
        out_specs=[y_spec, _full((bsz, 2 * S5_N))],
        out_shape=[jax.ShapeDtypeStruct(y_shape, F32), jax.ShapeDtypeStruct((bsz, 2 * S5_N), F32)],
        scratch_shapes=[pltpu.VMEM((2 * S5_N // LANES, rows, LANES), F32), pltpu.VMEM((bsz, 2 * S5_N), F32)]
        + relayout_scr,
        compiler_params=_params(("arbitrary",)),
        name="s5",
    )(z, h0, ps['lb_re'], ps['lb_im'], ps['bb'], ps['cc'], ps['d'], ps['glu_w'], ps['glu_b'])


def _conv_taps(full_scr, w_ref, t_tile):
    lo = CONV_HIST_PAD - CONV_HIST
    sub = 8
    assert t_tile % sub == 0
    acc = None
    for rho in range(sub):
        offs = [o for o in range(rho, lo + CONV_W, sub) if o >= lo]
        rows = t_tile + (sub if rho else 0)
        part = None
        for o in offs:
            term = full_scr[o - rho:o - rho + rows, :] * w_ref[o - lo:o - lo + 1, :]
            part = term if part is None else part + term
        part = part[rho:rho + t_tile]
        acc = part if acc is None else acc + part
    return acc


def _conv_kernel(z_ref, c0_ref, w_ref, b_ref, g_ref, be_ref, _state_buf, y_ref, c1_ref, full_scr, *, t_tile, n_tiles):
    i = pl.program_id(1)
    lo = CONV_HIST_PAD - CONV_HIST

    @pl.when(i == 0)
    def _():
        full_scr[0:lo, :] = jnp.zeros((lo, C_BR), F32)
        full_scr[lo:CONV_HIST_PAD, :] = c0_ref[...]

    z = z_ref[...]
    full_scr[CONV_HIST_PAD:CONV_HIST_PAD + t_tile, :] = z[:, 0:C_BR] * _sigmoid(z[:, C_BR:2 * C_BR])
    y = _layer_norm(_conv_taps(full_scr, w_ref, t_tile) + b_ref[...], g_ref[...], be_ref[...])
    y_ref[...] = y * _sigmoid(y)
    hist = full_scr[t_tile:t_tile + CONV_HIST_PAD, :]
    full_scr[0:CONV_HIST_PAD, :] = hist

    @pl.when(i == n_tiles - 1)
    def _():
        c1_ref[...] = hist[lo:, :]


def _conv(z3d, col_blk, conv0, conv_out, layer, pc, *, t_tile):
    bsz, t_len, _ = z3d.shape
    n_tiles = t_len // t_tile
    state_spec = pl.BlockSpec((None, None, CONV_HIST, C_BR), lambda b, i: (b, layer, 0, 0))
    kern = functools.partial(_conv_kernel, t_tile=t_tile, n_tiles=n_tiles)
    return pl.pallas_call(
        kern,
        grid=(bsz, n_tiles),
        in_specs=[pl.BlockSpec((None, t_tile, 2 * C_BR), lambda b, i: (b, i, col_blk)), state_spec,
                  _layer_block((CONV_W, C_BR), layer), _layer_block((1, C_BR), layer),
                  _layer_block((1, C_BR), layer), _layer_block((1, C_BR), layer),
                  pl.BlockSpec(memory_space=pl.ANY)],
        out_specs=[pl.BlockSpec((None, t_tile, C_BR), lambda b, i: (b, i, 0)), state_spec],
        out_shape=[jax.ShapeDtypeStruct((bsz, t_len, C_BR), F32), jax.ShapeDtypeStruct(conv_out.shape, F32)],
        input_output_aliases={6: 1},
        scratch_shapes=[pltpu.VMEM((CONV_HIST_PAD + t_tile, C_BR), F32)],
        compiler_params=_params(("parallel", "arbitrary")),
        name="conv",
    )(z3d, conv0, pc['w'], pc['b'], pc['ln_g'], pc['ln_b'], conv_out)


def _conv_short_kernel(z_ref, c0_ref, w_ref, b_ref, g_ref, be_ref, _state_buf, y_ref, c1_ref, in_scr, out_scr,
                       *, bsz, t_len):
    n_lt = C_BR // LANES
    z = z_ref[...]
    c = z[:, 0:C_BR] * _sigmoid(z[:, C_BR:2 * C_BR])
    for j in range(n_lt):
        in_scr[j] = c[:, j * LANES:(j + 1) * LANES]
    step_rows = lambda t: pl.ds(t, bsz, stride=t_len)
    hist = lambda r: c0_ref[:, r * C_BR:(r + 1) * C_BR]
    new = [jnp.concatenate([in_scr[j, step_rows(t), :] for j in range(n_lt)], axis=1) for t in range(t_len)]
    full = lambda r: hist(r) if r < CONV_HIST else new[r - CONV_HIST]
    for t in range(t_len):
        acc = b_ref[...] + full(t) * w_ref[0:1, :]
        for j in range(1, CONV_W):
            acc = acc + full(t + j) * w_ref[j:j + 1, :]
        y = _layer_norm(acc, g_ref[...], be_ref[...])
        y = y * _sigmoid(y)
        for j in range(n_lt):
            out_scr[j, step_rows(t), :] = y[:, j * LANES:(j + 1) * LANES]
    y_ref[...] = jnp.concatenate([out_scr[j] for j in range(n_lt)], axis=1)
    for r in range(CONV_HIST):
        c1_ref[:, r * C_BR:(r + 1) * C_BR] = full(r + t_len)


def _conv_short(z2d, col_blk, conv0, conv_out, layer, pc, *, bsz, t_len):
    rows = bsz * t_len
    width = CONV_HIST * C_BR
    state_spec = pl.BlockSpec((bsz, width), lambda i: (0, layer))
    kern = functools.partial(_conv_short_kernel, bsz=bsz, t_len=t_len)
    return pl.pallas_call(
        kern,
        grid=(1,),
        in_specs=[pl.BlockSpec((rows, 2 * C_BR), lambda i: (0, col_blk)), state_spec,
                  _layer_block((CONV_W, C_BR), layer), _layer_block((1, C_BR), layer),
                  _layer_block((1, C_BR), layer), _layer_block((1, C_BR), layer),
                  pl.BlockSpec(memory_space=pl.ANY)],
        out_specs=[pl.BlockSpec((rows, C_BR), lambda i: (0, 0)), state_spec],
        out_shape=[jax.ShapeDtypeStruct((rows, C_BR), F32), jax.ShapeDtypeStruct(conv_out.shape, F32)],
        input_output_aliases={6: 1},
        scratch_shapes=[pltpu.VMEM((C_BR // LANES, rows, LANES), F32)] * 2,
        compiler_params=_params(("arbitrary",)),
        name="conv_short",
    )(z2d, conv0, pc['w'], pc['b'], pc['ln_g'], pc['ln_b'], conv_out)


def _gmlp_kernel(z_ref, g_ref, b_ref, wm_ref, bias_ref, y_ref, *v_ref):
    z = z_ref[...]
    u = z[:, 0:C_BR]
    v = _layer_norm(z[:, C_BR:2 * C_BR], g_ref[...], b_ref[...])
    if v_ref:
        v_ref[0][...] = v
    vb = v.astype(MM_DTYPE)
    head = lax.broadcasted_iota(jnp.int32, (CHUNK, C_BR), 1) // GM_HEAD
    for c in range(GM_TILE // CHUNK):
        rows = slice(c * CHUNK, (c + 1) * CHUNK)
        s = bias_ref[...]
        for h in range(GM_HEADS):
            s = s + jnp.where(head == h, jnp.dot(wm_ref[h], vb[rows], preferred_element_type=F32), 0.0)
        y_ref[rows, :] = u[rows] * s


def _gmlp(z3d, col_blk, pg, wm, bias, layer, *, emit_v):
    lb = lambda shape: _layer_block(shape, layer)
    bsz, t_len, _ = z3d.shape
    out_spec = pl.BlockSpec((None, GM_TILE, C_BR), lambda b, i: (b, i, 0))
    n_out = 2 if emit_v else 1
    return pl.pallas_call(
        _gmlp_kernel,
        grid=(bsz, t_len // GM_TILE),
        in_specs=[pl.BlockSpec((None, GM_TILE, 2 * C_BR), lambda b, i: (b, i, col_blk)),
                  lb((1, C_BR)), lb((1, C_BR)), lb((GM_HEADS, CHUNK, CHUNK)), lb((CHUNK, C_BR))],
        out_specs=[out_spec] * n_out,
        out_shape=[jax.ShapeDtypeStruct((bsz, t_len, C_BR), F32)] * n_out,
        compiler_params=_params(("parallel", "parallel")),
        name="gmlp",
    )(z3d, pg['ln_g'], pg['ln_b'], wm, bias)


def _merge_kernel(x_ref, yrw_ref, ys5_ref, ycv_ref, ygm_ref, wg_ref, wb_ref, wo_ref, g_ref, b_ref, o_ref):
    tm = x_ref.shape[0]
    halves = [slice(0, tm // 2), slice(tm // 2, tm)]
    x = [x_ref[h, :] for h in halves]
    xb = [v.astype(MM_DTYPE) for v in x]
    merged = [None for _ in halves]
    for bidx, y_ref in enumerate((yrw_ref, ys5_ref, ycv_ref, ygm_ref)):
        wg = wg_ref[:, bidx * D_MODEL:(bidx + 1) * D_MODEL]
        gate = [_sigmoid(jnp.dot(v, wg, preferred_element_type=F32)) for v in xb]
        term = [gate[i] * _mm(y_ref[h, :], wb_ref[bidx]) for i, h in enumerate(halves)]
        merged = [t if m is None else m + t for m, t in zip(merged, term)]
    proj = [_mm(m, wo_ref[...]) for m in merged]
    for i, h in enumerate(halves):
        o_ref[h, :] = _layer_norm(DN_ALPHA * x[i] + proj[i], g_ref[...], b_ref[...])


def _merge(x2d, ys, big, pm, layer):
    n = x2d.shape[0]
    tm = 512
    row = lambda w: pl.BlockSpec((tm, w), lambda i: (i, 0))
    return pl.pallas_call(
        _merge_kernel,
        grid=(n // tm,),
        in_specs=[row(D_MODEL), row(C_BR), row(C_BR), row(C_BR), row(C_BR),
                  _layer_block((D_MODEL, N_BRANCH * D_MODEL), layer), _layer_block((N_BRANCH, C_BR, D_MODEL), layer),
                  _layer_block((D_MODEL, D_MODEL), layer), _layer_block((1, D_MODEL), layer),
                  _layer_block((1, D_MODEL), layer)],
        out_specs=row(D_MODEL),
        out_shape=jax.ShapeDtypeStruct((n, D_MODEL), F32),
        compiler_params=_params(("parallel",)),
        name="merge",
    )(x2d, *ys, big['w_gate'], big['w_branch'], big['w_out'], pm['ln1_g'], pm['ln1_b'])


def _moe_kernel(x_ref, wg_ref, bg_ref, wu_ref, wd_ref, g_ref, b_ref, o_ref, hh_scr, *, tm):
    lane = lax.broadcasted_iota(jnp.int32, (tm, LANES), 1)
    x = x_ref[...]
    xb = x.astype(MM_DTYPE)
    logits = _dot(x, wg_ref[...], PREC['route']) + bg_ref[...]
    gl = jnp.where(lane < N_GROUPS, logits, NEG_BIG)
    gmax = jnp.max(gl, axis=-1, keepdims=True)
    g_sel = jnp.min(jnp.where(gl == gmax, lane, LANES), axis=-1, keepdims=True)
    p_group = 1.0 / jnp.sum(jnp.where(lane < N_GROUPS, jnp.exp(gl - gmax), 0.0), axis=-1, keepdims=True)
    first = N_GROUPS + g_sel * E_PER_GROUP
    el = jnp.where((lane >= first) & (lane < first + E_PER_GROUP), logits, NEG_BIG)
    m1 = jnp.max(el, axis=-1, keepdims=True)
    i1 = jnp.min(jnp.where(el == m1, lane, LANES), axis=-1, keepdims=True)
    el2 = jnp.where(lane == i1, NEG_BIG, el)
    m2 = jnp.max(el2, axis=-1, keepdims=True)
    i2 = jnp.min(jnp.where(el2 == m2, lane, LANES), axis=-1, keepdims=True)
    e2 = jnp.exp(m2 - m1)
    w1 = p_group / (1.0 + e2)
    w2 = p_group * e2 / (1.0 + e2)
    for e in range(N_EXPERTS):
        comb_e = jnp.where(i1 == e + N_GROUPS, w1, 0.0) + jnp.where(i2 == e + N_GROUPS, w2, 0.0)
        h = jnp.dot(xb, wu_ref[e], preferred_element_type=F32)
        h1 = h[:, 0:D_EXPERT]
        hh = h1 * _sigmoid(h1) * h[:, D_EXPERT:2 * D_EXPERT] * comb_e
        hh_scr[:, e * D_EXPERT:(e + 1) * D_EXPERT] = hh.astype(MM_DTYPE)
    moe = jnp.dot(hh_scr[...], wd_ref[...], preferred_element_type=F32)
    o_ref[...] = _layer_norm(DN_ALPHA * x + moe, g_ref[...], b_ref[...])


def _moe(x2d, big, pe, layer):
    n = x2d.shape[0]
    tm = 512
    kern = functools.partial(_moe_kernel, tm=tm)
    resident = lambda shape: _layer_block(shape, layer, pipeline_mode=pl.Buffered(1))
    return pl.pallas_call(
        kern,
        grid=(n // tm,),
        in_specs=[pl.BlockSpec((tm, D_MODEL), lambda i: (i, 0)),
                  _layer_block((D_MODEL, LANES), layer), _layer_block((1, LANES), layer),
                  resident((N_EXPERTS, D_MODEL, 2 * D_EXPERT)), resident((N_EXPERTS * D_EXPERT, D_MODEL)),
                  _layer_block((1, D_MODEL), layer), _layer_block((1, D_MODEL), layer)],
        out_specs=pl.BlockSpec((tm, D_MODEL), lambda i: (i, 0)),
        out_shape=jax.ShapeDtypeStruct((n, D_MODEL), F32),
        scratch_shapes=[pltpu.VMEM((tm, N_EXPERTS * D_EXPERT), MM_DTYPE)],
        compiler_params=_params(("parallel",)),
        name="moe",
    )(x2d, pe['wg'], pe['bg'], big['w_up'], big['w_down'], pe['ln2_g'], pe['ln2_b'])


def _block_diag(blocks):
    nl, g, m, n = blocks.shape
    eye = jnp.eye(g, dtype=blocks.dtype)
    return (eye[None, :, None, :, None] * blocks[:, :, :, None, :]).reshape(nl, g * m, g * n)


def _prep_params(p, t_short):
    nl = p['w_in'].shape[0]
    row = lambda a: a.reshape(nl, 1, -1).astype(F32)
    w_in = p['w_in']
    big = dict(w_mix=jnp.concatenate([w_in[..., :OFF_S5], w_in[..., OFF_CV:OFF_GM], w_in[..., OFF_GM:OFF_GATE],
                                      w_in[..., OFF_S5:OFF_CV]], axis=-1).astype(MM_DTYPE),
               w_gate=w_in[..., OFF_GATE:].astype(MM_DTYPE),
               w_branch=p['w_branch'].astype(MM_DTYPE), w_out=p['w_out'].astype(MM_DTYPE),
               w_up=p['moe_w_up'].astype(MM_DTYPE),
               w_down=p['moe_w_down'].astype(MM_DTYPE).reshape(nl, N_EXPERTS * D_EXPERT, D_MODEL))
    zeros_lora = jnp.zeros((nl, RW_LW, C_BR), F32)
    rw = dict(mu=row(p['rw_mu']), w0=row(p['rw_w0']),
              w2=jnp.concatenate([p['rw_w2'], zeros_lora], axis=1).astype(MM_DTYPE), a0=row(p['rw_a0']),
              a2=jnp.concatenate([zeros_lora, p['rw_a2']], axis=1).astype(MM_DTYPE),
              g2=p['rw_g2'].astype(MM_DTYPE), kk=row(p['rw_kk']), ka=row(p['rw_ka']), rk=row(p['rw_rk']),
              gn_g=row(p['rw_gn_g']), gn_b=row(p['rw_gn_b']))
    lr, li = p['s5_lam_re'].astype(F32), p['s5_lam_im'].astype(F32)
    dt = jnp.exp(p['s5_log_dt'].astype(F32))[..., None]
    mag = jnp.exp(lr * dt)
    lb_re, lb_im = mag * jnp.cos(li * dt), mag * jnp.sin(li * dt)
    den = lr * lr + li * li
    q_re = ((lb_re - 1.0) * lr + lb_im * li) / den
    q_im = (lb_im * lr - (lb_re - 1.0) * li) / den
    br, bi = p['s5_b_re'].astype(F32), p['s5_b_im'].astype(F32)
    bb_re = q_re[..., None] * br - q_im[..., None] * bi
    bb_im = q_re[..., None] * bi + q_im[..., None] * br
    t23 = lambda a: jnp.swapaxes(a, 2, 3)
    bb = jnp.concatenate([_block_diag(t23(bb_re)), _block_diag(t23(bb_im))], axis=2).astype(MM_DTYPE)
    cc = jnp.concatenate([_block_diag(t23(p['s5_c_re'].astype(F32))), -_block_diag(t23(p['s5_c_im'].astype(F32)))],
                         axis=1).astype(MM_DTYPE)
    s5 = dict(lb_re=row(lb_re), lb_im=row(lb_im), bb=bb, cc=cc, d=row(p['s5_d']),
              glu_w=p['s5_glu_w'].astype(MM_DTYPE), glu_b=row(p['s5_glu_b']))
    cv = dict(w=p['cv_w'].astype(F32), b=row(p['cv_b']), ln_g=row(p['cv_ln_g']), ln_b=row(p['cv_ln_b']))
    causal = jnp.tril(jnp.ones((CHUNK, CHUNK), dtype=bool))
    wm = jnp.where(causal, p['gm_ws'], 0).astype(F32)
    bias = jnp.repeat(jnp.swapaxes(p['gm_bs'], 1, 2), GM_HEAD, axis=2).astype(F32)
    reps = CHUNK // t_short
    wm_short = jnp.einsum('rs,lhij->lhrisj', jnp.eye(reps, dtype=F32),
                          wm[:, :, :t_short, :t_short]).reshape(nl, GM_HEADS, CHUNK, CHUNK)
    gm = dict(ln_g=row(p['gm_ln_g']), ln_b=row(p['gm_ln_b']), wm=wm.astype(MM_DTYPE), bias=bias,
              wm_short=wm_short.astype(MM_DTYPE), bias_short=jnp.tile(bias[:, :t_short], (1, reps, 1)))
    mg = dict(ln1_g=row(p['ln1_g']), ln1_b=row(p['ln1_b']))
    pad = LANES - N_GROUPS - N_EXPERTS
    wg = jnp.concatenate([p['moe_wg1'], p['moe_wg2'], jnp.zeros((nl, D_MODEL, pad), F32)], axis=2).astype(F32)
    bg = row(jnp.concatenate([p['moe_bg1'], p['moe_bg2'], jnp.zeros((nl, pad), F32)], axis=1))
    moe = dict(wg=wg, bg=bg, ln2_g=row(p['ln2_g']), ln2_b=row(p['ln2_b']))
    return dict(big=big, rw=rw, s5=s5, cv=cv, gm=gm, mg=mg, moe=moe)


def _gmlp_group(z3d, pg, layer, *, is_prompt):
    bsz, t_len, n_cols = z3d.shape
    if is_prompt:
        y, = _gmlp(z3d, P_GM // (2 * C_BR), pg, pg['wm'], pg['bias'], layer, emit_v=False)
        return y, None
    y, v = _gmlp(z3d.reshape(1, bsz * t_len, n_cols), P_GM // (2 * C_BR), pg, pg['wm_short'], pg['bias_short'],
                 layer, emit_v=True)
    return y.reshape(bsz, t_len, C_BR), v.reshape(bsz, t_len, C_BR)


def _run_group(x, wkv0, shift0, s5r0, s5i0, conv0, pr, *, is_prompt):
    bsz, t_len, _ = x.shape
    big = pr['big']
    n = bsz * t_len
    x2d = x.reshape(n, D_MODEL)
    outs = []
    conv_shape = conv0.shape
    if not is_prompt:
        conv0 = conv0.reshape(bsz, DEPTH * CONV_HIST * C_BR)
    conv_buf = jnp.zeros_like(conv0)
    if not is_prompt:
        wkv0_g = wkv0.reshape((bsz // RW_GROUP, RW_GROUP) + wkv0.shape[1:])
        wkv_buf = jnp.zeros_like(wkv0_g)
        grp_rows = lambda a: a.reshape((bsz // RW_GROUP, RW_GROUP * a.shape[1]) + a.shape[2:])
    for l in range(DEPTH):
        z2d = _inproj(x2d, big['w_mix'], l)
        z3d = z2d.reshape(bsz, t_len, N_MIX)
        sh0 = shift0[:, l].reshape(bsz, 1, RW_IN)
        if is_prompt:
            y_rw, wkv1 = _rwkv(z3d, P_RW // RW_IN, sh0, wkv0[:, l], pr['rw'], l, t_tile=512, chunk=64,
                               t_valid=t_len, carry=True, prec=PREC)
        else:
            z_rw = grp_rows(jnp.pad(z3d[:, :, P_RW:P_RW + RW_IN], ((0, 0), (0, RW_PAD - t_len), (0, 0))))
            prev0 = grp_rows(jnp.pad(sh0, ((0, 0), (0, RW_PAD - 1), (0, 0))))
            y_rw, wkv_buf = _rwkv(z_rw, 0, prev0, wkv0_g, pr['rw'], l, t_tile=RW_GROUP * RW_PAD, chunk=RW_PAD,
                                  t_valid=t_len, carry=False, prec=PREC_SHORT, wkv_out=wkv_buf)
            y_rw = y_rw.reshape(bsz, RW_PAD, C_BR)[:, :t_len]
            wkv1 = None
        shift1 = z3d[:, t_len - 1, P_RW:P_RW + RW_IN]
        h0 = jnp.concatenate([s5r0[:, l].reshape(bsz, S5_N), s5i0[:, l].reshape(bsz, S5_N)], axis=1)
        if is_prompt:
            y_s5, h1 = _s5(z3d, P_S5 // C_BR, h0, pr['s5'], l, bsz=bsz, t_len=t_len, t_tile=128)
        else:
            y_s5, h1 = _s5(z2d, P_S5 // C_BR, h0, pr['s5'], l, bsz=bsz, t_len=t_len, t_tile=t_len)
        s5r1 = h1[:, :S5_N].reshape(bsz, S5_GROUPS, S5_STATE)
        s5i1 = h1[:, S5_N:].reshape(bsz, S5_GROUPS, S5_STATE)
        if is_prompt:
            y_cv, conv_buf = _conv(z3d, P_CV // (2 * C_BR), conv0, conv_buf, l, pr['cv'], t_tile=256)
        else:
            y_cv, conv_buf = _conv_short(z2d, P_CV // (2 * C_BR), conv0, conv_buf, l, pr['cv'], bsz=bsz,
                                         t_len=t_len)
        y_gm, v_gm = _gmlp_group(z3d, pr['gm'], l, is_prompt=is_prompt)
        ys = [y.reshape(n, C_BR) for y in (y_rw, y_s5, y_cv, y_gm)]
        x2d = _merge(x2d, ys, big, pr['mg'], l)
        x2d = _moe(x2d, big, pr['moe'], l)
        outs.append((wkv1, shift1, s5r1, s5i1, v_gm))
    stack = lambda i: None if outs[0][i] is None else jnp.stack([o[i] for o in outs], axis=1)
    wkv = stack(0) if is_prompt else wkv_buf.reshape(wkv0.shape)
    return (x2d.reshape(bsz, t_len, D_MODEL),
            (wkv, stack(1), stack(2), stack(3), conv_buf.reshape(conv_shape), stack(4)))


def kernel(x_prompt, x_sample, state_rwkv_wkv, state_rwkv_shift, state_s5_re, state_s5_im, cache_conv,
           w_in, rw_mu, rw_w0, rw_w2, rw_a0, rw_a2, rw_g2, rw_kk, rw_ka, rw_rk, rw_gn_g, rw_gn_b,
           s5_lam_re, s5_lam_im, s5_log_dt, s5_b_re, s5_b_im, s5_c_re, s5_c_im, s5_d, s5_glu_w, s5_glu_b,
           cv_w, cv_b, cv_ln_g, cv_ln_b, gm_ln_g, gm_ln_b, gm_ws, gm_bs,
           w_branch, w_out, ln1_g, ln1_b,
           moe_wg1, moe_bg1, moe_wg2, moe_bg2, moe_w_up, moe_w_down, ln2_g, ln2_b):
    p = dict(w_in=w_in, rw_mu=rw_mu, rw_w0=rw_w0, rw_w2=rw_w2, rw_a0=rw_a0, rw_a2=rw_a2, rw_g2=rw_g2,
             rw_kk=rw_kk, rw_ka=rw_ka, rw_rk=rw_rk, rw_gn_g=rw_gn_g, rw_gn_b=rw_gn_b,
             s5_lam_re=s5_lam_re, s5_lam_im=s5_lam_im, s5_log_dt=s5_log_dt, s5_b_re=s5_b_re, s5_b_im=s5_b_im,
             s5_c_re=s5_c_re, s5_c_im=s5_c_im, s5_d=s5_d, s5_glu_w=s5_glu_w, s5_glu_b=s5_glu_b,
             cv_w=cv_w, cv_b=cv_b, cv_ln_g=cv_ln_g, cv_ln_b=cv_ln_b, gm_ln_g=gm_ln_g, gm_ln_b=gm_ln_b,
             gm_ws=gm_ws, gm_bs=gm_bs, w_branch=w_branch, w_out=w_out, ln1_g=ln1_g, ln1_b=ln1_b,
             moe_wg1=moe_wg1, moe_bg1=moe_bg1, moe_wg2=moe_wg2, moe_bg2=moe_bg2, moe_w_up=moe_w_up,
             moe_w_down=moe_w_down, ln2_g=ln2_g, ln2_b=ln2_b)
    pr = _prep_params(p, x_sample.shape[1])
    bp = x_prompt.shape[0]
    dt = x_prompt.dtype
    y_prompt, (p_wkv, p_shift, p_s5r, p_s5i, p_conv, _) = _run_group(
        x_prompt,
        jnp.zeros((bp, DEPTH, RW_HEADS, RW_HEAD, RW_HEAD), dt),
        jnp.zeros((bp, DEPTH, RW_IN), dt),
        jnp.zeros((bp, DEPTH, S5_GROUPS, S5_STATE), dt),
        jnp.zeros((bp, DEPTH, S5_GROUPS, S5_STATE), dt),
        jnp.zeros((bp, DEPTH, CONV_HIST, C_BR), dt),
        pr, is_prompt=True)
    y_sample, (s_wkv, s_shift, s_s5r, s_s5i, s_conv, s_gmv) = _run_group(
        x_sample, state_rwkv_wkv, state_rwkv_shift, state_s5_re, state_s5_im, cache_conv, pr, is_prompt=False)
    return (y_prompt, y_sample, p_wkv, p_shift, p_s5r, p_s5i, p_conv,
            s_wkv, s_shift, s_s5r, s_s5i, s_conv, s_gmv)
```

```python
import functools
import math

import numpy as np
import jax
import jax.numpy as jnp
from jax import lax
from jax.experimental import pallas as pl
from jax.experimental.pallas import tpu as pltpu

D_MODEL = 1024
DEPTH = 4
N_BRANCH = 4
C_BR = D_MODEL // 4
RW_HEAD = 64
RW_HEADS = C_BR // RW_HEAD
RW_LW = 64
RW_LA = 64
RW_LG = 128
RW_IN = 3 * C_BR + RW_LW + RW_LA + RW_LG
RW_GN_EPS = 64e-5
RW_PAD = 8
RW_GROUP = 16
S5_GW = 16
S5_GROUPS = C_BR // S5_GW
S5_STATE = 64
S5_N = S5_GROUPS * S5_STATE
CONV_W = 31
CONV_HIST = CONV_W - 1
CONV_HIST_PAD = 32
CHUNK = 128
GM_TILE = 4 * CHUNK
GM_HEADS = 4
GM_HEAD = C_BR // GM_HEADS
N_GROUPS = 4
E_PER_GROUP = 4
N_EXPERTS = N_GROUPS * E_PER_GROUP
D_EXPERT = D_MODEL // 4
LN_EPS = 1e-5
DN_ALPHA = (2 * DEPTH) ** 0.25
OFF_S5 = RW_IN
OFF_CV = OFF_S5 + C_BR
OFF_GM = OFF_CV + 2 * C_BR
OFF_GATE = OFF_GM + 2 * C_BR
N_IN = OFF_GATE + N_BRANCH * D_MODEL
P_RW = 0
P_CV = P_RW + RW_IN
P_GM = P_CV + 2 * C_BR
P_S5 = P_GM + 2 * C_BR
N_MIX = P_S5 + C_BR

LANES = 128
VMEM_LIMIT = 56 * 1024 * 1024

F32 = jnp.float32
BF16 = jnp.bfloat16
MM_DTYPE = jnp.bfloat16
HI = lax.Precision.HIGHEST
NEG_BIG = -1e30


def _mm(a, b):
    return jnp.dot(a.astype(MM_DTYPE), b.astype(MM_DTYPE), preferred_element_type=F32)


def _split_bf16(a):
    hi = a.astype(BF16)
    return hi, (a - hi.astype(F32)).astype(BF16)


_NN = ((1,), (0,))
_NT = ((1,), (1,))
_TN = ((0,), (0,))


def _dot(a, b, mode, dims=_NN, exact=None):
    dn = (dims, ((), ()))
    if mode == 'hi':
        return lax.dot_general(a, b, dn, precision=HI, preferred_element_type=F32)
    f = lambda x, y: lax.dot_general(x, y, dn, preferred_element_type=F32)
    if mode == 'bf16':
        return f(a.astype(BF16), b.astype(BF16))
    assert mode == 'x3'
    if exact == 'a':
        b_hi, b_lo = _split_bf16(b)
        a = a.astype(BF16)
        return f(a, b_hi) + f(a, b_lo)
    if exact == 'b':
        a_hi, a_lo = _split_bf16(a)
        b = b.astype(BF16)
        return f(a_hi, b) + f(a_lo, b)
    a_hi, a_lo = _split_bf16(a)
    b_hi, b_lo = _split_bf16(b)
    return f(a_hi, b_hi) + (f(a_hi, b_lo) + f(a_lo, b_hi))


PREC = dict(cumsum='x3', headsum='x3', amat='bf16', inv='bf16', state='bf16', apply='bf16', update='bf16', route='x3')
PREC_SHORT = dict(PREC, update='x3')


def _sigmoid(x):
    return jax.nn.sigmoid(x)


def _softplus(x):
    return jnp.maximum(x, 0.0) + jnp.log1p(jnp.exp(-jnp.abs(x)))


def _gelu_tanh(x):
    return 0.5 * x * (1.0 + jnp.tanh(math.sqrt(2.0 / math.pi) * (x + 0.044715 * (x * x * x))))


def _layer_norm(x, g, b):
    mu = jnp.mean(x, axis=-1, keepdims=True)
    d = x - mu
    var = jnp.mean(d * d, axis=-1, keepdims=True)
    return d * lax.rsqrt(var + LN_EPS) * g + b


def _params(sem):
    return pltpu.CompilerParams(dimension_semantics=sem, vmem_limit_bytes=VMEM_LIMIT)


def _full(shape):
    nd = len(shape)
    return pl.BlockSpec(shape, lambda *_: (0,) * nd)


def _state_buffer(buf):
    return [buf], [pl.BlockSpec(memory_space=pl.ANY)]


def _layer_block(shape, layer, **kwargs):
    nd = len(shape)
    return pl.BlockSpec((None,) + tuple(shape), lambda *_: (layer,) + (0,) * nd, **kwargs)


def _inproj_kernel(x_ref, w_ref, z_ref):
    z_ref[...] = _mm(x_ref[...], w_ref[...])


def _inproj(x2d, w_bf16, layer):
    n = x2d.shape[0]
    tm = 512
    return pl.pallas_call(
        _inproj_kernel,
        grid=(n // tm,),
        in_specs=[pl.BlockSpec((tm, D_MODEL), lambda i: (i, 0)), _layer_block((D_MODEL, N_MIX), layer)],
        out_specs=pl.BlockSpec((tm, N_MIX), lambda i: (i, 0)),
        out_shape=jax.ShapeDtypeStruct((n, N_MIX), F32),
        compiler_params=_params(("parallel",)),
        name="inproj",
    )(x2d, w_bf16)


def _heads_bd(x, lane_head):
    return jnp.concatenate([jnp.where(lane_head == h, x, 0.0) for h in range(RW_HEADS)], axis=0)


def _state_bd(wkv):
    zeros_blk = jnp.zeros((RW_HEAD, RW_HEAD), F32)
    return jnp.concatenate(
        [jnp.concatenate([wkv[h] if g == h else zeros_blk for g in range(RW_HEADS)], axis=1)
         for h in range(RW_HEADS)], axis=0)


def _rwkv_prep(z, z_prev, valid, mu_ref, w0_ref, w2_ref, a0_ref, a2_ref, g2_ref, kkw_ref, kaw_ref, hsum, tri_ref,
               chunk, prec):
    zs = z + mu_ref[...] * (z_prev - z)
    r = zs[:, 0:C_BR]
    k = zs[:, C_BR:2 * C_BR]
    v = zs[:, 2 * C_BR:3 * C_BR]
    lwla = zs[:, 3 * C_BR:3 * C_BR + RW_LW + RW_LA]
    lg = zs[:, 3 * C_BR + RW_LW + RW_LA:]
    w_log = -_softplus(-(w0_ref[...] + _mm(jnp.tanh(lwla), w2_ref[...]))) - 0.5
    ld = -jnp.exp(w_log)
    a = _sigmoid(a0_ref[...] + _mm(lwla, a2_ref[...]))
    g = _mm(_sigmoid(lg), g2_ref[...])
    kk = k * kkw_ref[...]
    kk = kk * lax.rsqrt(jnp.maximum(_dot(kk * kk, hsum, prec['headsum'], exact='b'), 1e-24))
    k2 = k * (1.0 + (a - 1.0) * kaw_ref[...])
    bv = kk * a
    if valid is not None:
        ld = jnp.where(valid, ld, 0.0)
        k2 = jnp.where(valid, k2, 0.0)
        v = jnp.where(valid, v, 0.0)
        bv = jnp.where(valid, bv, 0.0)
    t_tile = z.shape[0]
    n = tri_ref.shape[0]
    lc = jnp.concatenate([_dot(tri_ref[...], ld[i * n:(i + 1) * n], prec['cumsum'], exact='a')
                          for i in range(t_tile // n)], axis=0)
    lend = jnp.concatenate([jnp.broadcast_to(lc[(c + 1) * chunk - 1:(c + 1) * chunk], (chunk, C_BR))
                            for c in range(t_tile // chunk)], axis=0)
    e_end = jnp.exp(lend - lc)
    e_neg = jnp.exp(-lc)
    return dict(r=r, k2=k2, v=v, g=g, rt=r * jnp.exp(lc), kkt=kk * jnp.exp(lc - ld), kh=k2 * e_neg, bh=bv * e_neg,
                kw=k2 * e_end, bw=bv * e_end, wc=jnp.exp(lend))


def _rwkv_chunks_local(chunks, strict_ref, incl_ref, lvl_ref, chunk, prec):
    hc = RW_HEADS * chunk
    nk = RW_HEADS * RW_HEAD
    n = range(len(chunks))
    kkt, rt, kh, bh, vv, kw, bw = (list(x) for x in zip(*chunks))
    amat = [_dot(jnp.concatenate([kkt[c], rt[c]], axis=0), jnp.concatenate([kh[c], bh[c]], axis=0),
                 prec['amat'], _NT) for c in n]
    strict = strict_ref[...] != 0.0
    incl = incl_ref[...] != 0.0
    a_kk = [jnp.where(strict, amat[c][0:hc, 0:hc], 0.0) for c in n]
    a_kb = [jnp.where(strict, amat[c][0:hc, hc:2 * hc], 0.0) for c in n]
    a_rk = [jnp.where(incl, amat[c][hc:2 * hc, 0:hc], 0.0) for c in n]
    a_rb = [jnp.where(incl, amat[c][hc:2 * hc, hc:2 * hc], 0.0) for c in n]
    av = [_dot(jnp.concatenate([a_kk[c], a_rk[c]], axis=0), vv[c], prec['apply']) for c in n]
    ri = lax.broadcasted_iota(jnp.int32, (hc, hc), 0)
    cj = lax.broadcasted_iota(jnp.int32, (hc, hc), 1)
    eye = jnp.where(ri == cj, 1.0, 0.0)
    lvl0 = lvl_ref[0] != 0.0
    t_inv = [eye - jnp.where(lvl0, a_kb[c], 0.0) for c in n]
    for lv in range(1, lvl_ref.shape[0]):
        lvl = lvl_ref[lv] != 0.0
        half = [_dot(t_inv[c], jnp.where(lvl, a_kb[c], 0.0), prec['inv']) for c in n]
        t_inv = [t_inv[c] - _dot(half[c], t_inv[c], prec['inv']) for c in n]
    gu = [_dot(t_inv[c], jnp.concatenate([kkt[c], av[c][0:hc]], axis=1), prec['apply']) for c in n]
    pu = [_dot(a_rb[c], gu[c], prec['apply']) for c in n]
    mc = [_dot(gu[c][:, 0:nk], bw[c], prec['update'], _TN) for c in n]
    nn = [_dot(jnp.concatenate([vv[c], gu[c][:, nk:2 * nk]], axis=0), jnp.concatenate([kw[c], -bw[c]], axis=0),
               prec['update'], _TN) for c in n]
    return [(rt[c] - pu[c][:, 0:nk], av[c][hc:2 * hc] - pu[c][:, nk:2 * nk], mc[c], nn[c]) for c in n]


def _rwkv_chunks_direct(chunks, states, wcs, strict_ref, incl_ref, lvl_ref, chunk, prec):
    hc = RW_HEADS * chunk
    n = range(len(chunks))
    kkt, rt, kh, bh, vv, kw, bw = (list(x) for x in zip(*chunks))
    lhs = [jnp.concatenate([kkt[c], rt[c]], axis=0) for c in n]
    amat = [_dot(lhs[c], jnp.concatenate([kh[c], bh[c]], axis=0), prec['amat'], _NT) for c in n]
    ls = [_dot(lhs[c], states[c], prec['state'], _NT) for c in n]
    strict = strict_ref[...] != 0.0
    incl = incl_ref[...] != 0.0
    a_kk = [jnp.where(strict, amat[c][0:hc, 0:hc], 0.0) for c in n]
    a_kb = [jnp.where(strict, amat[c][0:hc, hc:2 * hc], 0.0) for c in n]
    a_rk = [jnp.where(incl, amat[c][hc:2 * hc, 0:hc], 0.0) for c in n]
    a_rb = [jnp.where(incl, amat[c][hc:2 * hc, hc:2 * hc], 0.0) for c in n]
    av = [_dot(jnp.concatenate([a_kk[c], a_rk[c]], axis=0), vv[c], prec['apply']) for c in n]
    ri = lax.broadcasted_iota(jnp.int32, (hc, hc), 0)
    cj = lax.broadcasted_iota(jnp.int32, (hc, hc), 1)
    eye = jnp.where(ri == cj, 1.0, 0.0)
    lvl0 = lvl_ref[0] != 0.0
    t_inv = [eye - jnp.where(lvl0, a_kb[c], 0.0) for c in n]
    for lv in range(1, lvl_ref.shape[0]):
        lvl = lvl_ref[lv] != 0.0
        half = [_dot(t_inv[c], jnp.where(lvl, a_kb[c], 0.0), prec['inv']) for c in n]
        t_inv = [t_inv[c] - _dot(half[c], t_inv[c], prec['inv']) for c in n]
    u = [_dot(t_inv[c], ls[c][0:hc] + av[c][0:hc], prec['apply']) for c in n]
    o = [ls[c][hc:2 * hc] + av[c][hc:2 * hc] - _dot(a_rb[c], u[c], prec['apply']) for c in n]
    s_new = [states[c] * wcs[c] + _dot(jnp.concatenate([vv[c], u[c]], axis=0),
                                       jnp.concatenate([kw[c], -bw[c]], axis=0), prec['update'], _TN) for c in n]
    out = []
    for c in n:
        o_c = o[c][0:chunk]
        for h in range(1, RW_HEADS):
            o_c = o_c + o[c][h * chunk:(h + 1) * chunk]
        out.append((o_c, s_new[c]))
    return out


def _rwkv_chunk_apply(s, local, wc, chunk, prec):
    p, o0, mc, nn = local
    o = _dot(p, s, prec['state'], _NT) + o0
    o_c = o[0:chunk]
    for h in range(1, RW_HEADS):
        o_c = o_c + o[h * chunk:(h + 1) * chunk]
    return o_c, s * wc - _dot(s, mc, prec['state']) + nn


def _rwkv_post(o, pre, rk_ref, gng_ref, gnb_ref, hsum, prec):
    inv_n = 1.0 / RW_HEAD
    o_mu = _dot(o, hsum, prec['headsum'], exact='b') * inv_n
    od = o - o_mu
    o_var = _dot(od * od, hsum, prec['headsum'], exact='b') * inv_n
    on = od * lax.rsqrt(o_var + RW_GN_EPS) * gng_ref[...] + gnb_ref[...]
    bonus = _dot(pre['r'] * pre['k2'] * rk_ref[...], hsum, prec['headsum'], exact='b') * pre['v']
    return (on + bonus) * pre['g']


_RWKV_LOCAL_KEYS = ('kkt', 'rt', 'kh', 'bh', 'v', 'kw', 'bw')


def _rwkv_kernel(z_ref, prev0_ref, wkv0_ref, mu_ref, w0_ref, w2_ref, a0_ref, a2_ref, g2_ref, kkw_ref, kaw_ref,
                 rk_ref, gng_ref, gnb_ref, hsum_ref, tri_ref, strict_ref, incl_ref, lvl_ref, *rest,
                 t_tile, chunk, t_valid, n_tiles, carry, prec):
    y_ref, wkv1_ref, s_scr, prev_scr = rest[-4:]
    i = pl.program_id(1)
    z = z_ref[...]
    row = lax.broadcasted_iota(jnp.int32, (t_tile, 1), 0)
    if carry:
        assert t_valid == t_tile * n_tiles

        @pl.when(i == 0)
        def _():
            s_scr[...] = _state_bd(wkv0_ref)
            prev_scr[...] = prev0_ref[...]

        z_prev = jnp.where(row == 0, prev_scr[...], pltpu.roll(z, 1, 0))
        prev_scr[...] = z[t_tile - 1:t_tile, :]
        valid = None
    else:
        step = row % chunk
        z_prev = jnp.where(step == 0, prev0_ref[...], pltpu.roll(z, 1, 0))
        valid = step < t_valid
    hsum = hsum_ref[...]
    pre = _rwkv_prep(z, z_prev, valid, mu_ref, w0_ref, w2_ref, a0_ref, a2_ref, g2_ref, kkw_ref, kaw_ref, hsum,
                     tri_ref, chunk, prec)
    lane_head = lax.broadcasted_iota(jnp.int32, (chunk, C_BR), 1) // RW_HEAD
    n_chunks = t_tile // chunk
    chunks = [tuple(_heads_bd(pre[key][c * chunk:(c + 1) * chunk], lane_head) for key in _RWKV_LOCAL_KEYS)
              for c in range(n_chunks)]
    wcs = [pre['wc'][c * chunk:c * chunk + 1] for c in range(n_chunks)]
    o_rows = []
    if carry:
        local = _rwkv_chunks_local(chunks, strict_ref, incl_ref, lvl_ref, chunk, prec)
        s = s_scr[...]
        for c in range(n_chunks):
            o_c, s = _rwkv_chunk_apply(s, local[c], wcs[c], chunk, prec)
            o_rows.append(o_c)
    else:
        states = [_state_bd(wkv0_ref.at[c]) for c in range(n_chunks)]
        for c, (o_c, s_c) in enumerate(_rwkv_chunks_direct(chunks, states, wcs, strict_ref, incl_ref, lvl_ref,
                                                           chunk, prec)):
            o_rows.append(o_c)
            for h in range(RW_HEADS):
                wkv1_ref[c, h] = s_c[h * RW_HEAD:(h + 1) * RW_HEAD, h * RW_HEAD:(h + 1) * RW_HEAD]
    y_ref[...] = _rwkv_post(jnp.concatenate(o_rows, axis=0), pre, rk_ref, gng_ref, gnb_ref, hsum, prec)

    if carry:
        s_scr[...] = s

        @pl.when(i == n_tiles - 1)
        def _():
            for h in range(RW_HEADS):
                wkv1_ref[h] = s[h * RW_HEAD:(h + 1) * RW_HEAD, h * RW_HEAD:(h + 1) * RW_HEAD]


def _rwkv(z3d, col_blk, prev0, wkv0, pw, layer, *, t_tile, chunk, t_valid, carry, prec, wkv_out=None):
    bsz, t_len, _ = z3d.shape
    n_tiles = t_len // t_tile
    cs = min(t_tile, max(chunk, 64))
    assert t_tile % cs == 0 and cs % chunk == 0
    idx = np.arange(cs)
    tri = jnp.asarray(((idx[:, None] // chunk == idx[None, :] // chunk)
                       & (idx[None, :] <= idx[:, None])).astype(np.float32))
    hid = np.arange(C_BR) // RW_HEAD
    hsum = jnp.asarray((hid[:, None] == hid[None, :]).astype(np.float32))
    hc = RW_HEADS * chunk
    hh, tt = np.arange(hc) // chunk, np.arange(hc) % chunk
    same_head = hh[:, None] == hh[None, :]
    strict = jnp.asarray((same_head & (tt[None, :] < tt[:, None])).astype(np.float32))
    incl = jnp.asarray((same_head & (tt[None, :] <= tt[:, None])).astype(np.float32))
    lvls = []
    m = 1
    while m < chunk:
        lvls.append(same_head & (tt[:, None] // (2 * m) == tt[None, :] // (2 * m))
                    & (tt[:, None] % (2 * m) >= m) & (tt[None, :] % (2 * m) < m))
        m *= 2
    lvl = jnp.asarray(np.stack(lvls).astype(np.float32))
    lb = lambda shape: _layer_block(shape, layer)
    vec = lambda n: lb((1, n))
    if carry:
        prev_spec = pl.BlockSpec((None, 1, RW_IN), lambda b, i: (b, 0, 0))
        wkv_spec = pl.BlockSpec((None, RW_HEADS, RW_HEAD, RW_HEAD), lambda b, i: (b, 0, 0, 0))
        extra_in, extra_specs = [], []
    else:
        assert n_tiles == 1
        prev_spec = pl.BlockSpec((None, t_tile, RW_IN), lambda b, i: (b, 0, 0))
        wkv_spec = pl.BlockSpec((t_tile // chunk, None, RW_HEADS, RW_HEAD, RW_HEAD),
                                lambda b, i: (b, layer, 0, 0, 0))
        extra_in, extra_specs = _state_buffer(wkv_out)
    operands = [z3d, prev0, wkv0, pw['mu'], pw['w0'], pw['w2'], pw['a0'], pw['a2'], pw['g2'], pw['kk'], pw['ka'],
                pw['rk'], pw['gn_g'], pw['gn_b'], hsum, tri, strict, incl, lvl] + extra_in
    aliases = {len(operands) - 1: 1} if extra_in else {}
    kern = functools.partial(_rwkv_kernel, t_tile=t_tile, chunk=chunk, t_valid=t_valid, n_tiles=n_tiles, carry=carry,
                             prec=prec)
    return pl.pallas_call(
        kern,
        grid=(bsz, n_tiles),
        in_specs=[pl.BlockSpec((None, t_tile, RW_IN), lambda b, i: (b, i, col_blk)), prev_spec, wkv_spec,
                  vec(RW_IN), vec(C_BR), lb((RW_LW + RW_LA, C_BR)), vec(C_BR), lb((RW_LW + RW_LA, C_BR)),
                  lb((RW_LG, C_BR)), vec(C_BR), vec(C_BR), vec(C_BR), vec(C_BR), vec(C_BR),
                  _full((C_BR, C_BR)), _full((cs, cs)), _full((hc, hc)), _full((hc, hc)), _full(lvl.shape)]
        + extra_specs,
        out_specs=[pl.BlockSpec((None, t_tile, C_BR), lambda b, i: (b, i, 0)), wkv_spec],
        out_shape=[jax.ShapeDtypeStruct((bsz, t_len, C_BR), F32), jax.ShapeDtypeStruct(wkv0.shape, F32)],
        input_output_aliases=aliases,
        scratch_shapes=[pltpu.VMEM((C_BR, C_BR), F32), pltpu.VMEM((1, RW_IN), F32)],
        compiler_params=_params(("parallel", "arbitrary")),
        name="rwkv7",
    )(*operands)


def _s5_kernel(u_ref, h0_ref, lbr_ref, lbi_ref, bb_ref, cc_ref, d_ref, gw_ref, gb_ref,
               y_ref, h1_ref, bu_scr, h_scr, *relayout_scr, bsz, t_tile, n_tiles, time_major):
    i = pl.program_id(0)

    @pl.when(i == 0)
    def _():
        h_scr[...] = h0_ref[...]

    rows = bsz * t_tile
    n_lt = S5_N // LANES
    n_ut = C_BR // LANES
    lane_tile = lambda ref, j: ref[:, j * LANES:(j + 1) * LANES]
    seq_rows = lambda t: pl.ds(t, bsz, stride=t_tile)
    step_rows = lambda t: pl.ds(pl.multiple_of(t * bsz, bsz), bsz)
    u = u_ref[...].reshape(rows, C_BR)
    if time_major:
        bt_scr, tm_scr = relayout_scr
        for j in range(n_ut):
            bt_scr[j] = lane_tile(u, j)

        def to_time_major(t, _):
            for j in range(n_ut):
                tm_scr[j, step_rows(t), :] = bt_scr[j, seq_rows(t), :]
            return 0

        lax.fori_loop(0, t_tile, to_time_major, 0, unroll=4)
        u = jnp.concatenate([tm_scr[j] for j in range(n_ut)], axis=1)
        sl_of = step_rows
    else:
        sl_of = seq_rows
    bu = _mm(u, bb_ref[...])
    for j in range(2 * n_lt):
        bu_scr[j] = lane_tile(bu, j)
    lbr = [jnp.broadcast_to(lane_tile(lbr_ref, j), (bsz, LANES)) for j in range(n_lt)]
    lbi = [jnp.broadcast_to(lane_tile(lbi_ref, j), (bsz, LANES)) for j in range(n_lt)]

    def step(t, carry):
        hr, hi = carry
        sl = sl_of(t)
        new_r, new_i = [], []
        for j in range(n_lt):
            nr = lbr[j] * hr[j] - lbi[j] * hi[j] + bu_scr[j, sl, :]
            ni = lbr[j] * hi[j] + lbi[j] * hr[j] + bu_scr[n_lt + j, sl, :]
            bu_scr[j, sl, :] = nr
            bu_scr[n_lt + j, sl, :] = ni
            new_r.append(nr)
            new_i.append(ni)
        return tuple(new_r), tuple(new_i)

    h_init = (tuple(lane_tile(h_scr, j) for j in range(n_lt)),
              tuple(lane_tile(h_scr, n_lt + j) for j in range(n_lt)))
    hr, hi = lax.fori_loop(0, t_tile, step, h_init)
    for j in range(n_lt):
        h_scr[:, j * LANES:(j + 1) * LANES] = hr[j]
        h_scr[:, (n_lt + j) * LANES:(n_lt + j + 1) * LANES] = hi[j]

    hs = jnp.concatenate([bu_scr[j] for j in range(2 * n_lt)], axis=1)
    y = _mm(hs, cc_ref[...]) + d_ref[...] * u
    y = _gelu_tanh(y)
    y = y * _sigmoid(_mm(y, gw_ref[...]) + gb_ref[...])
    if time_major:
        for j in range(n_ut):
            tm_scr[j] = lane_tile(y, j)

        def to_seq_major(t, _):
            for j in range(n_ut):
                bt_scr[j, seq_rows(t), :] = tm_scr[j, step_rows(t), :]
            return 0

        lax.fori_loop(0, t_tile, to_seq_major, 0, unroll=4)
        y = jnp.concatenate([bt_scr[j] for j in range(n_ut)], axis=1)
    y_ref[...] = y.reshape(y_ref.shape)

    @pl.when(i == n_tiles - 1)
    def _():
        h1_ref[...] = h_scr[...]


def _s5(z, col_blk, h0, ps, layer, *, bsz, t_len, t_tile):
    lb = lambda shape: _layer_block(shape, layer)
    n_tiles = t_len // t_tile
    rows = bsz * t_tile
    if z.ndim == 3:
        u_spec = pl.BlockSpec((bsz, t_tile, C_BR), lambda i: (0, i, col_blk))
        y_spec = pl.BlockSpec((bsz, t_tile, C_BR), lambda i: (0, i, 0))
        y_shape = (bsz, t_len, C_BR)
    else:
        assert n_tiles == 1
        u_spec = pl.BlockSpec((rows, C_BR), lambda i: (0, col_blk))
        y_spec = pl.BlockSpec((rows, C_BR), lambda i: (0, 0))
        y_shape = (rows, C_BR)
    time_major = z.ndim == 3 and bsz == 8
    relayout_scr = [pltpu.VMEM((C_BR // LANES, rows, LANES), F32)] * 2 if time_major else []
    kern = functools.partial(_s5_kernel, bsz=bsz, t_tile=t_tile, n_tiles=n_tiles, time_major=time_major)
    return pl.pallas_call(
        kern,
        grid=(n_tiles,),
        in_specs=[u_spec, _full((bsz, 2 * S5_N)), lb((1, S5_N)), lb((1, S5_N)),
                  lb((C_BR, 2 * S5_N)), lb((2 * S5_N, C_BR)), lb((1, C_BR)), lb((C_BR, C_BR)), lb((1, C_BR))],
        out_specs=[y_spec, _full((bsz, 2 * S5_N))],
        out_shape=[jax.ShapeDtypeStruct(y_shape, F32), jax.ShapeDtypeStruct((bsz, 2 * S5_N), F32)],
        scratch_shapes=[pltpu.VMEM((2 * S5_N // LANES, rows, LANES), F32), pltpu.VMEM((bsz, 2 * S5_N), F32)]
        + relayout_scr,
        compiler_params=_params(("arbitrary",)),
        name="s5",
    )(z, h0, ps['lb_re'], ps['lb_im'], ps['bb'], ps['cc'], ps['d'], ps['glu_w'], ps['glu_b'])


def _conv_taps(full_scr, w_ref, t_tile):
    lo = CONV_HIST_PAD - CONV_HIST
    sub = 8
    assert t_tile % sub == 0
    acc = None
    for rho in range(sub):
        offs = [o for o in range(rho, lo + CONV_W, sub) if o >= lo]
        rows = t_tile + (sub if rho else 0)
        part = None
        for o in offs:
            term = full_scr[o - rho:o - rho + rows, :] * w_ref[o - lo:o - lo + 1, :]
            part = term if part is None else part + term
        part = part[rho:rho + t_tile]
        acc = part if acc is None else acc + part
    return acc


def _conv_kernel(z_ref, c0_ref, w_ref, b_ref, g_ref, be_ref, *rest, t_tile, n_tiles):
    y_ref, c1_ref, full_scr = rest[-3:]
    i = pl.program_id(1)
    lo = CONV_HIST_PAD - CONV_HIST

    @pl.when(i == 0)
    def _():
        full_scr[0:lo, :] = jnp.zeros((lo, C_BR), F32)
        full_scr[lo:CONV_HIST_PAD, :] = c0_ref[...]

    z = z_ref[...]
    full_scr[CONV_HIST_PAD:CONV_HIST_PAD + t_tile, :] = z[:, 0:C_BR] * _sigmoid(z[:, C_BR:2 * C_BR])
    y = _layer_norm(_conv_taps(full_scr, w_ref, t_tile) + b_ref[...], g_ref[...], be_ref[...])
    y_ref[...] = y * _sigmoid(y)
    hist = full_scr[t_tile:t_tile + CONV_HIST_PAD, :]
    full_scr[0:CONV_HIST_PAD, :] = hist

    @pl.when(i == n_tiles - 1)
    def _():
        c1_ref[...] = hist[lo:, :]


def _conv(z3d, col_blk, conv0, conv_out, layer, pc, *, t_tile):
    bsz, t_len, _ = z3d.shape
    n_tiles = t_len // t_tile
    state_spec = pl.BlockSpec((None, None, CONV_HIST, C_BR), lambda b, i: (b, layer, 0, 0))
    extra_in, extra_specs = _state_buffer(conv_out)
    kern = functools.partial(_conv_kernel, t_tile=t_tile, n_tiles=n_tiles)
    return pl.pallas_call(
        kern,
        grid=(bsz, n_tiles),
        in_specs=[pl.BlockSpec((None, t_tile, 2 * C_BR), lambda b, i: (b, i, col_blk)), state_spec,
                  _layer_block((CONV_W, C_BR), layer), _layer_block((1, C_BR), layer),
                  _layer_block((1, C_BR), layer), _layer_block((1, C_BR), layer)] + extra_specs,
        out_specs=[pl.BlockSpec((None, t_tile, C_BR), lambda b, i: (b, i, 0)), state_spec],
        out_shape=[jax.ShapeDtypeStruct((bsz, t_len, C_BR), F32), jax.ShapeDtypeStruct(conv0.shape, F32)],
        input_output_aliases={6: 1},
        scratch_shapes=[pltpu.VMEM((CONV_HIST_PAD + t_tile, C_BR), F32)],
        compiler_params=_params(("parallel", "arbitrary")),
        name="conv",
    )(z3d, conv0, pc['w'], pc['b'], pc['ln_g'], pc['ln_b'], *extra_in)


def _conv_short_kernel(z_ref, c0_ref, w_ref, b_ref, g_ref, be_ref, *rest, bsz, t_len):
    y_ref, c1_ref, in_scr, out_scr = rest[-4:]
    n_lt = C_BR // LANES
    z = z_ref[...]
    c = z[:, 0:C_BR] * _sigmoid(z[:, C_BR:2 * C_BR])
    for j in range(n_lt):
        in_scr[j] = c[:, j * LANES:(j + 1) * LANES]
    step_rows = lambda t: pl.ds(t, bsz, stride=t_len)
    hist = lambda r: c0_ref[:, r * C_BR:(r + 1) * C_BR]
    new = [jnp.concatenate([in_scr[j, step_rows(t), :] for j in range(n_lt)], axis=1) for t in range(t_len)]
    full = lambda r: hist(r) if r < CONV_HIST else new[r - CONV_HIST]
    for t in range(t_len):
        acc = b_ref[...] + full(t) * w_ref[0:1, :]
        for j in range(1, CONV_W):
            acc = acc + full(t + j) * w_ref[j:j + 1, :]
        y = _layer_norm(acc, g_ref[...], be_ref[...])
        y = y * _sigmoid(y)
        for j in range(n_lt):
            out_scr[j, step_rows(t), :] = y[:, j * LANES:(j + 1) * LANES]
    y_ref[...] = jnp.concatenate([out_scr[j] for j in range(n_lt)], axis=1)
    for r in range(CONV_HIST):
        c1_ref[:, r * C_BR:(r + 1) * C_BR] = full(r + t_len)


def _conv_short(z2d, col_blk, conv0, conv_out, layer, pc, *, bsz, t_len):
    rows = bsz * t_len
    width = CONV_HIST * C_BR
    state_spec = pl.BlockSpec((bsz, width), lambda i: (0, layer))
    extra_in, extra_specs = _state_buffer(conv_out)
    kern = functools.partial(_conv_short_kernel, bsz=bsz, t_len=t_len)
    return pl.pallas_call(
        kern,
        grid=(1,),
        in_specs=[pl.BlockSpec((rows, 2 * C_BR), lambda i: (0, col_blk)), state_spec,
                  _layer_block((CONV_W, C_BR), layer), _layer_block((1, C_BR), layer),
                  _layer_block((1, C_BR), layer), _layer_block((1, C_BR), layer)] + extra_specs,
        out_specs=[pl.BlockSpec((rows, C_BR), lambda i: (0, 0)), state_spec],
        out_shape=[jax.ShapeDtypeStruct((rows, C_BR), F32), jax.ShapeDtypeStruct(conv0.shape, F32)],
        input_output_aliases={6: 1},
        scratch_shapes=[pltpu.VMEM((C_BR // LANES, rows, LANES), F32)] * 2,
        compiler_params=_params(("arbitrary",)),
        name="conv_short",
    )(z2d, conv0, pc['w'], pc['b'], pc['ln_g'], pc['ln_b'], *extra_in)


def _gmlp_kernel(z_ref, g_ref, b_ref, wm_ref, bias_ref, y_ref, *v_ref):
    z = z_ref[...]
    u = z[:, 0:C_BR]
    v = _layer_norm(z[:, C_BR:2 * C_BR], g_ref[...], b_ref[...])
    if v_ref:
        v_ref[0][...] = v
    vb = v.astype(MM_DTYPE)
    head = lax.broadcasted_iota(jnp.int32, (CHUNK, C_BR), 1) // GM_HEAD
    for c in range(GM_TILE // CHUNK):
        rows = slice(c * CHUNK, (c + 1) * CHUNK)
        s = bias_ref[...]
        for h in range(GM_HEADS):
            s = s + jnp.where(head == h, jnp.dot(wm_ref[h], vb[rows], preferred_element_type=F32), 0.0)
        y_ref[rows, :] = u[rows] * s


def _gmlp(z3d, col_blk, pg, wm, bias, layer, *, emit_v):
    lb = lambda shape: _layer_block(shape, layer)
    bsz, t_len, _ = z3d.shape
    out_spec = pl.BlockSpec((None, GM_TILE, C_BR), lambda b, i: (b, i, 0))
    n_out = 2 if emit_v else 1
    return pl.pallas_call(
        _gmlp_kernel,
        grid=(bsz, t_len // GM_TILE),
        in_specs=[pl.BlockSpec((None, GM_TILE, 2 * C_BR), lambda b, i: (b, i, col_blk)),
                  lb((1, C_BR)), lb((1, C_BR)), lb((GM_HEADS, CHUNK, CHUNK)), lb((CHUNK, C_BR))],
        out_specs=[out_spec] * n_out,
        out_shape=[jax.ShapeDtypeStruct((bsz, t_len, C_BR), F32)] * n_out,
        compiler_params=_params(("parallel", "parallel")),
        name="gmlp",
    )(z3d, pg['ln_g'], pg['ln_b'], wm, bias)


def _merge_kernel(x_ref, yrw_ref, ys5_ref, ycv_ref, ygm_ref, wg_ref, wb_ref, wo_ref, g_ref, b_ref, o_ref):
    tm = x_ref.shape[0]
    halves = [slice(0, tm // 2), slice(tm // 2, tm)]
    x = [x_ref[h, :] for h in halves]
    xb = [v.astype(MM_DTYPE) for v in x]
    merged = [None for _ in halves]
    for bidx, y_ref in enumerate((yrw_ref, ys5_ref, ycv_ref, ygm_ref)):
        wg = wg_ref[:, bidx * D_MODEL:(bidx + 1) * D_MODEL]
        gate = [_sigmoid(jnp.dot(v, wg, preferred_element_type=F32)) for v in xb]
        term = [gate[i] * _mm(y_ref[h, :], wb_ref[bidx]) for i, h in enumerate(halves)]
        merged = [t if m is None else m + t for m, t in zip(merged, term)]
    proj = [_mm(m, wo_ref[...]) for m in merged]
    for i, h in enumerate(halves):
        o_ref[h, :] = _layer_norm(DN_ALPHA * x[i] + proj[i], g_ref[...], b_ref[...])


def _merge(x2d, ys, big, pm, layer):
    n = x2d.shape[0]
    tm = 512
    row = lambda w: pl.BlockSpec((tm, w), lambda i: (i, 0))
    return pl.pallas_call(
        _merge_kernel,
        grid=(n // tm,),
        in_specs=[row(D_MODEL), row(C_BR), row(C_BR), row(C_BR), row(C_BR),
                  _layer_block((D_MODEL, N_BRANCH * D_MODEL), layer), _layer_block((N_BRANCH, C_BR, D_MODEL), layer),
                  _layer_block((D_MODEL, D_MODEL), layer), _layer_block((1, D_MODEL), layer),
                  _layer_block((1, D_MODEL), layer)],
        out_specs=row(D_MODEL),
        out_shape=jax.ShapeDtypeStruct((n, D_MODEL), F32),
        compiler_params=_params(("parallel",)),
        name="merge",
    )(x2d, *ys, big['w_gate'], big['w_branch'], big['w_out'], pm['ln1_g'], pm['ln1_b'])


def _moe_kernel(x_ref, wg_ref, bg_ref, wu_ref, wd_ref, g_ref, b_ref, o_ref, hh_scr, *, tm):
    lane = lax.broadcasted_iota(jnp.int32, (tm, LANES), 1)
    x = x_ref[...]
    xb = x.astype(MM_DTYPE)
    logits = _dot(x, wg_ref[...], PREC['route']) + bg_ref[...]
    gl = jnp.where(lane < N_GROUPS, logits, NEG_BIG)
    gmax = jnp.max(gl, axis=-1, keepdims=True)
    g_sel = jnp.min(jnp.where(gl == gmax, lane, LANES), axis=-1, keepdims=True)
    p_group = 1.0 / jnp.sum(jnp.where(lane < N_GROUPS, jnp.exp(gl - gmax), 0.0), axis=-1, keepdims=True)
    first = N_GROUPS + g_sel * E_PER_GROUP
    el = jnp.where((lane >= first) & (lane < first + E_PER_GROUP), logits, NEG_BIG)
    m1 = jnp.max(el, axis=-1, keepdims=True)
    i1 = jnp.min(jnp.where(el == m1, lane, LANES), axis=-1, keepdims=True)
    el2 = jnp.where(lane == i1, NEG_BIG, el)
    m2 = jnp.max(el2, axis=-1, keepdims=True)
    i2 = jnp.min(jnp.where(el2 == m2, lane, LANES), axis=-1, keepdims=True)
    e2 = jnp.exp(m2 - m1)
    w1 = p_group / (1.0 + e2)
    w2 = p_group * e2 / (1.0 + e2)
    for e in range(N_EXPERTS):
        comb_e = jnp.where(i1 == e + N_GROUPS, w1, 0.0) + jnp.where(i2 == e + N_GROUPS, w2, 0.0)
        h = jnp.dot(xb, wu_ref[e], preferred_element_type=F32)
        h1 = h[:, 0:D_EXPERT]
        hh = h1 * _sigmoid(h1) * h[:, D_EXPERT:2 * D_EXPERT] * comb_e
        hh_scr[:, e * D_EXPERT:(e + 1) * D_EXPERT] = hh.astype(MM_DTYPE)
    moe = jnp.dot(hh_scr[...], wd_ref[...], preferred_element_type=F32)
    o_ref[...] = _layer_norm(DN_ALPHA * x + moe, g_ref[...], b_ref[...])


def _moe(x2d, big, pe, layer):
    n = x2d.shape[0]
    tm = 512
    kern = functools.partial(_moe_kernel, tm=tm)
    resident = lambda shape: _layer_block(shape, layer, pipeline_mode=pl.Buffered(1))
    return pl.pallas_call(
        kern,
        grid=(n // tm,),
        in_specs=[pl.BlockSpec((tm, D_MODEL), lambda i: (i, 0)),
                  _layer_block((D_MODEL, LANES), layer), _layer_block((1, LANES), layer),
                  resident((N_EXPERTS, D_MODEL, 2 * D_EXPERT)), resident((N_EXPERTS * D_EXPERT, D_MODEL)),
                  _layer_block((1, D_MODEL), layer), _layer_block((1, D_MODEL), layer)],
        out_specs=pl.BlockSpec((tm, D_MODEL), lambda i: (i, 0)),
        out_shape=jax.ShapeDtypeStruct((n, D_MODEL), F32),
        scratch_shapes=[pltpu.VMEM((tm, N_EXPERTS * D_EXPERT), MM_DTYPE)],
        compiler_params=_params(("parallel",)),
        name="moe",
    )(x2d, pe['wg'], pe['bg'], big['w_up'], big['w_down'], pe['ln2_g'], pe['ln2_b'])


def _block_diag(blocks):
    nl, g, m, n = blocks.shape
    eye = jnp.eye(g, dtype=blocks.dtype)
    return (eye[None, :, None, :, None] * blocks[:, :, :, None, :]).reshape(nl, g * m, g * n)


def _prep_params(p, t_short):
    nl = p['w_in'].shape[0]
    row = lambda a: a.reshape(nl, 1, -1).astype(F32)
    w_in = p['w_in']
    big = dict(w_mix=jnp.concatenate([w_in[..., :OFF_S5], w_in[..., OFF_CV:OFF_GM], w_in[..., OFF_GM:OFF_GATE],
                                      w_in[..., OFF_S5:OFF_CV]], axis=-1).astype(MM_DTYPE),
               w_gate=w_in[..., OFF_GATE:].astype(MM_DTYPE),
               w_branch=p['w_branch'].astype(MM_DTYPE), w_out=p['w_out'].astype(MM_DTYPE),
               w_up=p['moe_w_up'].astype(MM_DTYPE),
               w_down=p['moe_w_down'].astype(MM_DTYPE).reshape(nl, N_EXPERTS * D_EXPERT, D_MODEL))
    zeros_lora = jnp.zeros((nl, RW_LW, C_BR), F32)
    rw = dict(mu=row(p['rw_mu']), w0=row(p['rw_w0']),
              w2=jnp.concatenate([p['rw_w2'], zeros_lora], axis=1).astype(MM_DTYPE), a0=row(p['rw_a0']),
              a2=jnp.concatenate([zeros_lora, p['rw_a2']], axis=1).astype(MM_DTYPE),
              g2=p['rw_g2'].astype(MM_DTYPE), kk=row(p['rw_kk']), ka=row(p['rw_ka']), rk=row(p['rw_rk']),
              gn_g=row(p['rw_gn_g']), gn_b=row(p['rw_gn_b']))
    lr, li = p['s5_lam_re'].astype(F32), p['s5_lam_im'].astype(F32)
    dt = jnp.exp(p['s5_log_dt'].astype(F32))[..., None]
    mag = jnp.exp(lr * dt)
    lb_re, lb_im = mag * jnp.cos(li * dt), mag * jnp.sin(li * dt)
    den = lr * lr + li * li
    q_re = ((lb_re - 1.0) * lr + lb_im * li) / den
    q_im = (lb_im * lr - (lb_re - 1.0) * li) / den
    br, bi = p['s5_b_re'].astype(F32), p['s5_b_im'].astype(F32)
    bb_re = q_re[..., None] * br - q_im[..., None] * bi
    bb_im = q_re[..., None] * bi + q_im[..., None] * br
    t23 = lambda a: jnp.swapaxes(a, 2, 3)
    bb = jnp.concatenate([_block_diag(t23(bb_re)), _block_diag(t23(bb_im))], axis=2).astype(MM_DTYPE)
    cc = jnp.concatenate([_block_diag(t23(p['s5_c_re'].astype(F32))), -_block_diag(t23(p['s5_c_im'].astype(F32)))],
                         axis=1).astype(MM_DTYPE)
    s5 = dict(lb_re=row(lb_re), lb_im=row(lb_im), bb=bb, cc=cc, d=row(p['s5_d']),
              glu_w=p['s5_glu_w'].astype(MM_DTYPE), glu_b=row(p['s5_glu_b']))
    cv = dict(w=p['cv_w'].astype(F32), b=row(p['cv_b']), ln_g=row(p['cv_ln_g']), ln_b=row(p['cv_ln_b']))
    causal = jnp.tril(jnp.ones((CHUNK, CHUNK), dtype=bool))
    wm = jnp.where(causal, p['gm_ws'], 0).astype(F32)
    bias = jnp.repeat(jnp.swapaxes(p['gm_bs'], 1, 2), GM_HEAD, axis=2).astype(F32)
    reps = CHUNK // t_short
    wm_short = jnp.einsum('rs,lhij->lhrisj', jnp.eye(reps, dtype=F32),
                          wm[:, :, :t_short, :t_short]).reshape(nl, GM_HEADS, CHUNK, CHUNK)
    gm = dict(ln_g=row(p['gm_ln_g']), ln_b=row(p['gm_ln_b']), wm=wm.astype(MM_DTYPE), bias=bias,
              wm_short=wm_short.astype(MM_DTYPE), bias_short=jnp.tile(bias[:, :t_short], (1, reps, 1)))
    mg = dict(ln1_g=row(p['ln1_g']), ln1_b=row(p['ln1_b']))
    pad = LANES - N_GROUPS - N_EXPERTS
    wg = jnp.concatenate([p['moe_wg1'], p['moe_wg2'], jnp.zeros((nl, D_MODEL, pad), F32)], axis=2).astype(F32)
    bg = row(jnp.concatenate([p['moe_bg1'], p['moe_bg2'], jnp.zeros((nl, pad), F32)], axis=1))
    moe = dict(wg=wg, bg=bg, ln2_g=row(p['ln2_g']), ln2_b=row(p['ln2_b']))
    return dict(big=big, rw=rw, s5=s5, cv=cv, gm=gm, mg=mg, moe=moe)


def _gmlp_group(z3d, pg, layer, *, is_prompt):
    bsz, t_len, n_cols = z3d.shape
    if is_prompt:
        y, = _gmlp(z3d, P_GM // (2 * C_BR), pg, pg['wm'], pg['bias'], layer, emit_v=False)
        return y, None
    y, v = _gmlp(z3d.reshape(1, bsz * t_len, n_cols), P_GM // (2 * C_BR), pg, pg['wm_short'], pg['bias_short'],
                 layer, emit_v=True)
    return y.reshape(bsz, t_len, C_BR), v.reshape(bsz, t_len, C_BR)


def _run_group(x, wkv0, shift0, s5r0, s5i0, conv0, pr, *, is_prompt):
    bsz, t_len, _ = x.shape
    big = pr['big']
    n = bsz * t_len
    x2d = x.reshape(n, D_MODEL)
    outs = []
    conv_shape = conv0.shape
    if not is_prompt:
        conv0 = conv0.reshape(bsz, DEPTH * CONV_HIST * C_BR)
    conv_buf = jnp.zeros_like(conv0)
    if not is_prompt:
        wkv_buf = jnp.zeros_like(wkv0)
        grp_rows = lambda a: a.reshape((bsz // RW_GROUP, RW_GROUP * a.shape[1]) + a.shape[2:])
    for l in range(DEPTH):
        z2d = _inproj(x2d, big['w_mix'], l)
        z3d = z2d.reshape(bsz, t_len, N_MIX)
        sh0 = shift0[:, l].reshape(bsz, 1, RW_IN)
        if is_prompt:
            y_rw, wkv1 = _rwkv(z3d, P_RW // RW_IN, sh0, wkv0[:, l], pr['rw'], l, t_tile=512, chunk=64,
                               t_valid=t_len, carry=True, prec=PREC)
        else:
            z_rw = grp_rows(jnp.pad(z3d[:, :, P_RW:P_RW + RW_IN], ((0, 0), (0, RW_PAD - t_len), (0, 0))))
            prev0 = grp_rows(jnp.pad(sh0, ((0, 0), (0, RW_PAD - 1), (0, 0))))
            y_rw, wkv_buf = _rwkv(z_rw, 0, prev0, wkv0, pr['rw'], l, t_tile=RW_GROUP * RW_PAD, chunk=RW_PAD,
                                  t_valid=t_len, carry=False, prec=PREC_SHORT, wkv_out=wkv_buf)
            y_rw = y_rw.reshape(bsz, RW_PAD, C_BR)[:, :t_len]
            wkv1 = None
        shift1 = z3d[:, t_len - 1, P_RW:P_RW + RW_IN]
        h0 = jnp.concatenate([s5r0[:, l].reshape(bsz, S5_N), s5i0[:, l].reshape(bsz, S5_N)], axis=1)
        if is_prompt:
            y_s5, h1 = _s5(z3d, P_S5 // C_BR, h0, pr['s5'], l, bsz=bsz, t_len=t_len, t_tile=128)
        else:
            y_s5, h1 = _s5(z2d, P_S5 // C_BR, h0, pr['s5'], l, bsz=bsz, t_len=t_len, t_tile=t_len)
        s5r1 = h1[:, :S5_N].reshape(bsz, S5_GROUPS, S5_STATE)
        s5i1 = h1[:, S5_N:].reshape(bsz, S5_GROUPS, S5_STATE)
        if is_prompt:
            y_cv, conv_buf = _conv(z3d, P_CV // (2 * C_BR), conv0, conv_buf, l, pr['cv'], t_tile=256)
        else:
            y_cv, conv_buf = _conv_short(z2d, P_CV // (2 * C_BR), conv0, conv_buf, l, pr['cv'], bsz=bsz,
                                         t_len=t_len)
        y_gm, v_gm = _gmlp_group(z3d, pr['gm'], l, is_prompt=is_prompt)
        ys = [y.reshape(n, C_BR) for y in (y_rw, y_s5, y_cv, y_gm)]
        x2d = _merge(x2d, ys, big, pr['mg'], l)
        x2d = _moe(x2d, big, pr['moe'], l)
        outs.append((wkv1, shift1, s5r1, s5i1, v_gm))
    stack = lambda i: None if outs[0][i] is None else jnp.stack([o[i] for o in outs], axis=1)
    wkv = stack(0) if is_prompt else wkv_buf
    return (x2d.reshape(bsz, t_len, D_MODEL),
            (wkv, stack(1), stack(2), stack(3), conv_buf.reshape(conv_shape), stack(4)))


def kernel(x_prompt, x_sample, state_rwkv_wkv, state_rwkv_shift, state_s5_re, state_s5_im, cache_conv,
           w_in, rw_mu, rw_w0, rw_w2, rw_a0, rw_a2, rw_g2, rw_kk, rw_ka, rw_rk, rw_gn_g, rw_gn_b,
           s5_lam_re, s5_lam_im, s5_log_dt, s5_b_re, s5_b_im, s5_c_re, s5_c_im, s5_d, s5_glu_w, s5_glu_b,
           cv_w, cv_b, cv_ln_g, cv_ln_b, gm_ln_g, gm_ln_b, gm_ws, gm_bs,
           w_branch, w_out, ln1_g, ln1_b,
           moe_wg1, moe_bg1, moe_wg2, moe_bg2, moe_w_up, moe_w_down, ln2_g, ln2_b):
    p = dict(w_in=w_in, rw_mu=rw_mu, rw_w0=rw_w0, rw_w2=rw_w2, rw_a0=rw_a0, rw_a2=rw_a2, rw_g2=rw_g2,
             rw_kk=rw_kk, rw_ka=rw_ka, rw_rk=rw_rk, rw_gn_g=rw_gn_g, rw_gn_b=rw_gn_b,
             s5_lam_re=s5_lam_re, s5_lam_im=s5_lam_im, s5_log_dt=s5_log_dt, s5_b_re=s5_b_re, s5_b_im=s5_b_im,
             s5_c_re=s5_c_re, s5_c_im=s5_c_im, s5_d=s5_d, s5_glu_w=s5_glu_w, s5_glu_b=s5_glu_b,
             cv_w=cv_w, cv_b=cv_b, cv_ln_g=cv_ln_g, cv_ln_b=cv_ln_b, gm_ln_g=gm_ln_g, gm_ln_b=gm_ln_b,
             gm_ws=gm_ws, gm_bs=gm_bs, w_branch=w_branch, w_out=w_out, ln1_g=ln1_g, ln1_b=ln1_b,
             moe_wg1=moe_wg1, moe_bg1=moe_bg1, moe_wg2=moe_wg2, moe_bg2=moe_bg2, moe_w_up=moe_w_up,
             moe_w_down=moe_w_down, ln2_g=ln2_g, ln2_b=ln2_b)
    pr = _prep_params(p, x_sample.shape[1])
    bp = x_prompt.shape[0]
    dt = x_prompt.dtype
    y_prompt, (p_wkv, p_shift, p_s5r, p_s5i, p_conv, _) = _run_group(
        x_prompt,
        jnp.zeros((bp, DEPTH, RW_HEADS, RW_HEAD, RW_HEAD), dt),
        jnp.zeros((bp, DEPTH, RW_IN), dt),
        jnp.zeros((bp, DEPTH, S5_GROUPS, S5_STATE), dt),
        jnp.zeros((bp, DEPTH, S5_GROUPS, S5_STATE), dt),
        jnp.zeros((bp, DEPTH, CONV_HIST, C_BR), dt),
        pr, is_prompt=True)
    y_sample, (s_wkv, s_shift, s_s5r, s_s5i, s_conv, s_gmv) = _run_group(
        x_sample, state_rwkv_wkv, state_rwkv_shift, state_s5_re, state_s5_im, cache_conv, pr, is_prompt=False)
    return (y_prompt, y_sample, p_wkv, p_shift, p_s5r, p_s5i, p_conv,
            s_wkv, s_shift, s_s5r, s_s5i, s_conv, s_gmv)
```

```python
import functools
import math

import numpy as np
import jax
import jax.numpy as jnp
from jax import lax
from jax.experimental import pallas as pl
from jax.experimental.pallas import tpu as pltpu

D_MODEL = 1024
DEPTH = 4
N_BRANCH = 4
C_BR = D_MODEL // 4
RW_HEAD = 64
RW_HEADS = C_BR // RW_HEAD
RW_LW = 64
RW_LA = 64
RW_LG = 128
RW_IN = 3 * C_BR + RW_LW + RW_LA + RW_LG
RW_GN_EPS = 64e-5
RW_PAD = 8
RW_GROUP = 16
S5_GW = 16
S5_GROUPS = C_BR // S5_GW
S5_STATE = 64
S5_N = S5_GROUPS * S5_STATE
CONV_W = 31
CONV_HIST = CONV_W - 1
CONV_HIST_PAD = 32
CHUNK = 128
GM_TILE = 8 * CHUNK
GM_HEADS = 4
GM_HEAD = C_BR // GM_HEADS
N_GROUPS = 4
E_PER_GROUP = 4
N_EXPERTS = N_GROUPS * E_PER_GROUP
D_EXPERT = D_MODEL // 4
LN_EPS = 1e-5
DN_ALPHA = (2 * DEPTH) ** 0.25
OFF_S5 = RW_IN
OFF_CV = OFF_S5 + C_BR
OFF_GM = OFF_CV + 2 * C_BR
OFF_GATE = OFF_GM + 2 * C_BR
N_IN = OFF_GATE + N_BRANCH * D_MODEL
P_RW = 0
P_CV = P_RW + RW_IN
P_GM = P_CV + 2 * C_BR
P_S5 = P_GM + 2 * C_BR
N_MIX = P_S5 + C_BR

LANES = 128
VMEM_LIMIT = 56 * 1024 * 1024

F32 = jnp.float32
BF16 = jnp.bfloat16
MM_DTYPE = jnp.bfloat16
HI = lax.Precision.HIGHEST
NEG_BIG = -1e30


def _mm(a, b):
    return jnp.dot(a.astype(MM_DTYPE), b.astype(MM_DTYPE), preferred_element_type=F32)


def _split_bf16(a):
    hi = a.astype(BF16)
    return hi, (a - hi.astype(F32)).astype(BF16)


_NN = ((1,), (0,))
_NT = ((1,), (1,))
_TN = ((0,), (0,))


def _dot(a, b, mode, dims=_NN, exact=None):
    dn = (dims, ((), ()))
    if mode == 'hi':
        return lax.dot_general(a, b, dn, precision=HI, preferred_element_type=F32)
    f = lambda x, y: lax.dot_general(x, y, dn, preferred_element_type=F32)
    if mode == 'bf16':
        return f(a.astype(BF16), b.astype(BF16))
    assert mode == 'x3'
    if exact == 'a':
        b_hi, b_lo = _split_bf16(b)
        a = a.astype(BF16)
        return f(a, b_hi) + f(a, b_lo)
    if exact == 'b':
        a_hi, a_lo = _split_bf16(a)
        b = b.astype(BF16)
        return f(a_hi, b) + f(a_lo, b)
    a_hi, a_lo = _split_bf16(a)
    b_hi, b_lo = _split_bf16(b)
    return f(a_hi, b_hi) + (f(a_hi, b_lo) + f(a_lo, b_hi))


PREC = dict(cumsum='x3', headsum='x3', amat='bf16', inv='bf16', state='bf16', apply='bf16', update='bf16', route='x3')
PREC_SHORT = dict(PREC, update='x3')


def _sigmoid(x):
    return jax.nn.sigmoid(x)


def _softplus(x):
    return jnp.maximum(x, 0.0) + jnp.log1p(jnp.exp(-jnp.abs(x)))


def _gelu_tanh(x):
    return 0.5 * x * (1.0 + jnp.tanh(math.sqrt(2.0 / math.pi) * (x + 0.044715 * (x * x * x))))


def _layer_norm(x, g, b):
    mu = jnp.mean(x, axis=-1, keepdims=True)
    d = x - mu
    var = jnp.mean(d * d, axis=-1, keepdims=True)
    return d * lax.rsqrt(var + LN_EPS) * g + b


def _params(sem):
    return pltpu.CompilerParams(dimension_semantics=sem, vmem_limit_bytes=VMEM_LIMIT)


def _full(shape):
    nd = len(shape)
    return pl.BlockSpec(shape, lambda *_: (0,) * nd)


def _state_buffer(buf):
    return [buf], [pl.BlockSpec(memory_space=pl.ANY)]


def _layer_block(shape, layer, **kwargs):
    nd = len(shape)
    return pl.BlockSpec((None,) + tuple(shape), lambda *_: (layer,) + (0,) * nd, **kwargs)


def _inproj_kernel(x_ref, w_ref, z_ref):
    z_ref[...] = _mm(x_ref[...], w_ref[...])


def _inproj(x2d, w_bf16, layer):
    n = x2d.shape[0]
    tm = 512
    return pl.pallas_call(
        _inproj_kernel,
        grid=(n // tm,),
        in_specs=[pl.BlockSpec((tm, D_MODEL), lambda i: (i, 0)), _layer_block((D_MODEL, N_MIX), layer)],
        out_specs=pl.BlockSpec((tm, N_MIX), lambda i: (i, 0)),
        out_shape=jax.ShapeDtypeStruct((n, N_MIX), F32),
        compiler_params=_params(("parallel",)),
        name="inproj",
    )(x2d, w_bf16)


def _heads_bd(x, lane_head):
    return jnp.concatenate([jnp.where(lane_head == h, x, 0.0) for h in range(RW_HEADS)], axis=0)


def _state_bd(wkv):
    zeros_blk = jnp.zeros((RW_HEAD, RW_HEAD), F32)
    return jnp.concatenate(
        [jnp.concatenate([wkv[h] if g == h else zeros_blk for g in range(RW_HEADS)], axis=1)
         for h in range(RW_HEADS)], axis=0)


def _rwkv_prep(z, z_prev, valid, mu_ref, w0_ref, w2_ref, a0_ref, a2_ref, g2_ref, kkw_ref, kaw_ref, hsum, tri_ref,
               chunk, prec):
    zs = z + mu_ref[...] * (z_prev - z)
    r = zs[:, 0:C_BR]
    k = zs[:, C_BR:2 * C_BR]
    v = zs[:, 2 * C_BR:3 * C_BR]
    lwla = zs[:, 3 * C_BR:3 * C_BR + RW_LW + RW_LA]
    lg = zs[:, 3 * C_BR + RW_LW + RW_LA:]
    w_log = -_softplus(-(w0_ref[...] + _mm(jnp.tanh(lwla), w2_ref[...]))) - 0.5
    ld = -jnp.exp(w_log)
    a = _sigmoid(a0_ref[...] + _mm(lwla, a2_ref[...]))
    g = _mm(_sigmoid(lg), g2_ref[...])
    kk = k * kkw_ref[...]
    kk = kk * lax.rsqrt(jnp.maximum(_dot(kk * kk, hsum, prec['headsum'], exact='b'), 1e-24))
    k2 = k * (1.0 + (a - 1.0) * kaw_ref[...])
    bv = kk * a
    if valid is not None:
        ld = jnp.where(valid, ld, 0.0)
        k2 = jnp.where(valid, k2, 0.0)
        v = jnp.where(valid, v, 0.0)
        bv = jnp.where(valid, bv, 0.0)
    t_tile = z.shape[0]
    n = tri_ref.shape[0]
    lc = jnp.concatenate([_dot(tri_ref[...], ld[i * n:(i + 1) * n], prec['cumsum'], exact='a')
                          for i in range(t_tile // n)], axis=0)
    lend = jnp.concatenate([jnp.broadcast_to(lc[(c + 1) * chunk - 1:(c + 1) * chunk], (chunk, C_BR))
                            for c in range(t_tile // chunk)], axis=0)
    e_end = jnp.exp(lend - lc)
    e_neg = jnp.exp(-lc)
    return dict(r=r, k2=k2, v=v, g=g, rt=r * jnp.exp(lc), kkt=kk * jnp.exp(lc - ld), kh=k2 * e_neg, bh=bv * e_neg,
                kw=k2 * e_end, bw=bv * e_end, wc=jnp.exp(lend))


def _rwkv_chunks_local(chunks, strict_ref, incl_ref, lvl_ref, chunk, prec):
    hc = RW_HEADS * chunk
    nk = RW_HEADS * RW_HEAD
    n = range(len(chunks))
    kkt, rt, kh, bh, vv, kw, bw = (list(x) for x in zip(*chunks))
    amat = [_dot(jnp.concatenate([kkt[c], rt[c]], axis=0), jnp.concatenate([kh[c], bh[c]], axis=0),
                 prec['amat'], _NT) for c in n]
    strict = strict_ref[...] != 0.0
    incl = incl_ref[...] != 0.0
    a_kk = [jnp.where(strict, amat[c][0:hc, 0:hc], 0.0) for c in n]
    a_kb = [jnp.where(strict, amat[c][0:hc, hc:2 * hc], 0.0) for c in n]
    a_rk = [jnp.where(incl, amat[c][hc:2 * hc, 0:hc], 0.0) for c in n]
    a_rb = [jnp.where(incl, amat[c][hc:2 * hc, hc:2 * hc], 0.0) for c in n]
    av = [_dot(jnp.concatenate([a_kk[c], a_rk[c]], axis=0), vv[c], prec['apply']) for c in n]
    ri = lax.broadcasted_iota(jnp.int32, (hc, hc), 0)
    cj = lax.broadcasted_iota(jnp.int32, (hc, hc), 1)
    eye = jnp.where(ri == cj, 1.0, 0.0)
    lvl0 = lvl_ref[0] != 0.0
    t_inv = [eye - jnp.where(lvl0, a_kb[c], 0.0) for c in n]
    for lv in range(1, lvl_ref.shape[0]):
        lvl = lvl_ref[lv] != 0.0
        half = [_dot(t_inv[c], jnp.where(lvl, a_kb[c], 0.0), prec['inv']) for c in n]
        t_inv = [t_inv[c] - _dot(half[c], t_inv[c], prec['inv']) for c in n]
    gu = [_dot(t_inv[c], jnp.concatenate([kkt[c], av[c][0:hc]], axis=1), prec['apply']) for c in n]
    pu = [_dot(a_rb[c], gu[c], prec['apply']) for c in n]
    mc = [_dot(gu[c][:, 0:nk], bw[c], prec['update'], _TN) for c in n]
    nn = [_dot(jnp.concatenate([vv[c], gu[c][:, nk:2 * nk]], axis=0), jnp.concatenate([kw[c], -bw[c]], axis=0),
               prec['update'], _TN) for c in n]
    return [(rt[c] - pu[c][:, 0:nk], av[c][hc:2 * hc] - pu[c][:, nk:2 * nk], mc[c], nn[c]) for c in n]


def _rwkv_chunks_direct(chunks, states, wcs, strict_ref, incl_ref, lvl_ref, chunk, prec):
    hc = RW_HEADS * chunk
    n = range(len(chunks))
    kkt, rt, kh, bh, vv, kw, bw = (list(x) for x in zip(*chunks))
    lhs = [jnp.concatenate([kkt[c], rt[c]], axis=0) for c in n]
    amat = [_dot(lhs[c], jnp.concatenate([kh[c], bh[c]], axis=0), prec['amat'], _NT) for c in n]
    ls = [_dot(lhs[c], states[c], prec['state'], _NT) for c in n]
    strict = strict_ref[...] != 0.0
    incl = incl_ref[...] != 0.0
    a_kk = [jnp.where(strict, amat[c][0:hc, 0:hc], 0.0) for c in n]
    a_kb = [jnp.where(strict, amat[c][0:hc, hc:2 * hc], 0.0) for c in n]
    a_rk = [jnp.where(incl, amat[c][hc:2 * hc, 0:hc], 0.0) for c in n]
    a_rb = [jnp.where(incl, amat[c][hc:2 * hc, hc:2 * hc], 0.0) for c in n]
    av = [_dot(jnp.concatenate([a_kk[c], a_rk[c]], axis=0), vv[c], prec['apply']) for c in n]
    ri = lax.broadcasted_iota(jnp.int32, (hc, hc), 0)
    cj = lax.broadcasted_iota(jnp.int32, (hc, hc), 1)
    eye = jnp.where(ri == cj, 1.0, 0.0)
    lvl0 = lvl_ref[0] != 0.0
    t_inv = [eye - jnp.where(lvl0, a_kb[c], 0.0) for c in n]
    for lv in range(1, lvl_ref.shape[0]):
        lvl = lvl_ref[lv] != 0.0
        half = [_dot(t_inv[c], jnp.where(lvl, a_kb[c], 0.0), prec['inv']) for c in n]
        t_inv = [t_inv[c] - _dot(half[c], t_inv[c], prec['inv']) for c in n]
    u = [_dot(t_inv[c], ls[c][0:hc] + av[c][0:hc], prec['apply']) for c in n]
    o = [ls[c][hc:2 * hc] + av[c][hc:2 * hc] - _dot(a_rb[c], u[c], prec['apply']) for c in n]
    s_new = [states[c] * wcs[c] + _dot(jnp.concatenate([vv[c], u[c]], axis=0),
                                       jnp.concatenate([kw[c], -bw[c]], axis=0), prec['update'], _TN) for c in n]
    out = []
    for c in n:
        o_c = o[c][0:chunk]
        for h in range(1, RW_HEADS):
            o_c = o_c + o[c][h * chunk:(h + 1) * chunk]
        out.append((o_c, s_new[c]))
    return out


def _rwkv_chunk_apply(s, local, wc, chunk, prec):
    p, o0, mc, nn = local
    o = _dot(p, s, prec['state'], _NT) + o0
    o_c = o[0:chunk]
    for h in range(1, RW_HEADS):
        o_c = o_c + o[h * chunk:(h + 1) * chunk]
    return o_c, s * wc - _dot(s, mc, prec['state']) + nn


def _rwkv_post(o, pre, rk_ref, gng_ref, gnb_ref, hsum, prec):
    inv_n = 1.0 / RW_HEAD
    o_mu = _dot(o, hsum, prec['headsum'], exact='b') * inv_n
    od = o - o_mu
    o_var = _dot(od * od, hsum, prec['headsum'], exact='b') * inv_n
    on = od * lax.rsqrt(o_var + RW_GN_EPS) * gng_ref[...] + gnb_ref[...]
    bonus = _dot(pre['r'] * pre['k2'] * rk_ref[...], hsum, prec['headsum'], exact='b') * pre['v']
    return (on + bonus) * pre['g']


_RWKV_LOCAL_KEYS = ('kkt', 'rt', 'kh', 'bh', 'v', 'kw', 'bw')


def _rwkv_kernel(z_ref, prev0_ref, wkv0_ref, mu_ref, w0_ref, w2_ref, a0_ref, a2_ref, g2_ref, kkw_ref, kaw_ref,
                 rk_ref, gng_ref, gnb_ref, hsum_ref, tri_ref, strict_ref, incl_ref, lvl_ref, *rest,
                 t_tile, chunk, t_valid, n_tiles, carry, prec):
    y_ref, wkv1_ref, s_scr, prev_scr = rest[-4:]
    i = pl.program_id(1)
    z = z_ref[...]
    row = lax.broadcasted_iota(jnp.int32, (t_tile, 1), 0)
    if carry:
        assert t_valid == t_tile * n_tiles

        @pl.when(i == 0)
        def _():
            s_scr[...] = _state_bd(wkv0_ref)
            prev_scr[...] = prev0_ref[...]

        z_prev = jnp.where(row == 0, prev_scr[...], pltpu.roll(z, 1, 0))
        prev_scr[...] = z[t_tile - 1:t_tile, :]
        valid = None
    else:
        step = row % chunk
        z_prev = jnp.where(step == 0, prev0_ref[...], pltpu.roll(z, 1, 0))
        valid = step < t_valid
    hsum = hsum_ref[...]
    pre = _rwkv_prep(z, z_prev, valid, mu_ref, w0_ref, w2_ref, a0_ref, a2_ref, g2_ref, kkw_ref, kaw_ref, hsum,
                     tri_ref, chunk, prec)
    lane_head = lax.broadcasted_iota(jnp.int32, (chunk, C_BR), 1) // RW_HEAD
    n_chunks = t_tile // chunk
    chunks = [tuple(_heads_bd(pre[key][c * chunk:(c + 1) * chunk], lane_head) for key in _RWKV_LOCAL_KEYS)
              for c in range(n_chunks)]
    wcs = [pre['wc'][c * chunk:c * chunk + 1] for c in range(n_chunks)]
    o_rows = []
    if carry:
        local = _rwkv_chunks_local(chunks, strict_ref, incl_ref, lvl_ref, chunk, prec)
        s = s_scr[...]
        for c in range(n_chunks):
            o_c, s = _rwkv_chunk_apply(s, local[c], wcs[c], chunk, prec)
            o_rows.append(o_c)
    else:
        states = [_state_bd(wkv0_ref.at[c]) for c in range(n_chunks)]
        for c, (o_c, s_c) in enumerate(_rwkv_chunks_direct(chunks, states, wcs, strict_ref, incl_ref, lvl_ref,
                                                           chunk, prec)):
            o_rows.append(o_c)
            for h in range(RW_HEADS):
                wkv1_ref[c, h] = s_c[h * RW_HEAD:(h + 1) * RW_HEAD, h * RW_HEAD:(h + 1) * RW_HEAD]
    y_ref[...] = _rwkv_post(jnp.concatenate(o_rows, axis=0), pre, rk_ref, gng_ref, gnb_ref, hsum, prec)

    if carry:
        s_scr[...] = s

        @pl.when(i == n_tiles - 1)
        def _():
            for h in range(RW_HEADS):
                wkv1_ref[h] = s[h * RW_HEAD:(h + 1) * RW_HEAD, h * RW_HEAD:(h + 1) * RW_HEAD]


def _rwkv(z3d, col_blk, prev0, wkv0, pw, layer, *, t_tile, chunk, t_valid, carry, prec, wkv_out=None):
    bsz, t_len, _ = z3d.shape
    n_tiles = t_len // t_tile
    cs = min(t_tile, max(chunk, 64))
    assert t_tile % cs == 0 and cs % chunk == 0
    idx = np.arange(cs)
    tri = jnp.asarray(((idx[:, None] // chunk == idx[None, :] // chunk)
                       & (idx[None, :] <= idx[:, None])).astype(np.float32))
    hid = np.arange(C_BR) // RW_HEAD
    hsum = jnp.asarray((hid[:, None] == hid[None, :]).astype(np.float32))
    hc = RW_HEADS * chunk
    hh, tt = np.arange(hc) // chunk, np.arange(hc) % chunk
    same_head = hh[:, None] == hh[None, :]
    strict = jnp.asarray((same_head & (tt[None, :] < tt[:, None])).astype(np.float32))
    incl = jnp.asarray((same_head & (tt[None, :] <= tt[:, None])).astype(np.float32))
    lvls = []
    m = 1
    while m < chunk:
        lvls.append(same_head & (tt[:, None] // (2 * m) == tt[None, :] // (2 * m))
                    & (tt[:, None] % (2 * m) >= m) & (tt[None, :] % (2 * m) < m))
        m *= 2
    lvl = jnp.asarray(np.stack(lvls).astype(np.float32))
    lb = lambda shape: _layer_block(shape, layer)
    vec = lambda n: lb((1, n))
    if carry:
        prev_spec = pl.BlockSpec((None, 1, RW_IN), lambda b, i: (b, 0, 0))
        wkv_spec = pl.BlockSpec((None, RW_HEADS, RW_HEAD, RW_HEAD), lambda b, i: (b, 0, 0, 0))
        extra_in, extra_specs = [], []
    else:
        assert n_tiles == 1
        prev_spec = pl.BlockSpec((None, t_tile, RW_IN), lambda b, i: (b, 0, 0))
        wkv_spec = pl.BlockSpec((t_tile // chunk, None, RW_HEADS, RW_HEAD, RW_HEAD),
                                lambda b, i: (b, layer, 0, 0, 0))
        extra_in, extra_specs = _state_buffer(wkv_out)
    operands = [z3d, prev0, wkv0, pw['mu'], pw['w0'], pw['w2'], pw['a0'], pw['a2'], pw['g2'], pw['kk'], pw['ka'],
                pw['rk'], pw['gn_g'], pw['gn_b'], hsum, tri, strict, incl, lvl] + extra_in
    aliases = {len(operands) - 1: 1} if extra_in else {}
    kern = functools.partial(_rwkv_kernel, t_tile=t_tile, chunk=chunk, t_valid=t_valid, n_tiles=n_tiles, carry=carry,
                             prec=prec)
    return pl.pallas_call(
        kern,
        grid=(bsz, n_tiles),
        in_specs=[pl.BlockSpec((None, t_tile, RW_IN), lambda b, i: (b, i, col_blk)), prev_spec, wkv_spec,
                  vec(RW_IN), vec(C_BR), lb((RW_LW + RW_LA, C_BR)), vec(C_BR), lb((RW_LW + RW_LA, C_BR)),
                  lb((RW_LG, C_BR)), vec(C_BR), vec(C_BR), vec(C_BR), vec(C_BR), vec(C_BR),
                  _full((C_BR, C_BR)), _full((cs, cs)), _full((hc, hc)), _full((hc, hc)), _full(lvl.shape)]
        + extra_specs,
        out_specs=[pl.BlockSpec((None, t_tile, C_BR), lambda b, i: (b, i, 0)), wkv_spec],
        out_shape=[jax.ShapeDtypeStruct((bsz, t_len, C_BR), F32), jax.ShapeDtypeStruct(wkv0.shape, F32)],
        input_output_aliases=aliases,
        scratch_shapes=[pltpu.VMEM((C_BR, C_BR), F32), pltpu.VMEM((1, RW_IN), F32)],
        compiler_params=_params(("parallel", "arbitrary")),
        name="rwkv7",
    )(*operands)


def _s5_kernel(u_ref, h0_ref, lbr_ref, lbi_ref, bb_ref, cc_ref, d_ref, gw_ref, gb_ref,
               y_ref, h1_ref, bu_scr, h_scr, *relayout_scr, bsz, t_tile, n_tiles, time_major):
    i = pl.program_id(0)

    @pl.when(i == 0)
    def _():
        h_scr[...] = h0_ref[...]

    rows = bsz * t_tile
    n_lt = S5_N // LANES
    n_ut = C_BR // LANES
    lane_tile = lambda ref, j: ref[:, j * LANES:(j + 1) * LANES]
    seq_rows = lambda t: pl.ds(t, bsz, stride=t_tile)
    step_rows = lambda t: pl.ds(pl.multiple_of(t * bsz, bsz), bsz)
    u = u_ref[...].reshape(rows, C_BR)
    if time_major:
        bt_scr, tm_scr = relayout_scr
        for j in range(n_ut):
            bt_scr[j] = lane_tile(u, j)

        def to_time_major(t, _):
            for j in range(n_ut):
                tm_scr[j, step_rows(t), :] = bt_scr[j, seq_rows(t), :]
            return 0

        lax.fori_loop(0, t_tile, to_time_major, 0, unroll=4)
        u = jnp.concatenate([tm_scr[j] for j in range(n_ut)], axis=1)
        sl_of = step_rows
    else:
        sl_of = seq_rows
    bu = _mm(u, bb_ref[...])
    for j in range(2 * n_lt):
        bu_scr[j] = lane_tile(bu, j)
    lbr = [jnp.broadcast_to(lane_tile(lbr_ref, j), (bsz, LANES)) for j in range(n_lt)]
    lbi = [jnp.broadcast_to(lane_tile(lbi_ref, j), (bsz, LANES)) for j in range(n_lt)]

    def step(t, carry):
        hr, hi = carry
        sl = sl_of(t)
        new_r, new_i = [], []
        for j in range(n_lt):
            nr = lbr[j] * hr[j] - lbi[j] * hi[j] + bu_scr[j, sl, :]
            ni = lbr[j] * hi[j] + lbi[j] * hr[j] + bu_scr[n_lt + j, sl, :]
            bu_scr[j, sl, :] = nr
            bu_scr[n_lt + j, sl, :] = ni
            new_r.append(nr)
            new_i.append(ni)
        return tuple(new_r), tuple(new_i)

    h_init = (tuple(lane_tile(h_scr, j) for j in range(n_lt)),
              tuple(lane_tile(h_scr, n_lt + j) for j in range(n_lt)))
    hr, hi = lax.fori_loop(0, t_tile, step, h_init)
    for j in range(n_lt):
        h_scr[:, j * LANES:(j + 1) * LANES] = hr[j]
        h_scr[:, (n_lt + j) * LANES:(n_lt + j + 1) * LANES] = hi[j]

    hs = jnp.concatenate([bu_scr[j] for j in range(2 * n_lt)], axis=1)
    y = _mm(hs, cc_ref[...]) + d_ref[...] * u
    y = _gelu_tanh(y)
    y = y * _sigmoid(_mm(y, gw_ref[...]) + gb_ref[...])
    if time_major:
        for j in range(n_ut):
            tm_scr[j] = lane_tile(y, j)

        def to_seq_major(t, _):
            for j in range(n_ut):
                bt_scr[j, seq_rows(t), :] = tm_scr[j, step_rows(t), :]
            return 0

        lax.fori_loop(0, t_tile, to_seq_major, 0, unroll=4)
        y = jnp.concatenate([bt_scr[j] for j in range(n_ut)], axis=1)
    y_ref[...] = y.reshape(y_ref.shape)

    @pl.when(i == n_tiles - 1)
    def _():
        h1_ref[...] = h_scr[...]


def _s5(z, col_blk, h0, ps, layer, *, bsz, t_len, t_tile):
    lb = lambda shape: _layer_block(shape, layer)
    n_tiles = t_len // t_tile
    rows = bsz * t_tile
    if z.ndim == 3:
        u_spec = pl.BlockSpec((bsz, t_tile, C_BR), lambda i: (0, i, col_blk))
        y_spec = pl.BlockSpec((bsz, t_tile, C_BR), lambda i: (0, i, 0))
        y_shape = (bsz, t_len, C_BR)
    else:
        assert n_tiles == 1
        u_spec = pl.BlockSpec((rows, C_BR), lambda i: (0, col_blk))
        y_spec = pl.BlockSpec((rows, C_BR), lambda i: (0, 0))
        y_shape = (rows, C_BR)
    time_major = z.ndim == 3 and bsz == 8
    relayout_scr = [pltpu.VMEM((C_BR // LANES, rows, LANES), F32)] * 2 if time_major else []
    kern = functools.partial(_s5_kernel, bsz=bsz, t_tile=t_tile, n_tiles=n_tiles, time_major=time_major)
    return pl.pallas_call(
        kern,
        grid=(n_tiles,),
        in_specs=[u_spec, _full((bsz, 2 * S5_N)), lb((1, S5_N)), lb((1, S5_N)),
                  lb((C_BR, 2 * S5_N)), lb((2 * S5_N, C_BR)), lb((1, C_BR)), lb((C_BR, C_BR)), lb((1, C_BR))],
        out_specs=[y_spec, _full((bsz, 2 * S5_N))],
        out_shape=[jax.ShapeDtypeStruct(y_shape, F32), jax.ShapeDtypeStruct((bsz, 2 * S5_N), F32)],
        scratch_shapes=[pltpu.VMEM((2 * S5_N // LANES, rows, LANES), F32), pltpu.VMEM((bsz, 2 * S5_N), F32)]
        + relayout_scr,
        compiler_params=_params(("arbitrary",)),
        name="s5",
    )(z, h0, ps['lb_re'], ps['lb_im'], ps['bb'], ps['cc'], ps['d'], ps['glu_w'], ps['glu_b'])


def _conv_taps(full_scr, w_ref, t_tile):
    lo = CONV_HIST_PAD - CONV_HIST
    sub = 8
    assert t_tile % sub == 0
    acc = None
    for rho in range(sub):
        offs = [o for o in range(rho, lo + CONV_W, sub) if o >= lo]
        rows = t_tile + (sub if rho else 0)
        part = None
        for o in offs:
            term = full_scr[o - rho:o - rho + rows, :] * w_ref[o - lo:o - lo + 1, :]
            part = term if part is None else part + term
        part = part[rho:rho + t_tile]
        acc = part if acc is None else acc + part
    return acc


def _conv_kernel(z_ref, c0_ref, w_ref, b_ref, g_ref, be_ref, *rest, t_tile, n_tiles):
    y_ref, c1_ref, full_scr = rest[-3:]
    i = pl.program_id(1)
    lo = CONV_HIST_PAD - CONV_HIST

    @pl.when(i == 0)
    def _():
        full_scr[0:lo, :] = jnp.zeros((lo, C_BR), F32)
        full_scr[lo:CONV_HIST_PAD, :] = c0_ref[...]

    z = z_ref[...]
    full_scr[CONV_HIST_PAD:CONV_HIST_PAD + t_tile, :] = z[:, 0:C_BR] * _sigmoid(z[:, C_BR:2 * C_BR])
    y = _layer_norm(_conv_taps(full_scr, w_ref, t_tile) + b_ref[...], g_ref[...], be_ref[...])
    y_ref[...] = y * _sigmoid(y)
    hist = full_scr[t_tile:t_tile + CONV_HIST_PAD, :]
    full_scr[0:CONV_HIST_PAD, :] = hist

    @pl.when(i == n_tiles - 1)
    def _():
        c1_ref[...] = hist[lo:, :]


def _conv(z3d, col_blk, conv0, conv_out, layer, pc, *, t_tile):
    bsz, t_len, _ = z3d.shape
    n_tiles = t_len // t_tile
    state_spec = pl.BlockSpec((None, None, CONV_HIST, C_BR), lambda b, i: (b, layer, 0, 0))
    extra_in, extra_specs = _state_buffer(conv_out)
    kern = functools.partial(_conv_kernel, t_tile=t_tile, n_tiles=n_tiles)
    return pl.pallas_call(
        kern,
        grid=(bsz, n_tiles),
        in_specs=[pl.BlockSpec((None, t_tile, 2 * C_BR), lambda b, i: (b, i, col_blk)), state_spec,
                  _layer_block((CONV_W, C_BR), layer), _layer_block((1, C_BR), layer),
                  _layer_block((1, C_BR), layer), _layer_block((1, C_BR), layer)] + extra_specs,
        out_specs=[pl.BlockSpec((None, t_tile, C_BR), lambda b, i: (b, i, 0)), state_spec],
        out_shape=[jax.ShapeDtypeStruct((bsz, t_len, C_BR), F32), jax.ShapeDtypeStruct(conv0.shape, F32)],
        input_output_aliases={6: 1},
        scratch_shapes=[pltpu.VMEM((CONV_HIST_PAD + t_tile, C_BR), F32)],
        compiler_params=_params(("parallel", "arbitrary")),
        name="conv",
    )(z3d, conv0, pc['w'], pc['b'], pc['ln_g'], pc['ln_b'], *extra_in)


def _conv_short_kernel(z_ref, c0_ref, w_ref, b_ref, g_ref, be_ref, *rest, bsz, t_len):
    y_ref, c1_ref, in_scr, out_scr = rest[-4:]
    n_lt = C_BR // LANES
    z = z_ref[...]
    c = z[:, 0:C_BR] * _sigmoid(z[:, C_BR:2 * C_BR])
    for j in range(n_lt):
        in_scr[j] = c[:, j * LANES:(j + 1) * LANES]
    step_rows = lambda t: pl.ds(t, bsz, stride=t_len)
    hist = lambda r: c0_ref[:, r * C_BR:(r + 1) * C_BR]
    new = [jnp.concatenate([in_scr[j, step_rows(t), :] for j in range(n_lt)], axis=1) for t in range(t_len)]
    full = lambda r: hist(r) if r < CONV_HIST else new[r - CONV_HIST]
    for t in range(t_len):
        acc = b_ref[...] + full(t) * w_ref[0:1, :]
        for j in range(1, CONV_W):
            acc = acc + full(t + j) * w_ref[j:j + 1, :]
        y = _layer_norm(acc, g_ref[...], be_ref[...])
        y = y * _sigmoid(y)
        for j in range(n_lt):
            out_scr[j, step_rows(t), :] = y[:, j * LANES:(j + 1) * LANES]
    y_ref[...] = jnp.concatenate([out_scr[j] for j in range(n_lt)], axis=1)
    for r in range(CONV_HIST):
        c1_ref[:, r * C_BR:(r + 1) * C_BR] = full(r + t_len)


def _conv_short(z2d, col_blk, conv0, conv_out, layer, pc, *, bsz, t_len):
    rows = bsz * t_len
    width = CONV_HIST * C_BR
    state_spec = pl.BlockSpec((bsz, width), lambda i: (0, layer))
    extra_in, extra_specs = _state_buffer(conv_out)
    kern = functools.partial(_conv_short_kernel, bsz=bsz, t_len=t_len)
    return pl.pallas_call(
        kern,
        grid=(1,),
        in_specs=[pl.BlockSpec((rows, 2 * C_BR), lambda i: (0, col_blk)), state_spec,
                  _layer_block((CONV_W, C_BR), layer), _layer_block((1, C_BR), layer),
                  _layer_block((1, C_BR), layer), _layer_block((1, C_BR), layer)] + extra_specs,
        out_specs=[pl.BlockSpec((rows, C_BR), lambda i: (0, 0)), state_spec],
        out_shape=[jax.ShapeDtypeStruct((rows, C_BR), F32), jax.ShapeDtypeStruct(conv0.shape, F32)],
        input_output_aliases={6: 1},
        scratch_shapes=[pltpu.VMEM((C_BR // LANES, rows, LANES), F32)] * 2,
        compiler_params=_params(("arbitrary",)),
        name="conv_short",
    )(z2d, conv0, pc['w'], pc['b'], pc['ln_g'], pc['ln_b'], *extra_in)


def _gmlp_kernel(z_ref, g_ref, b_ref, wm_ref, bias_ref, y_ref, *v_ref):
    z = z_ref[...]
    u = z[:, 0:C_BR]
    v = _layer_norm(z[:, C_BR:2 * C_BR], g_ref[...], b_ref[...])
    if v_ref:
        v_ref[0][...] = v
    vb = v.astype(MM_DTYPE)
    head = lax.broadcasted_iota(jnp.int32, (CHUNK, C_BR), 1) // GM_HEAD
    for c in range(z.shape[0] // CHUNK):
        rows = slice(c * CHUNK, (c + 1) * CHUNK)
        s = bias_ref[...]
        for h in range(GM_HEADS):
            s = s + jnp.where(head == h, jnp.dot(wm_ref[h], vb[rows], preferred_element_type=F32), 0.0)
        y_ref[rows, :] = u[rows] * s


def _gmlp(z3d, col_blk, pg, wm, bias, layer, *, emit_v):
    lb = lambda shape: _layer_block(shape, layer)
    bsz, t_len, _ = z3d.shape
    tile = min(GM_TILE, t_len)
    assert t_len % tile == 0 and tile % CHUNK == 0
    out_spec = pl.BlockSpec((None, tile, C_BR), lambda b, i: (b, i, 0))
    n_out = 2 if emit_v else 1
    return pl.pallas_call(
        _gmlp_kernel,
        grid=(bsz, t_len // tile),
        in_specs=[pl.BlockSpec((None, tile, 2 * C_BR), lambda b, i: (b, i, col_blk)),
                  lb((1, C_BR)), lb((1, C_BR)), lb((GM_HEADS, CHUNK, CHUNK)), lb((CHUNK, C_BR))],
        out_specs=[out_spec] * n_out,
        out_shape=[jax.ShapeDtypeStruct((bsz, t_len, C_BR), F32)] * n_out,
        compiler_params=_params(("parallel", "parallel")),
        name="gmlp",
    )(z3d, pg['ln_g'], pg['ln_b'], wm, bias)


def _merge_kernel(x_ref, yrw_ref, ys5_ref, ycv_ref, ygm_ref, wg_ref, wb_ref, wo_ref, g_ref, b_ref, o_ref):
    tm = x_ref.shape[0]
    halves = [slice(0, tm // 2), slice(tm // 2, tm)]
    x = [x_ref[h, :] for h in halves]
    xb = [v.astype(MM_DTYPE) for v in x]
    merged = [None for _ in halves]
    for bidx, y_ref in enumerate((yrw_ref, ys5_ref, ycv_ref, ygm_ref)):
        wg = wg_ref[:, bidx * D_MODEL:(bidx + 1) * D_MODEL]
        gate = [_sigmoid(jnp.dot(v, wg, preferred_element_type=F32)) for v in xb]
        term = [gate[i] * _mm(y_ref[h, :], wb_ref[bidx]) for i, h in enumerate(halves)]
        merged = [t if m is None else m + t for m, t in zip(merged, term)]
    proj = [_mm(m, wo_ref[...]) for m in merged]
    for i, h in enumerate(halves):
        o_ref[h, :] = _layer_norm(DN_ALPHA * x[i] + proj[i], g_ref[...], b_ref[...])


def _merge(x2d, ys, big, pm, layer):
    n = x2d.shape[0]
    tm = 512
    row = lambda w: pl.BlockSpec((tm, w), lambda i: (i, 0))
    return pl.pallas_call(
        _merge_kernel,
        grid=(n // tm,),
        in_specs=[row(D_MODEL), row(C_BR), row(C_BR), row(C_BR), row(C_BR),
                  _layer_block((D_MODEL, N_BRANCH * D_MODEL), layer), _layer_block((N_BRANCH, C_BR, D_MODEL), layer),
                  _layer_block((D_MODEL, D_MODEL), layer), _layer_block((1, D_MODEL), layer),
                  _layer_block((1, D_MODEL), layer)],
        out_specs=row(D_MODEL),
        out_shape=jax.ShapeDtypeStruct((n, D_MODEL), F32),
        compiler_params=_params(("parallel",)),
        name="merge",
    )(x2d, *ys, big['w_gate'], big['w_branch'], big['w_out'], pm['ln1_g'], pm['ln1_b'])


def _moe_kernel(x_ref, wg_ref, bg_ref, wu_ref, wd_ref, g_ref, b_ref, o_ref, hh_scr, *, tm):
    lane = lax.broadcasted_iota(jnp.int32, (tm, LANES), 1)
    x = x_ref[...]
    xb = x.astype(MM_DTYPE)
    logits = _dot(x, wg_ref[...], PREC['route']) + bg_ref[...]
    gl = jnp.where(lane < N_GROUPS, logits, NEG_BIG)
    gmax = jnp.max(gl, axis=-1, keepdims=True)
    g_sel = jnp.min(jnp.where(gl == gmax, lane, LANES), axis=-1, keepdims=True)
    p_group = 1.0 / jnp.sum(jnp.where(lane < N_GROUPS, jnp.exp(gl - gmax), 0.0), axis=-1, keepdims=True)
    first = N_GROUPS + g_sel * E_PER_GROUP
    el = jnp.where((lane >= first) & (lane < first + E_PER_GROUP), logits, NEG_BIG)
    m1 = jnp.max(el, axis=-1, keepdims=True)
    i1 = jnp.min(jnp.where(el == m1, lane, LANES), axis=-1, keepdims=True)
    el2 = jnp.where(lane == i1, NEG_BIG, el)
    m2 = jnp.max(el2, axis=-1, keepdims=True)
    i2 = jnp.min(jnp.where(el2 == m2, lane, LANES), axis=-1, keepdims=True)
    e2 = jnp.exp(m2 - m1)
    w1 = p_group / (1.0 + e2)
    w2 = p_group * e2 / (1.0 + e2)
    for e in range(N_EXPERTS):
        comb_e = jnp.where(i1 == e + N_GROUPS, w1, 0.0) + jnp.where(i2 == e + N_GROUPS, w2, 0.0)
        h = jnp.dot(xb, wu_ref[e], preferred_element_type=F32)
        h1 = h[:, 0:D_EXPERT]
        hh = h1 * _sigmoid(h1) * h[:, D_EXPERT:2 * D_EXPERT] * comb_e
        hh_scr[:, e * D_EXPERT:(e + 1) * D_EXPERT] = hh.astype(MM_DTYPE)
    moe = jnp.dot(hh_scr[...], wd_ref[...], preferred_element_type=F32)
    o_ref[...] = _layer_norm(DN_ALPHA * x + moe, g_ref[...], b_ref[...])


def _moe(x2d, big, pe, layer):
    n = x2d.shape[0]
    tm = 512
    kern = functools.partial(_moe_kernel, tm=tm)
    resident = lambda shape: _layer_block(shape, layer, pipeline_mode=pl.Buffered(1))
    return pl.pallas_call(
        kern,
        grid=(n // tm,),
        in_specs=[pl.BlockSpec((tm, D_MODEL), lambda i: (i, 0)),
                  _layer_block((D_MODEL, LANES), layer), _layer_block((1, LANES), layer),
                  resident((N_EXPERTS, D_MODEL, 2 * D_EXPERT)), resident((N_EXPERTS * D_EXPERT, D_MODEL)),
                  _layer_block((1, D_MODEL), layer), _layer_block((1, D_MODEL), layer)],
        out_specs=pl.BlockSpec((tm, D_MODEL), lambda i: (i, 0)),
        out_shape=jax.ShapeDtypeStruct((n, D_MODEL), F32),
        scratch_shapes=[pltpu.VMEM((tm, N_EXPERTS * D_EXPERT), MM_DTYPE)],
        compiler_params=_params(("parallel",)),
        name="moe",
    )(x2d, pe['wg'], pe['bg'], big['w_up'], big['w_down'], pe['ln2_g'], pe['ln2_b'])


def _block_diag(blocks):
    nl, g, m, n = blocks.shape
    eye = jnp.eye(g, dtype=blocks.dtype)
    return (eye[None, :, None, :, None] * blocks[:, :, :, None, :]).reshape(nl, g * m, g * n)


def _prep_params(p, t_short):
    nl = p['w_in'].shape[0]
    row = lambda a: a.reshape(nl, 1, -1).astype(F32)
    w_in = p['w_in']
    big = dict(w_mix=jnp.concatenate([w_in[..., :OFF_S5], w_in[..., OFF_CV:OFF_GM], w_in[..., OFF_GM:OFF_GATE],
                                      w_in[..., OFF_S5:OFF_CV]], axis=-1).astype(MM_DTYPE),
               w_gate=w_in[..., OFF_GATE:].astype(MM_DTYPE),
               w_branch=p['w_branch'].astype(MM_DTYPE), w_out=p['w_out'].astype(MM_DTYPE),
               w_up=p['moe_w_up'].astype(MM_DTYPE),
               w_down=p['moe_w_down'].astype(MM_DTYPE).reshape(nl, N_EXPERTS * D_EXPERT, D_MODEL))
    zeros_lora = jnp.zeros((nl, RW_LW, C_BR), F32)
    rw = dict(mu=row(p['rw_mu']), w0=row(p['rw_w0']),
              w2=jnp.concatenate([p['rw_w2'], zeros_lora], axis=1).astype(MM_DTYPE), a0=row(p['rw_a0']),
              a2=jnp.concatenate([zeros_lora, p['rw_a2']], axis=1).astype(MM_DTYPE),
              g2=p['rw_g2'].astype(MM_DTYPE), kk=row(p['rw_kk']), ka=row(p['rw_ka']), rk=row(p['rw_rk']),
              gn_g=row(p['rw_gn_g']), gn_b=row(p['rw_gn_b']))
    lr, li = p['s5_lam_re'].astype(F32), p['s5_lam_im'].astype(F32)
    dt = jnp.exp(p['s5_log_dt'].astype(F32))[..., None]
    mag = jnp.exp(lr * dt)
    lb_re, lb_im = mag * jnp.cos(li * dt), mag * jnp.sin(li * dt)
    den = lr * lr + li * li
    q_re = ((lb_re - 1.0) * lr + lb_im * li) / den
    q_im = (lb_im * lr - (lb_re - 1.0) * li) / den
    br, bi = p['s5_b_re'].astype(F32), p['s5_b_im'].astype(F32)
    bb_re = q_re[..., None] * br - q_im[..., None] * bi
    bb_im = q_re[..., None] * bi + q_im[..., None] * br
    t23 = lambda a: jnp.swapaxes(a, 2, 3)
    bb = jnp.concatenate([_block_diag(t23(bb_re)), _block_diag(t23(bb_im))], axis=2).astype(MM_DTYPE)
    cc = jnp.concatenate([_block_diag(t23(p['s5_c_re'].astype(F32))), -_block_diag(t23(p['s5_c_im'].astype(F32)))],
                         axis=1).astype(MM_DTYPE)
    s5 = dict(lb_re=row(lb_re), lb_im=row(lb_im), bb=bb, cc=cc, d=row(p['s5_d']),
              glu_w=p['s5_glu_w'].astype(MM_DTYPE), glu_b=row(p['s5_glu_b']))
    cv = dict(w=p['cv_w'].astype(F32), b=row(p['cv_b']), ln_g=row(p['cv_ln_g']), ln_b=row(p['cv_ln_b']))
    causal = jnp.tril(jnp.ones((CHUNK, CHUNK), dtype=bool))
    wm = jnp.where(causal, p['gm_ws'], 0).astype(F32)
    bias = jnp.repeat(jnp.swapaxes(p['gm_bs'], 1, 2), GM_HEAD, axis=2).astype(F32)
    reps = CHUNK // t_short
    wm_short = jnp.einsum('rs,lhij->lhrisj', jnp.eye(reps, dtype=F32),
                          wm[:, :, :t_short, :t_short]).reshape(nl, GM_HEADS, CHUNK, CHUNK)
    gm = dict(ln_g=row(p['gm_ln_g']), ln_b=row(p['gm_ln_b']), wm=wm.astype(MM_DTYPE), bias=bias,
              wm_short=wm_short.astype(MM_DTYPE), bias_short=jnp.tile(bias[:, :t_short], (1, reps, 1)))
    mg = dict(ln1_g=row(p['ln1_g']), ln1_b=row(p['ln1_b']))
    pad = LANES - N_GROUPS - N_EXPERTS
    wg = jnp.concatenate([p['moe_wg1'], p['moe_wg2'], jnp.zeros((nl, D_MODEL, pad), F32)], axis=2).astype(F32)
    bg = row(jnp.concatenate([p['moe_bg1'], p['moe_bg2'], jnp.zeros((nl, pad), F32)], axis=1))
    moe = dict(wg=wg, bg=bg, ln2_g=row(p['ln2_g']), ln2_b=row(p['ln2_b']))
    return dict(big=big, rw=rw, s5=s5, cv=cv, gm=gm, mg=mg, moe=moe)


def _gmlp_group(z3d, pg, layer, *, is_prompt):
    bsz, t_len, n_cols = z3d.shape
    if is_prompt:
        y, = _gmlp(z3d, P_GM // (2 * C_BR), pg, pg['wm'], pg['bias'], layer, emit_v=False)
        return y, None
    y, v = _gmlp(z3d.reshape(1, bsz * t_len, n_cols), P_GM // (2 * C_BR), pg, pg['wm_short'], pg['bias_short'],
                 layer, emit_v=True)
    return y.reshape(bsz, t_len, C_BR), v.reshape(bsz, t_len, C_BR)


def _run_group(x, wkv0, shift0, s5r0, s5i0, conv0, pr, *, is_prompt):
    bsz, t_len, _ = x.shape
    big = pr['big']
    n = bsz * t_len
    x2d = x.reshape(n, D_MODEL)
    outs = []
    conv_shape = conv0.shape
    if not is_prompt:
        conv0 = conv0.reshape(bsz, DEPTH * CONV_HIST * C_BR)
    conv_buf = jnp.zeros_like(conv0)
    if not is_prompt:
        wkv_buf = jnp.zeros_like(wkv0)
        grp_rows = lambda a: a.reshape((bsz // RW_GROUP, RW_GROUP * a.shape[1]) + a.shape[2:])
    for l in range(DEPTH):
        z2d = _inproj(x2d, big['w_mix'], l)
        z3d = z2d.reshape(bsz, t_len, N_MIX)
        sh0 = shift0[:, l].reshape(bsz, 1, RW_IN)
        if is_prompt:
            y_rw, wkv1 = _rwkv(z3d, P_RW // RW_IN, sh0, wkv0[:, l], pr['rw'], l, t_tile=512, chunk=64,
                               t_valid=t_len, carry=True, prec=PREC)
        else:
            z_rw = grp_rows(jnp.pad(z3d[:, :, P_RW:P_RW + RW_IN], ((0, 0), (0, RW_PAD - t_len), (0, 0))))
            prev0 = grp_rows(jnp.pad(sh0, ((0, 0), (0, RW_PAD - 1), (0, 0))))
            y_rw, wkv_buf = _rwkv(z_rw, 0, prev0, wkv0, pr['rw'], l, t_tile=RW_GROUP * RW_PAD, chunk=RW_PAD,
                                  t_valid=t_len, carry=False, prec=PREC_SHORT, wkv_out=wkv_buf)
            y_rw = y_rw.reshape(bsz, RW_PAD, C_BR)[:, :t_len]
            wkv1 = None
        shift1 = z3d[:, t_len - 1, P_RW:P_RW + RW_IN]
        h0 = jnp.concatenate([s5r0[:, l].reshape(bsz, S5_N), s5i0[:, l].reshape(bsz, S5_N)], axis=1)
        if is_prompt:
            y_s5, h1 = _s5(z3d, P_S5 // C_BR, h0, pr['s5'], l, bsz=bsz, t_len=t_len, t_tile=128)
        else:
            y_s5, h1 = _s5(z2d, P_S5 // C_BR, h0, pr['s5'], l, bsz=bsz, t_len=t_len, t_tile=t_len)
        s5r1 = h1[:, :S5_N].reshape(bsz, S5_GROUPS, S5_STATE)
        s5i1 = h1[:, S5_N:].reshape(bsz, S5_GROUPS, S5_STATE)
        if is_prompt:
            y_cv, conv_buf = _conv(z3d, P_CV // (2 * C_BR), conv0, conv_buf, l, pr['cv'], t_tile=512)
        else:
            y_cv, conv_buf = _conv_short(z2d, P_CV // (2 * C_BR), conv0, conv_buf, l, pr['cv'], bsz=bsz,
                                         t_len=t_len)
        y_gm, v_gm = _gmlp_group(z3d, pr['gm'], l, is_prompt=is_prompt)
        ys = [y.reshape(n, C_BR) for y in (y_rw, y_s5, y_cv, y_gm)]
        x2d = _merge(x2d, ys, big, pr['mg'], l)
        x2d = _moe(x2d, big, pr['moe'], l)
        outs.append((wkv1, shift1, s5r1, s5i1, v_gm))
    stack = lambda i: None if outs[0][i] is None else jnp.stack([o[i] for o in outs], axis=1)
    wkv = stack(0) if is_prompt else wkv_buf
    return (x2d.reshape(bsz, t_len, D_MODEL),
            (wkv, stack(1), stack(2), stack(3), conv_buf.reshape(conv_shape), stack(4)))


def kernel(x_prompt, x_sample, state_rwkv_wkv, state_rwkv_shift, state_s5_re, state_s5_im, cache_conv,
           w_in, rw_mu, rw_w0, rw_w2, rw_a0, rw_a2, rw_g2, rw_kk, rw_ka, rw_rk, rw_gn_g, rw_gn_b,
           s5_lam_re, s5_lam_im, s5_log_dt, s5_b_re, s5_b_im, s5_c_re, s5_c_im, s5_d, s5_glu_w, s5_glu_b,
           cv_w, cv_b, cv_ln_g, cv_ln_b, gm_ln_g, gm_ln_b, gm_ws, gm_bs,
           w_branch, w_out, ln1_g, ln1_b,
           moe_wg1, moe_bg1, moe_wg2, moe_bg2, moe_w_up, moe_w_down, ln2_g, ln2_b):
    p = dict(w_in=w_in, rw_mu=rw_mu, rw_w0=rw_w0, rw_w2=rw_w2, rw_a0=rw_a0, rw_a2=rw_a2, rw_g2=rw_g2,
             rw_kk=rw_kk, rw_ka=rw_ka, rw_rk=rw_rk, rw_gn_g=rw_gn_g, rw_gn_b=rw_gn_b,
             s5_lam_re=s5_lam_re, s5_lam_im=s5_lam_im, s5_log_dt=s5_log_dt, s5_b_re=s5_b_re, s5_b_im=s5_b_im,
             s5_c_re=s5_c_re, s5_c_im=s5_c_im, s5_d=s5_d, s5_glu_w=s5_glu_w, s5_glu_b=s5_glu_b,
             cv_w=cv_w, cv_b=cv_b, cv_ln_g=cv_ln_g, cv_ln_b=cv_ln_b, gm_ln_g=gm_ln_g, gm_ln_b=gm_ln_b,
             gm_ws=gm_ws, gm_bs=gm_bs, w_branch=w_branch, w_out=w_out, ln1_g=ln1_g, ln1_b=ln1_b,
             moe_wg1=moe_wg1, moe_bg1=moe_bg1, moe_wg2=moe_wg2, moe_bg2=moe_bg2, moe_w_up=moe_w_up,
             moe_w_down=moe_w_down, ln2_g=ln2_g, ln2_b=ln2_b)
    pr = _prep_params(p, x_sample.shape[1])
    bp = x_prompt.shape[0]
    dt = x_prompt.dtype
    y_prompt, (p_wkv, p_shift, p_s5r, p_s5i, p_conv, _) = _run_group(
        x_prompt,
        jnp.zeros((bp, DEPTH, RW_HEADS, RW_HEAD, RW_HEAD), dt),
        jnp.zeros((bp, DEPTH, RW_IN), dt),
        jnp.zeros((bp, DEPTH, S5_GROUPS, S5_STATE), dt),
        jnp.zeros((bp, DEPTH, S5_GROUPS, S5_STATE), dt),
        jnp.zeros((bp, DEPTH, CONV_HIST, C_BR), dt),
        pr, is_prompt=True)
    y_sample, (s_wkv, s_shift, s_s5r, s_s5i, s_conv, s_gmv) = _run_group(
        x_sample, state_rwkv_wkv, state_rwkv_shift, state_s5_re, state_s5_im, cache_conv, pr, is_prompt=False)
    return (y_prompt, y_sample, p_wkv, p_shift, p_s5r, p_s5i, p_conv,
            s_wkv, s_shift, s_s5r, s_s5i, s_conv, s_gmv)
```

```python
import functools
import math

import numpy as np
import jax
import jax.numpy as jnp
from jax import lax
from jax.experimental import pallas as pl
from jax.experimental.pallas import tpu as pltpu

D_MODEL = 1024
DEPTH = 4
N_BRANCH = 4
C_BR = D_MODEL // 4
RW_HEAD = 64
RW_HEADS = C_BR // RW_HEAD
RW_LW = 64
RW_LA = 64
RW_LG = 128
RW_IN = 3 * C_BR + RW_LW + RW_LA + RW_LG
RW_GN_EPS = 64e-5
RW_PAD = 8
RW_GROUP = 16
S5_GW = 16
S5_GROUPS = C_BR // S5_GW
S5_STATE = 64
S5_N = S5_GROUPS * S5_STATE
CONV_W = 31
CONV_HIST = CONV_W - 1
CONV_HIST_PAD = 32
CHUNK = 128
GM_TILE = 8 * CHUNK
GM_HEADS = 4
GM_HEAD = C_BR // GM_HEADS
N_GROUPS = 4
E_PER_GROUP = 4
N_EXPERTS = N_GROUPS * E_PER_GROUP
D_EXPERT = D_MODEL // 4
LN_EPS = 1e-5
DN_ALPHA = (2 * DEPTH) ** 0.25
OFF_S5 = RW_IN
OFF_CV = OFF_S5 + C_BR
OFF_GM = OFF_CV + 2 * C_BR
OFF_GATE = OFF_GM + 2 * C_BR
P_RW = 0
P_CV = P_RW + RW_IN
P_GM = P_CV + 2 * C_BR
P_S5 = P_GM + 2 * C_BR
N_MIX = P_S5 + C_BR

LANES = 128
SUBLANES = 8
TOKEN_TILE = 512
RW_TILE = 512
RW_CHUNK = 64
S5_TILE = 128
CV_TILE = 512
VMEM_LIMIT = 56 * 1024 * 1024

F32 = jnp.float32
BF16 = jnp.bfloat16
MM_DTYPE = jnp.bfloat16
NEG_BIG = -1e30


def _mm(a, b):
    return jnp.dot(a.astype(MM_DTYPE), b.astype(MM_DTYPE), preferred_element_type=F32)


def _split_bf16(a):
    hi = a.astype(BF16)
    return hi, (a - hi.astype(F32)).astype(BF16)


_NN = ((1,), (0,))
_NT = ((1,), (1,))
_TN = ((0,), (0,))


def _dot(a, b, mode, dims=_NN, exact=None):
    dn = (dims, ((), ()))
    f = lambda x, y: lax.dot_general(x, y, dn, preferred_element_type=F32)
    if mode == 'bf16':
        return f(a.astype(BF16), b.astype(BF16))
    assert mode == 'x3'
    if exact == 'a':
        b_hi, b_lo = _split_bf16(b)
        a = a.astype(BF16)
        return f(a, b_hi) + f(a, b_lo)
    if exact == 'b':
        a_hi, a_lo = _split_bf16(a)
        b = b.astype(BF16)
        return f(a_hi, b) + f(a_lo, b)
    a_hi, a_lo = _split_bf16(a)
    b_hi, b_lo = _split_bf16(b)
    return f(a_hi, b_hi) + (f(a_hi, b_lo) + f(a_lo, b_hi))


PREC = dict(cumsum='x3', headsum='x3', amat='bf16', inv='bf16', state='bf16', apply='bf16', update='bf16', route='x3')
PREC_SHORT = dict(PREC, update='x3')


def _sigmoid(x):
    return jax.nn.sigmoid(x)


def _softplus(x):
    return jnp.maximum(x, 0.0) + jnp.log1p(jnp.exp(-jnp.abs(x)))


def _gelu_tanh(x):
    return 0.5 * x * (1.0 + jnp.tanh(math.sqrt(2.0 / math.pi) * (x + 0.044715 * (x * x * x))))


def _layer_norm(x, g, b):
    mu = jnp.mean(x, axis=-1, keepdims=True)
    d = x - mu
    var = jnp.mean(d * d, axis=-1, keepdims=True)
    return d * lax.rsqrt(var + LN_EPS) * g + b


def _params(sem):
    return pltpu.CompilerParams(dimension_semantics=sem, vmem_limit_bytes=VMEM_LIMIT)


def _full(shape):
    nd = len(shape)
    return pl.BlockSpec(shape, lambda *_: (0,) * nd)


def _state_buffer(buf):
    return [buf], [pl.BlockSpec(memory_space=pl.ANY)]


def _layer_block(shape, layer, **kwargs):
    nd = len(shape)
    return pl.BlockSpec((None,) + tuple(shape), lambda *_: (layer,) + (0,) * nd, **kwargs)


def _inproj_kernel(x_ref, w_ref, z_ref):
    z_ref[...] = _mm(x_ref[...], w_ref[...])


def _inproj(x2d, w_bf16, layer):
    n = x2d.shape[0]
    tm = min(TOKEN_TILE, n)
    return pl.pallas_call(
        _inproj_kernel,
        grid=(n // tm,),
        in_specs=[pl.BlockSpec((tm, D_MODEL), lambda i: (i, 0)), _layer_block((D_MODEL, N_MIX), layer)],
        out_specs=pl.BlockSpec((tm, N_MIX), lambda i: (i, 0)),
        out_shape=jax.ShapeDtypeStruct((n, N_MIX), F32),
        compiler_params=_params(("parallel",)),
        name="inproj",
    )(x2d, w_bf16)


def _heads_bd(x, lane_head):
    return jnp.concatenate([jnp.where(lane_head == h, x, 0.0) for h in range(RW_HEADS)], axis=0)


def _state_bd(wkv):
    zeros_blk = jnp.zeros((RW_HEAD, RW_HEAD), F32)
    return jnp.concatenate(
        [jnp.concatenate([wkv[h] if g == h else zeros_blk for g in range(RW_HEADS)], axis=1)
         for h in range(RW_HEADS)], axis=0)


def _rwkv_prep(z, z_prev, valid, mu_ref, w0_ref, w2_ref, a0_ref, a2_ref, g2_ref, kkw_ref, kaw_ref, hsum, tri_ref,
               chunk, prec):
    zs = z + mu_ref[...] * (z_prev - z)
    r = zs[:, 0:C_BR]
    k = zs[:, C_BR:2 * C_BR]
    v = zs[:, 2 * C_BR:3 * C_BR]
    lwla = zs[:, 3 * C_BR:3 * C_BR + RW_LW + RW_LA]
    lg = zs[:, 3 * C_BR + RW_LW + RW_LA:]
    w_log = -_softplus(-(w0_ref[...] + _mm(jnp.tanh(lwla), w2_ref[...]))) - 0.5
    ld = -jnp.exp(w_log)
    a = _sigmoid(a0_ref[...] + _mm(lwla, a2_ref[...]))
    g = _mm(_sigmoid(lg), g2_ref[...])
    kk = k * kkw_ref[...]
    kk = kk * lax.rsqrt(jnp.maximum(_dot(kk * kk, hsum, prec['headsum'], exact='b'), 1e-24))
    k2 = k * (1.0 + (a - 1.0) * kaw_ref[...])
    bv = kk * a
    if valid is not None:
        ld = jnp.where(valid, ld, 0.0)
        k2 = jnp.where(valid, k2, 0.0)
        v = jnp.where(valid, v, 0.0)
        bv = jnp.where(valid, bv, 0.0)
    t_tile = z.shape[0]
    n = tri_ref.shape[0]
    lc = jnp.concatenate([_dot(tri_ref[...], ld[i * n:(i + 1) * n], prec['cumsum'], exact='a')
                          for i in range(t_tile // n)], axis=0)
    lend = jnp.concatenate([jnp.broadcast_to(lc[(c + 1) * chunk - 1:(c + 1) * chunk], (chunk, C_BR))
                            for c in range(t_tile // chunk)], axis=0)
    e_end = jnp.exp(lend - lc)
    e_neg = jnp.exp(-lc)
    return dict(r=r, k2=k2, v=v, g=g, rt=r * jnp.exp(lc), kkt=kk * jnp.exp(lc - ld), kh=k2 * e_neg, bh=bv * e_neg,
                kw=k2 * e_end, bw=bv * e_end, wc=jnp.exp(lend))


def _rwkv_chunks_local(chunks, strict_ref, incl_ref, lvl_ref, chunk, prec):
    hc = RW_HEADS * chunk
    nk = RW_HEADS * RW_HEAD
    n = range(len(chunks))
    kkt, rt, kh, bh, vv, kw, bw = (list(x) for x in zip(*chunks))
    amat = [_dot(jnp.concatenate([kkt[c], rt[c]], axis=0), jnp.concatenate([kh[c], bh[c]], axis=0),
                 prec['amat'], _NT) for c in n]
    strict = strict_ref[...] != 0.0
    incl = incl_ref[...] != 0.0
    a_kk = [jnp.where(strict, amat[c][0:hc, 0:hc], 0.0) for c in n]
    a_kb = [jnp.where(strict, amat[c][0:hc, hc:2 * hc], 0.0) for c in n]
    a_rk = [jnp.where(incl, amat[c][hc:2 * hc, 0:hc], 0.0) for c in n]
    a_rb = [jnp.where(incl, amat[c][hc:2 * hc, hc:2 * hc], 0.0) for c in n]
    av = [_dot(jnp.concatenate([a_kk[c], a_rk[c]], axis=0), vv[c], prec['apply']) for c in n]
    ri = lax.broadcasted_iota(jnp.int32, (hc, hc), 0)
    cj = lax.broadcasted_iota(jnp.int32, (hc, hc), 1)
    eye = jnp.where(ri == cj, 1.0, 0.0)
    lvl0 = lvl_ref[0] != 0.0
    t_inv = [eye - jnp.where(lvl0, a_kb[c], 0.0) for c in n]
    for lv in range(1, lvl_ref.shape[0]):
        lvl = lvl_ref[lv] != 0.0
        half = [_dot(t_inv[c], jnp.where(lvl, a_kb[c], 0.0), prec['inv']) for c in n]
        t_inv = [t_inv[c] - _dot(half[c], t_inv[c], prec['inv']) for c in n]
    gu = [_dot(t_inv[c], jnp.concatenate([kkt[c], av[c][0:hc]], axis=1), prec['apply']) for c in n]
    pu = [_dot(a_rb[c], gu[c], prec['apply']) for c in n]
    mc = [_dot(gu[c][:, 0:nk], bw[c], prec['update'], _TN) for c in n]
    nn = [_dot(jnp.concatenate([vv[c], gu[c][:, nk:2 * nk]], axis=0), jnp.concatenate([kw[c], -bw[c]], axis=0),
               prec['update'], _TN) for c in n]
    return [(rt[c] - pu[c][:, 0:nk], av[c][hc:2 * hc] - pu[c][:, nk:2 * nk], mc[c], nn[c]) for c in n]


def _rwkv_chunks_direct(chunks, states, wcs, strict_ref, incl_ref, lvl_ref, chunk, prec):
    hc = RW_HEADS * chunk
    n = range(len(chunks))
    kkt, rt, kh, bh, vv, kw, bw = (list(x) for x in zip(*chunks))
    lhs = [jnp.concatenate([kkt[c], rt[c]], axis=0) for c in n]
    amat = [_dot(lhs[c], jnp.concatenate([kh[c], bh[c]], axis=0), prec['amat'], _NT) for c in n]
    ls = [_dot(lhs[c], states[c], prec['state'], _NT) for c in n]
    strict = strict_ref[...] != 0.0
    incl = incl_ref[...] != 0.0
    a_kk = [jnp.where(strict, amat[c][0:hc, 0:hc], 0.0) for c in n]
    a_kb = [jnp.where(strict, amat[c][0:hc, hc:2 * hc], 0.0) for c in n]
    a_rk = [jnp.where(incl, amat[c][hc:2 * hc, 0:hc], 0.0) for c in n]
    a_rb = [jnp.where(incl, amat[c][hc:2 * hc, hc:2 * hc], 0.0) for c in n]
    av = [_dot(jnp.concatenate([a_kk[c], a_rk[c]], axis=0), vv[c], prec['apply']) for c in n]
    ri = lax.broadcasted_iota(jnp.int32, (hc, hc), 0)
    cj = lax.broadcasted_iota(jnp.int32, (hc, hc), 1)
    eye = jnp.where(ri == cj, 1.0, 0.0)
    lvl0 = lvl_ref[0] != 0.0
    t_inv = [eye - jnp.where(lvl0, a_kb[c], 0.0) for c in n]
    for lv in range(1, lvl_ref.shape[0]):
        lvl = lvl_ref[lv] != 0.0
        half = [_dot(t_inv[c], jnp.where(lvl, a_kb[c], 0.0), prec['inv']) for c in n]
        t_inv = [t_inv[c] - _dot(half[c], t_inv[c], prec['inv']) for c in n]
    u = [_dot(t_inv[c], ls[c][0:hc] + av[c][0:hc], prec['apply']) for c in n]
    o = [ls[c][hc:2 * hc] + av[c][hc:2 * hc] - _dot(a_rb[c], u[c], prec['apply']) for c in n]
    s_new = [states[c] * wcs[c] + _dot(jnp.concatenate([vv[c], u[c]], axis=0),
                                       jnp.concatenate([kw[c], -bw[c]], axis=0), prec['update'], _TN) for c in n]
    out = []
    for c in n:
        o_c = o[c][0:chunk]
        for h in range(1, RW_HEADS):
            o_c = o_c + o[c][h * chunk:(h + 1) * chunk]
        out.append((o_c, s_new[c]))
    return out


def _rwkv_chunk_apply(s, local, wc, chunk, prec):
    p, o0, mc, nn = local
    o = _dot(p, s, prec['state'], _NT) + o0
    o_c = o[0:chunk]
    for h in range(1, RW_HEADS):
        o_c = o_c + o[h * chunk:(h + 1) * chunk]
    return o_c, s * wc - _dot(s, mc, prec['state']) + nn


def _rwkv_post(o, pre, rk_ref, gng_ref, gnb_ref, hsum, prec):
    inv_n = 1.0 / RW_HEAD
    o_mu = _dot(o, hsum, prec['headsum'], exact='b') * inv_n
    od = o - o_mu
    o_var = _dot(od * od, hsum, prec['headsum'], exact='b') * inv_n
    on = od * lax.rsqrt(o_var + RW_GN_EPS) * gng_ref[...] + gnb_ref[...]
    bonus = _dot(pre['r'] * pre['k2'] * rk_ref[...], hsum, prec['headsum'], exact='b') * pre['v']
    return (on + bonus) * pre['g']


_RWKV_LOCAL_KEYS = ('kkt', 'rt', 'kh', 'bh', 'v', 'kw', 'bw')


def _rwkv_kernel(z_ref, prev0_ref, wkv0_ref, mu_ref, w0_ref, w2_ref, a0_ref, a2_ref, g2_ref, kkw_ref, kaw_ref,
                 rk_ref, gng_ref, gnb_ref, hsum_ref, tri_ref, strict_ref, incl_ref, lvl_ref, *rest,
                 t_tile, chunk, t_valid, n_tiles, carry, prec):
    y_ref, wkv1_ref, s_scr, prev_scr = rest[-4:]
    i = pl.program_id(1)
    z = z_ref[...]
    row = lax.broadcasted_iota(jnp.int32, (t_tile, 1), 0)
    if carry:
        assert t_valid == t_tile * n_tiles

        @pl.when(i == 0)
        def _():
            s_scr[...] = _state_bd(wkv0_ref)
            prev_scr[...] = prev0_ref[...]

        z_prev = jnp.where(row == 0, prev_scr[...], pltpu.roll(z, 1, 0))
        prev_scr[...] = z[t_tile - 1:t_tile, :]
        valid = None
    else:
        step = row % chunk
        z_prev = jnp.where(step == 0, prev0_ref[...], pltpu.roll(z, 1, 0))
        valid = step < t_valid
    hsum = hsum_ref[...]
    pre = _rwkv_prep(z, z_prev, valid, mu_ref, w0_ref, w2_ref, a0_ref, a2_ref, g2_ref, kkw_ref, kaw_ref, hsum,
                     tri_ref, chunk, prec)
    lane_head = lax.broadcasted_iota(jnp.int32, (chunk, C_BR), 1) // RW_HEAD
    n_chunks = t_tile // chunk
    chunks = [tuple(_heads_bd(pre[key][c * chunk:(c + 1) * chunk], lane_head) for key in _RWKV_LOCAL_KEYS)
              for c in range(n_chunks)]
    wcs = [pre['wc'][c * chunk:c * chunk + 1] for c in range(n_chunks)]
    o_rows = []
    if carry:
        local = _rwkv_chunks_local(chunks, strict_ref, incl_ref, lvl_ref, chunk, prec)
        s = s_scr[...]
        for c in range(n_chunks):
            o_c, s = _rwkv_chunk_apply(s, local[c], wcs[c], chunk, prec)
            o_rows.append(o_c)
    else:
        states = [_state_bd(wkv0_ref.at[c]) for c in range(n_chunks)]
        for c, (o_c, s_c) in enumerate(_rwkv_chunks_direct(chunks, states, wcs, strict_ref, incl_ref, lvl_ref,
                                                           chunk, prec)):
            o_rows.append(o_c)
            for h in range(RW_HEADS):
                wkv1_ref[c, h] = s_c[h * RW_HEAD:(h + 1) * RW_HEAD, h * RW_HEAD:(h + 1) * RW_HEAD]
    y_ref[...] = _rwkv_post(jnp.concatenate(o_rows, axis=0), pre, rk_ref, gng_ref, gnb_ref, hsum, prec)

    if carry:
        s_scr[...] = s

        @pl.when(i == n_tiles - 1)
        def _():
            for h in range(RW_HEADS):
                wkv1_ref[h] = s[h * RW_HEAD:(h + 1) * RW_HEAD, h * RW_HEAD:(h + 1) * RW_HEAD]


def _rwkv(z3d, col_blk, prev0, wkv0, pw, layer, *, t_tile, chunk, t_valid, carry, prec, wkv_out=None):
    bsz, t_len, _ = z3d.shape
    n_tiles = t_len // t_tile
    cs = min(t_tile, max(chunk, 64))
    assert t_tile % cs == 0 and cs % chunk == 0
    idx = np.arange(cs)
    tri = jnp.asarray(((idx[:, None] // chunk == idx[None, :] // chunk)
                       & (idx[None, :] <= idx[:, None])).astype(np.float32))
    hid = np.arange(C_BR) // RW_HEAD
    hsum = jnp.asarray((hid[:, None] == hid[None, :]).astype(np.float32))
    hc = RW_HEADS * chunk
    hh, tt = np.arange(hc) // chunk, np.arange(hc) % chunk
    same_head = hh[:, None] == hh[None, :]
    strict = jnp.asarray((same_head & (tt[None, :] < tt[:, None])).astype(np.float32))
    incl = jnp.asarray((same_head & (tt[None, :] <= tt[:, None])).astype(np.float32))
    lvls = []
    m = 1
    while m < chunk:
        lvls.append(same_head & (tt[:, None] // (2 * m) == tt[None, :] // (2 * m))
                    & (tt[:, None] % (2 * m) >= m) & (tt[None, :] % (2 * m) < m))
        m *= 2
    lvl = jnp.asarray(np.stack(lvls).astype(np.float32))
    lb = lambda shape: _layer_block(shape, layer)
    vec = lambda n: lb((1, n))
    if carry:
        prev_spec = pl.BlockSpec((None, 1, RW_IN), lambda b, i: (b, 0, 0))
        wkv_spec = pl.BlockSpec((None, RW_HEADS, RW_HEAD, RW_HEAD), lambda b, i: (b, 0, 0, 0))
        extra_in, extra_specs = [], []
    else:
        assert n_tiles == 1
        prev_spec = pl.BlockSpec((None, t_tile, RW_IN), lambda b, i: (b, 0, 0))
        wkv_spec = pl.BlockSpec((t_tile // chunk, None, RW_HEADS, RW_HEAD, RW_HEAD),
                                lambda b, i: (b, layer, 0, 0, 0))
        extra_in, extra_specs = _state_buffer(wkv_out)
    operands = [z3d, prev0, wkv0, pw['mu'], pw['w0'], pw['w2'], pw['a0'], pw['a2'], pw['g2'], pw['kk'], pw['ka'],
                pw['rk'], pw['gn_g'], pw['gn_b'], hsum, tri, strict, incl, lvl] + extra_in
    aliases = {len(operands) - 1: 1} if extra_in else {}
    kern = functools.partial(_rwkv_kernel, t_tile=t_tile, chunk=chunk, t_valid=t_valid, n_tiles=n_tiles, carry=carry,
                             prec=prec)
    return pl.pallas_call(
        kern,
        grid=(bsz, n_tiles),
        in_specs=[pl.BlockSpec((None, t_tile, RW_IN), lambda b, i: (b, i, col_blk)), prev_spec, wkv_spec,
                  vec(RW_IN), vec(C_BR), lb((RW_LW + RW_LA, C_BR)), vec(C_BR), lb((RW_LW + RW_LA, C_BR)),
                  lb((RW_LG, C_BR)), vec(C_BR), vec(C_BR), vec(C_BR), vec(C_BR), vec(C_BR),
                  _full((C_BR, C_BR)), _full((cs, cs)), _full((hc, hc)), _full((hc, hc)), _full(lvl.shape)]
        + extra_specs,
        out_specs=[pl.BlockSpec((None, t_tile, C_BR), lambda b, i: (b, i, 0)), wkv_spec],
        out_shape=[jax.ShapeDtypeStruct((bsz, t_len, C_BR), F32), jax.ShapeDtypeStruct(wkv0.shape, F32)],
        input_output_aliases=aliases,
        scratch_shapes=[pltpu.VMEM((C_BR, C_BR), F32), pltpu.VMEM((1, RW_IN), F32)],
        compiler_params=_params(("parallel", "arbitrary")),
        name="rwkv7",
    )(*operands)


def _s5_kernel(u_ref, h0_ref, lbr_ref, lbi_ref, bb_ref, cc_ref, d_ref, gw_ref, gb_ref,
               y_ref, h1_ref, bu_scr, h_scr, *relayout_scr, bsz, t_tile, n_tiles, time_major):
    i = pl.program_id(0)

    @pl.when(i == 0)
    def _():
        h_scr[...] = h0_ref[...]

    rows = bsz * t_tile
    n_lt = S5_N // LANES
    n_ut = C_BR // LANES
    lane_tile = lambda ref, j: ref[:, j * LANES:(j + 1) * LANES]
    seq_rows = lambda t: pl.ds(t, bsz, stride=t_tile)
    step_rows = lambda t: pl.ds(pl.multiple_of(t * bsz, bsz), bsz)
    u = u_ref[...].reshape(rows, C_BR)
    if time_major:
        bt_scr, tm_scr = relayout_scr
        for j in range(n_ut):
            bt_scr[j] = lane_tile(u, j)

        def to_time_major(t, _):
            for j in range(n_ut):
                tm_scr[j, step_rows(t), :] = bt_scr[j, seq_rows(t), :]
            return 0

        lax.fori_loop(0, t_tile, to_time_major, 0, unroll=4)
        u = jnp.concatenate([tm_scr[j] for j in range(n_ut)], axis=1)
        sl_of = step_rows
    else:
        sl_of = seq_rows
    bu = _mm(u, bb_ref[...])
    for j in range(2 * n_lt):
        bu_scr[j] = lane_tile(bu, j)
    lbr = [jnp.broadcast_to(lane_tile(lbr_ref, j), (bsz, LANES)) for j in range(n_lt)]
    lbi = [jnp.broadcast_to(lane_tile(lbi_ref, j), (bsz, LANES)) for j in range(n_lt)]

    def step(t, carry):
        hr, hi = carry
        sl = sl_of(t)
        new_r, new_i = [], []
        for j in range(n_lt):
            nr = lbr[j] * hr[j] - lbi[j] * hi[j] + bu_scr[j, sl, :]
            ni = lbr[j] * hi[j] + lbi[j] * hr[j] + bu_scr[n_lt + j, sl, :]
            bu_scr[j, sl, :] = nr
            bu_scr[n_lt + j, sl, :] = ni
            new_r.append(nr)
            new_i.append(ni)
        return tuple(new_r), tuple(new_i)

    h_init = (tuple(lane_tile(h_scr, j) for j in range(n_lt)),
              tuple(lane_tile(h_scr, n_lt + j) for j in range(n_lt)))
    hr, hi = lax.fori_loop(0, t_tile, step, h_init)
    for j in range(n_lt):
        h_scr[:, j * LANES:(j + 1) * LANES] = hr[j]
        h_scr[:, (n_lt + j) * LANES:(n_lt + j + 1) * LANES] = hi[j]

    hs = jnp.concatenate([bu_scr[j] for j in range(2 * n_lt)], axis=1)
    y = _mm(hs, cc_ref[...]) + d_ref[...] * u
    y = _gelu_tanh(y)
    y = y * _sigmoid(_mm(y, gw_ref[...]) + gb_ref[...])
    if time_major:
        for j in range(n_ut):
            tm_scr[j] = lane_tile(y, j)

        def to_seq_major(t, _):
            for j in range(n_ut):
                bt_scr[j, seq_rows(t), :] = tm_scr[j, step_rows(t), :]
            return 0

        lax.fori_loop(0, t_tile, to_seq_major, 0, unroll=4)
        y = jnp.concatenate([bt_scr[j] for j in range(n_ut)], axis=1)
    y_ref[...] = y.reshape(y_ref.shape)

    @pl.when(i == n_tiles - 1)
    def _():
        h1_ref[...] = h_scr[...]


def _s5(z, col_blk, h0, ps, layer, *, bsz, t_len, t_tile):
    lb = lambda shape: _layer_block(shape, layer)
    n_tiles = t_len // t_tile
    rows = bsz * t_tile
    if z.ndim == 3:
        u_spec = pl.BlockSpec((bsz, t_tile, C_BR), lambda i: (0, i, col_blk))
        y_spec = pl.BlockSpec((bsz, t_tile, C_BR), lambda i: (0, i, 0))
        y_shape = (bsz, t_len, C_BR)
    else:
        assert n_tiles == 1
        u_spec = pl.BlockSpec((rows, C_BR), lambda i: (0, col_blk))
        y_spec = pl.BlockSpec((rows, C_BR), lambda i: (0, 0))
        y_shape = (rows, C_BR)
    time_major = z.ndim == 3 and bsz == SUBLANES
    relayout_scr = [pltpu.VMEM((C_BR // LANES, rows, LANES), F32)] * 2 if time_major else []
    kern = functools.partial(_s5_kernel, bsz=bsz, t_tile=t_tile, n_tiles=n_tiles, time_major=time_major)
    return pl.pallas_call(
        kern,
        grid=(n_tiles,),
        in_specs=[u_spec, _full((bsz, 2 * S5_N)), lb((1, S5_N)), lb((1, S5_N)),
                  lb((C_BR, 2 * S5_N)), lb((2 * S5_N, C_BR)), lb((1, C_BR)), lb((C_BR, C_BR)), lb((1, C_BR))],
        out_specs=[y_spec, _full((bsz, 2 * S5_N))],
        out_shape=[jax.ShapeDtypeStruct(y_shape, F32), jax.ShapeDtypeStruct((bsz, 2 * S5_N), F32)],
        scratch_shapes=[pltpu.VMEM((2 * S5_N // LANES, rows, LANES), F32), pltpu.VMEM((bsz, 2 * S5_N), F32)]
        + relayout_scr,
        compiler_params=_params(("arbitrary",)),
        name="s5",
    )(z, h0, ps['lb_re'], ps['lb_im'], ps['bb'], ps['cc'], ps['d'], ps['glu_w'], ps['glu_b'])


def _conv_taps(full_scr, w_ref, t_tile):
    lo = CONV_HIST_PAD - CONV_HIST
    sub = SUBLANES
    assert t_tile % sub == 0
    acc = None
    for rho in range(sub):
        offs = [o for o in range(rho, lo + CONV_W, sub) if o >= lo]
        rows = t_tile + (sub if rho else 0)
        part = None
        for o in offs:
            term = full_scr[o - rho:o - rho + rows, :] * w_ref[o - lo:o - lo + 1, :]
            part = term if part is None else part + term
        part = part[rho:rho + t_tile]
        acc = part if acc is None else acc + part
    return acc


def _conv_kernel(z_ref, c0_ref, w_ref, b_ref, g_ref, be_ref, *rest, t_tile, n_tiles):
    y_ref, c1_ref, full_scr = rest[-3:]
    i = pl.program_id(1)
    lo = CONV_HIST_PAD - CONV_HIST

    @pl.when(i == 0)
    def _():
        full_scr[0:lo, :] = jnp.zeros((lo, C_BR), F32)
        full_scr[lo:CONV_HIST_PAD, :] = c0_ref[...]

    z = z_ref[...]
    full_scr[CONV_HIST_PAD:CONV_HIST_PAD + t_tile, :] = z[:, 0:C_BR] * _sigmoid(z[:, C_BR:2 * C_BR])
    y = _layer_norm(_conv_taps(full_scr, w_ref, t_tile) + b_ref[...], g_ref[...], be_ref[...])
    y_ref[...] = y * _sigmoid(y)
    hist = full_scr[t_tile:t_tile + CONV_HIST_PAD, :]
    full_scr[0:CONV_HIST_PAD, :] = hist

    @pl.when(i == n_tiles - 1)
    def _():
        c1_ref[...] = hist[lo:, :]


def _conv(z3d, col_blk, conv0, conv_out, layer, pc, *, t_tile):
    bsz, t_len, _ = z3d.shape
    n_tiles = t_len // t_tile
    state_spec = pl.BlockSpec((None, None, CONV_HIST, C_BR), lambda b, i: (b, layer, 0, 0))
    extra_in, extra_specs = _state_buffer(conv_out)
    kern = functools.partial(_conv_kernel, t_tile=t_tile, n_tiles=n_tiles)
    return pl.pallas_call(
        kern,
        grid=(bsz, n_tiles),
        in_specs=[pl.BlockSpec((None, t_tile, 2 * C_BR), lambda b, i: (b, i, col_blk)), state_spec,
                  _layer_block((CONV_W, C_BR), layer), _layer_block((1, C_BR), layer),
                  _layer_block((1, C_BR), layer), _layer_block((1, C_BR), layer)] + extra_specs,
        out_specs=[pl.BlockSpec((None, t_tile, C_BR), lambda b, i: (b, i, 0)), state_spec],
        out_shape=[jax.ShapeDtypeStruct((bsz, t_len, C_BR), F32), jax.ShapeDtypeStruct(conv0.shape, F32)],
        input_output_aliases={6: 1},
        scratch_shapes=[pltpu.VMEM((CONV_HIST_PAD + t_tile, C_BR), F32)],
        compiler_params=_params(("parallel", "arbitrary")),
        name="conv",
    )(z3d, conv0, pc['w'], pc['b'], pc['ln_g'], pc['ln_b'], *extra_in)


def _conv_short_kernel(z_ref, c0_ref, w_ref, b_ref, g_ref, be_ref, *rest, bsz, t_len):
    y_ref, c1_ref, in_scr, out_scr = rest[-4:]
    n_lt = C_BR // LANES
    z = z_ref[...]
    c = z[:, 0:C_BR] * _sigmoid(z[:, C_BR:2 * C_BR])
    for j in range(n_lt):
        in_scr[j] = c[:, j * LANES:(j + 1) * LANES]
    step_rows = lambda t: pl.ds(t, bsz, stride=t_len)
    hist = lambda r: c0_ref[:, r * C_BR:(r + 1) * C_BR]
    new = [jnp.concatenate([in_scr[j, step_rows(t), :] for j in range(n_lt)], axis=1) for t in range(t_len)]
    full = lambda r: hist(r) if r < CONV_HIST else new[r - CONV_HIST]
    for t in range(t_len):
        acc = b_ref[...] + full(t) * w_ref[0:1, :]
        for j in range(1, CONV_W):
            acc = acc + full(t + j) * w_ref[j:j + 1, :]
        y = _layer_norm(acc, g_ref[...], be_ref[...])
        y = y * _sigmoid(y)
        for j in range(n_lt):
            out_scr[j, step_rows(t), :] = y[:, j * LANES:(j + 1) * LANES]
    y_ref[...] = jnp.concatenate([out_scr[j] for j in range(n_lt)], axis=1)
    for r in range(CONV_HIST):
        c1_ref[:, r * C_BR:(r + 1) * C_BR] = full(r + t_len)


def _conv_short(z2d, col_blk, conv0, conv_out, layer, pc, *, bsz, t_len):
    rows = bsz * t_len
    width = CONV_HIST * C_BR
    state_spec = pl.BlockSpec((bsz, width), lambda i: (0, layer))
    extra_in, extra_specs = _state_buffer(conv_out)
    kern = functools.partial(_conv_short_kernel, bsz=bsz, t_len=t_len)
    return pl.pallas_call(
        kern,
        grid=(1,),
        in_specs=[pl.BlockSpec((rows, 2 * C_BR), lambda i: (0, col_blk)), state_spec,
                  _layer_block((CONV_W, C_BR), layer), _layer_block((1, C_BR), layer),
                  _layer_block((1, C_BR), layer), _layer_block((1, C_BR), layer)] + extra_specs,
        out_specs=[pl.BlockSpec((rows, C_BR), lambda i: (0, 0)), state_spec],
        out_shape=[jax.ShapeDtypeStruct((rows, C_BR), F32), jax.ShapeDtypeStruct(conv0.shape, F32)],
        input_output_aliases={6: 1},
        scratch_shapes=[pltpu.VMEM((C_BR // LANES, rows, LANES), F32)] * 2,
        compiler_params=_params(("arbitrary",)),
        name="conv_short",
    )(z2d, conv0, pc['w'], pc['b'], pc['ln_g'], pc['ln_b'], *extra_in)


def _gmlp_kernel(z_ref, g_ref, b_ref, wm_ref, bias_ref, y_ref, *v_ref):
    z = z_ref[...]
    u = z[:, 0:C_BR]
    v = _layer_norm(z[:, C_BR:2 * C_BR], g_ref[...], b_ref[...])
    if v_ref:
        v_ref[0][...] = v
    vb = v.astype(MM_DTYPE)
    head = lax.broadcasted_iota(jnp.int32, (CHUNK, C_BR), 1) // GM_HEAD
    for c in range(z.shape[0] // CHUNK):
        rows = slice(c * CHUNK, (c + 1) * CHUNK)
        s = bias_ref[...]
        for h in range(GM_HEADS):
            s = s + jnp.where(head == h, jnp.dot(wm_ref[h], vb[rows], preferred_element_type=F32), 0.0)
        y_ref[rows, :] = u[rows] * s


def _gmlp(z3d, col_blk, pg, wm, bias, layer, *, emit_v):
    lb = lambda shape: _layer_block(shape, layer)
    bsz, t_len, _ = z3d.shape
    tile = min(GM_TILE, t_len)
    assert t_len % tile == 0 and tile % CHUNK == 0
    out_spec = pl.BlockSpec((None, tile, C_BR), lambda b, i: (b, i, 0))
    n_out = 2 if emit_v else 1
    return pl.pallas_call(
        _gmlp_kernel,
        grid=(bsz, t_len // tile),
        in_specs=[pl.BlockSpec((None, tile, 2 * C_BR), lambda b, i: (b, i, col_blk)),
                  lb((1, C_BR)), lb((1, C_BR)), lb((GM_HEADS, CHUNK, CHUNK)), lb((CHUNK, C_BR))],
        out_specs=[out_spec] * n_out,
        out_shape=[jax.ShapeDtypeStruct((bsz, t_len, C_BR), F32)] * n_out,
        compiler_params=_params(("parallel", "parallel")),
        name="gmlp",
    )(z3d, pg['ln_g'], pg['ln_b'], wm, bias)


def _merge_kernel(x_ref, yrw_ref, ys5_ref, ycv_ref, ygm_ref, wg_ref, wb_ref, wo_ref, g_ref, b_ref, o_ref):
    tm = x_ref.shape[0]
    halves = [slice(0, tm // 2), slice(tm // 2, tm)]
    x = [x_ref[h, :] for h in halves]
    xb = [v.astype(MM_DTYPE) for v in x]
    merged = [None for _ in halves]
    for bidx, y_ref in enumerate((yrw_ref, ys5_ref, ycv_ref, ygm_ref)):
        wg = wg_ref[:, bidx * D_MODEL:(bidx + 1) * D_MODEL]
        gate = [_sigmoid(jnp.dot(v, wg, preferred_element_type=F32)) for v in xb]
        term = [gate[i] * _mm(y_ref[h, :], wb_ref[bidx]) for i, h in enumerate(halves)]
        merged = [t if m is None else m + t for m, t in zip(merged, term)]
    proj = [_mm(m, wo_ref[...]) for m in merged]
    for i, h in enumerate(halves):
        o_ref[h, :] = _layer_norm(DN_ALPHA * x[i] + proj[i], g_ref[...], b_ref[...])


def _merge(x2d, ys, big, pm, layer):
    n = x2d.shape[0]
    tm = min(TOKEN_TILE, n)
    row = lambda w: pl.BlockSpec((tm, w), lambda i: (i, 0))
    return pl.pallas_call(
        _merge_kernel,
        grid=(n // tm,),
        in_specs=[row(D_MODEL), row(C_BR), row(C_BR), row(C_BR), row(C_BR),
                  _layer_block((D_MODEL, N_BRANCH * D_MODEL), layer), _layer_block((N_BRANCH, C_BR, D_MODEL), layer),
                  _layer_block((D_MODEL, D_MODEL), layer), _layer_block((1, D_MODEL), layer),
                  _layer_block((1, D_MODEL), layer)],
        out_specs=row(D_MODEL),
        out_shape=jax.ShapeDtypeStruct((n, D_MODEL), F32),
        compiler_params=_params(("parallel",)),
        name="merge",
    )(x2d, *ys, big['w_gate'], big['w_branch'], big['w_out'], pm['ln1_g'], pm['ln1_b'])


def _moe_kernel(x_ref, wg_ref, bg_ref, wu_ref, wd_ref, g_ref, b_ref, o_ref, hh_scr, *, tm):
    lane = lax.broadcasted_iota(jnp.int32, (tm, LANES), 1)
    x = x_ref[...]
    xb = x.astype(MM_DTYPE)
    logits = _dot(x, wg_ref[...], PREC['route']) + bg_ref[...]
    gl = jnp.where(lane < N_GROUPS, logits, NEG_BIG)
    gmax = jnp.max(gl, axis=-1, keepdims=True)
    g_sel = jnp.min(jnp.where(gl == gmax, lane, LANES), axis=-1, keepdims=True)
    p_group = 1.0 / jnp.sum(jnp.where(lane < N_GROUPS, jnp.exp(gl - gmax), 0.0), axis=-1, keepdims=True)
    first = N_GROUPS + g_sel * E_PER_GROUP
    el = jnp.where((lane >= first) & (lane < first + E_PER_GROUP), logits, NEG_BIG)
    m1 = jnp.max(el, axis=-1, keepdims=True)
    i1 = jnp.min(jnp.where(el == m1, lane, LANES), axis=-1, keepdims=True)
    el2 = jnp.where(lane == i1, NEG_BIG, el)
    m2 = jnp.max(el2, axis=-1, keepdims=True)
    i2 = jnp.min(jnp.where(el2 == m2, lane, LANES), axis=-1, keepdims=True)
    e2 = jnp.exp(m2 - m1)
    w1 = p_group / (1.0 + e2)
    w2 = p_group * e2 / (1.0 + e2)
    for e in range(N_EXPERTS):
        comb_e = jnp.where(i1 == e + N_GROUPS, w1, 0.0) + jnp.where(i2 == e + N_GROUPS, w2, 0.0)
        h = jnp.dot(xb, wu_ref[e], preferred_element_type=F32)
        h1 = h[:, 0:D_EXPERT]
        hh = h1 * _sigmoid(h1) * h[:, D_EXPERT:2 * D_EXPERT] * comb_e
        hh_scr[:, e * D_EXPERT:(e + 1) * D_EXPERT] = hh.astype(MM_DTYPE)
    moe = jnp.dot(hh_scr[...], wd_ref[...], preferred_element_type=F32)
    o_ref[...] = _layer_norm(DN_ALPHA * x + moe, g_ref[...], b_ref[...])


def _moe(x2d, big, pe, layer):
    n = x2d.shape[0]
    tm = min(TOKEN_TILE, n)
    kern = functools.partial(_moe_kernel, tm=tm)
    resident = lambda shape: _layer_block(shape, layer, pipeline_mode=pl.Buffered(1))
    return pl.pallas_call(
        kern,
        grid=(n // tm,),
        in_specs=[pl.BlockSpec((tm, D_MODEL), lambda i: (i, 0)),
                  _layer_block((D_MODEL, LANES), layer), _layer_block((1, LANES), layer),
                  resident((N_EXPERTS, D_MODEL, 2 * D_EXPERT)), resident((N_EXPERTS * D_EXPERT, D_MODEL)),
                  _layer_block((1, D_MODEL), layer), _layer_block((1, D_MODEL), layer)],
        out_specs=pl.BlockSpec((tm, D_MODEL), lambda i: (i, 0)),
        out_shape=jax.ShapeDtypeStruct((n, D_MODEL), F32),
        scratch_shapes=[pltpu.VMEM((tm, N_EXPERTS * D_EXPERT), MM_DTYPE)],
        compiler_params=_params(("parallel",)),
        name="moe",
    )(x2d, pe['wg'], pe['bg'], big['w_up'], big['w_down'], pe['ln2_g'], pe['ln2_b'])


def _block_diag(blocks):
    nl, g, m, n = blocks.shape
    eye = jnp.eye(g, dtype=blocks.dtype)
    return (eye[None, :, None, :, None] * blocks[:, :, :, None, :]).reshape(nl, g * m, g * n)


def _prep_params(p, t_short):
    nl = p['w_in'].shape[0]
    row = lambda a: a.reshape(nl, 1, -1).astype(F32)
    w_in = p['w_in']
    big = dict(w_mix=jnp.concatenate([w_in[..., :OFF_S5], w_in[..., OFF_CV:OFF_GM], w_in[..., OFF_GM:OFF_GATE],
                                      w_in[..., OFF_S5:OFF_CV]], axis=-1).astype(MM_DTYPE),
               w_gate=w_in[..., OFF_GATE:].astype(MM_DTYPE),
               w_branch=p['w_branch'].astype(MM_DTYPE), w_out=p['w_out'].astype(MM_DTYPE),
               w_up=p['moe_w_up'].astype(MM_DTYPE),
               w_down=p['moe_w_down'].astype(MM_DTYPE).reshape(nl, N_EXPERTS * D_EXPERT, D_MODEL))
    zeros_lora = jnp.zeros((nl, RW_LW, C_BR), F32)
    rw = dict(mu=row(p['rw_mu']), w0=row(p['rw_w0']),
              w2=jnp.concatenate([p['rw_w2'], zeros_lora], axis=1).astype(MM_DTYPE), a0=row(p['rw_a0']),
              a2=jnp.concatenate([zeros_lora, p['rw_a2']], axis=1).astype(MM_DTYPE),
              g2=p['rw_g2'].astype(MM_DTYPE), kk=row(p['rw_kk']), ka=row(p['rw_ka']), rk=row(p['rw_rk']),
              gn_g=row(p['rw_gn_g']), gn_b=row(p['rw_gn_b']))
    lr, li = p['s5_lam_re'].astype(F32), p['s5_lam_im'].astype(F32)
    dt = jnp.exp(p['s5_log_dt'].astype(F32))[..., None]
    mag = jnp.exp(lr * dt)
    lb_re, lb_im = mag * jnp.cos(li * dt), mag * jnp.sin(li * dt)
    den = lr * lr + li * li
    q_re = ((lb_re - 1.0) * lr + lb_im * li) / den
    q_im = (lb_im * lr - (lb_re - 1.0) * li) / den
    br, bi = p['s5_b_re'].astype(F32), p['s5_b_im'].astype(F32)
    bb_re = q_re[..., None] * br - q_im[..., None] * bi
    bb_im = q_re[..., None] * bi + q_im[..., None] * br
    t23 = lambda a: jnp.swapaxes(a, 2, 3)
    bb = jnp.concatenate([_block_diag(t23(bb_re)), _block_diag(t23(bb_im))], axis=2).astype(MM_DTYPE)
    cc = jnp.concatenate([_block_diag(t23(p['s5_c_re'].astype(F32))), -_block_diag(t23(p['s5_c_im'].astype(F32)))],
                         axis=1).astype(MM_DTYPE)
    s5 = dict(lb_re=row(lb_re), lb_im=row(lb_im), bb=bb, cc=cc, d=row(p['s5_d']),
              glu_w=p['s5_glu_w'].astype(MM_DTYPE), glu_b=row(p['s5_glu_b']))
    cv = dict(w=p['cv_w'].astype(F32), b=row(p['cv_b']), ln_g=row(p['cv_ln_g']), ln_b=row(p['cv_ln_b']))
    causal = jnp.tril(jnp.ones((CHUNK, CHUNK), dtype=bool))
    wm = jnp.where(causal, p['gm_ws'], 0).astype(F32)
    bias = jnp.repeat(jnp.swapaxes(p['gm_bs'], 1, 2), GM_HEAD, axis=2).astype(F32)
    reps = CHUNK // t_short
    wm_short = jnp.einsum('rs,lhij->lhrisj', jnp.eye(reps, dtype=F32),
                          wm[:, :, :t_short, :t_short]).reshape(nl, GM_HEADS, CHUNK, CHUNK)
    gm = dict(ln_g=row(p['gm_ln_g']), ln_b=row(p['gm_ln_b']), wm=wm.astype(MM_DTYPE), bias=bias,
              wm_short=wm_short.astype(MM_DTYPE), bias_short=jnp.tile(bias[:, :t_short], (1, reps, 1)))
    mg = dict(ln1_g=row(p['ln1_g']), ln1_b=row(p['ln1_b']))
    pad = LANES - N_GROUPS - N_EXPERTS
    wg = jnp.concatenate([p['moe_wg1'], p['moe_wg2'], jnp.zeros((nl, D_MODEL, pad), F32)], axis=2).astype(F32)
    bg = row(jnp.concatenate([p['moe_bg1'], p['moe_bg2'], jnp.zeros((nl, pad), F32)], axis=1))
    moe = dict(wg=wg, bg=bg, ln2_g=row(p['ln2_g']), ln2_b=row(p['ln2_b']))
    return dict(big=big, rw=rw, s5=s5, cv=cv, gm=gm, mg=mg, moe=moe)


def _gmlp_group(z3d, pg, layer, *, is_prompt):
    bsz, t_len, n_cols = z3d.shape
    if is_prompt:
        y, = _gmlp(z3d, P_GM // (2 * C_BR), pg, pg['wm'], pg['bias'], layer, emit_v=False)
        return y, None
    y, v = _gmlp(z3d.reshape(1, bsz * t_len, n_cols), P_GM // (2 * C_BR), pg, pg['wm_short'], pg['bias_short'],
                 layer, emit_v=True)
    return y.reshape(bsz, t_len, C_BR), v.reshape(bsz, t_len, C_BR)


def _run_group(x, wkv0, shift0, s5r0, s5i0, conv0, pr, *, is_prompt):
    bsz, t_len, _ = x.shape
    big = pr['big']
    n = bsz * t_len
    x2d = x.reshape(n, D_MODEL)
    outs = []
    conv_shape = conv0.shape
    if not is_prompt:
        conv0 = conv0.reshape(bsz, DEPTH * CONV_HIST * C_BR)
    conv_buf = jnp.zeros_like(conv0)
    if not is_prompt:
        wkv_buf = jnp.zeros_like(wkv0)
        grp_rows = lambda a: a.reshape((bsz // RW_GROUP, RW_GROUP * a.shape[1]) + a.shape[2:])
    for l in range(DEPTH):
        z2d = _inproj(x2d, big['w_mix'], l)
        z3d = z2d.reshape(bsz, t_len, N_MIX)
        sh0 = shift0[:, l].reshape(bsz, 1, RW_IN)
        if is_prompt:
            y_rw, wkv1 = _rwkv(z3d, P_RW // RW_IN, sh0, wkv0[:, l], pr['rw'], l, t_tile=RW_TILE, chunk=RW_CHUNK,
                               t_valid=t_len, carry=True, prec=PREC)
        else:
            z_rw = grp_rows(jnp.pad(z3d[:, :, P_RW:P_RW + RW_IN], ((0, 0), (0, RW_PAD - t_len), (0, 0))))
            prev0 = grp_rows(jnp.pad(sh0, ((0, 0), (0, RW_PAD - 1), (0, 0))))
            y_rw, wkv_buf = _rwkv(z_rw, 0, prev0, wkv0, pr['rw'], l, t_tile=RW_GROUP * RW_PAD, chunk=RW_PAD,
                                  t_valid=t_len, carry=False, prec=PREC_SHORT, wkv_out=wkv_buf)
            y_rw = y_rw.reshape(bsz, RW_PAD, C_BR)[:, :t_len]
            wkv1 = None
        shift1 = z3d[:, t_len - 1, P_RW:P_RW + RW_IN]
        h0 = jnp.concatenate([s5r0[:, l].reshape(bsz, S5_N), s5i0[:, l].reshape(bsz, S5_N)], axis=1)
        if is_prompt:
            y_s5, h1 = _s5(z3d, P_S5 // C_BR, h0, pr['s5'], l, bsz=bsz, t_len=t_len, t_tile=S5_TILE)
        else:
            y_s5, h1 = _s5(z2d, P_S5 // C_BR, h0, pr['s5'], l, bsz=bsz, t_len=t_len, t_tile=t_len)
        s5r1 = h1[:, :S5_N].reshape(bsz, S5_GROUPS, S5_STATE)
        s5i1 = h1[:, S5_N:].reshape(bsz, S5_GROUPS, S5_STATE)
        if is_prompt:
            y_cv, conv_buf = _conv(z3d, P_CV // (2 * C_BR), conv0, conv_buf, l, pr['cv'], t_tile=CV_TILE)
        else:
            y_cv, conv_buf = _conv_short(z2d, P_CV // (2 * C_BR), conv0, conv_buf, l, pr['cv'], bsz=bsz,
                                         t_len=t_len)
        y_gm, v_gm = _gmlp_group(z3d, pr['gm'], l, is_prompt=is_prompt)
        ys = [y.reshape(n, C_BR) for y in (y_rw, y_s5, y_cv, y_gm)]
        x2d = _merge(x2d, ys, big, pr['mg'], l)
        x2d = _moe(x2d, big, pr['moe'], l)
        outs.append((wkv1, shift1, s5r1, s5i1, v_gm))
    stack = lambda i: None if outs[0][i] is None else jnp.stack([o[i] for o in outs], axis=1)
    wkv = stack(0) if is_prompt else wkv_buf
    return (x2d.reshape(bsz, t_len, D_MODEL),
            (wkv, stack(1), stack(2), stack(3), conv_buf.reshape(conv_shape), stack(4)))


def kernel(x_prompt, x_sample, state_rwkv_wkv, state_rwkv_shift, state_s5_re, state_s5_im, cache_conv,
           w_in, rw_mu, rw_w0, rw_w2, rw_a0, rw_a2, rw_g2, rw_kk, rw_ka, rw_rk, rw_gn_g, rw_gn_b,
           s5_lam_re, s5_lam_im, s5_log_dt, s5_b_re, s5_b_im, s5_c_re, s5_c_im, s5_d, s5_glu_w, s5_glu_b,
           cv_w, cv_b, cv_ln_g, cv_ln_b, gm_ln_g, gm_ln_b, gm_ws, gm_bs,
           w_branch, w_out, ln1_g, ln1_b,
           moe_wg1, moe_bg1, moe_wg2, moe_bg2, moe_w_up, moe_w_down, ln2_g, ln2_b):
    p = dict(w_in=w_in, rw_mu=rw_mu, rw_w0=rw_w0, rw_w2=rw_w2, rw_a0=rw_a0, rw_a2=rw_a2, rw_g2=rw_g2,
             rw_kk=rw_kk, rw_ka=rw_ka, rw_rk=rw_rk, rw_gn_g=rw_gn_g, rw_gn_b=rw_gn_b,
             s5_lam_re=s5_lam_re, s5_lam_im=s5_lam_im, s5_log_dt=s5_log_dt, s5_b_re=s5_b_re, s5_b_im=s5_b_im,
             s5_c_re=s5_c_re, s5_c_im=s5_c_im, s5_d=s5_d, s5_glu_w=s5_glu_w, s5_glu_b=s5_glu_b,
             cv_w=cv_w, cv_b=cv_b, cv_ln_g=cv_ln_g, cv_ln_b=cv_ln_b, gm_ln_g=gm_ln_g, gm_ln_b=gm_ln_b,
             gm_ws=gm_ws, gm_bs=gm_bs, w_branch=w_branch, w_out=w_out, ln1_g=ln1_g, ln1_b=ln1_b,
             moe_wg1=moe_wg1, moe_bg1=moe_bg1, moe_wg2=moe_wg2, moe_bg2=moe_bg2, moe_w_up=moe_w_up,
             moe_w_down=moe_w_down, ln2_g=ln2_g, ln2_b=ln2_b)
    pr = _prep_params(p, x_sample.shape[1])
    bp = x_prompt.shape[0]
    dt = x_prompt.dtype
    y_prompt, (p_wkv, p_shift, p_s5r, p_s5i, p_conv, _) = _run_group(
        x_prompt,
        jnp.zeros((bp, DEPTH, RW_HEADS, RW_HEAD, RW_HEAD), dt),
        jnp.zeros((bp, DEPTH, RW_IN), dt),
        jnp.zeros((bp, DEPTH, S5_GROUPS, S5_STATE), dt),
        jnp.zeros((bp, DEPTH, S5_GROUPS, S5_STATE), dt),
        jnp.zeros((bp, DEPTH, CONV_HIST, C_BR), dt),
        pr, is_prompt=True)
    y_sample, (s_wkv, s_shift, s_s5r, s_s5i, s_conv, s_gmv) = _run_group(
        x_sample, state_rwkv_wkv, state_rwkv_shift, state_s5_re, state_s5_im, cache_conv, pr, is_prompt=False)
    return (y_prompt, y_sample, p_wkv, p_shift, p_s5r, p_s5i, p_conv,
            s_wkv, s_shift, s_s5r, s_s5i, s_conv, s_gmv)
```

```python
import functools
import math

import numpy as np
import jax
import jax.numpy as jnp
from jax import lax
from jax.experimental import pallas as pl
from jax.experimental.pallas import tpu as pltpu

D_MODEL = 1024
DEPTH = 4
N_BRANCH = 4
C_BR = D_MODEL // 4
RW_HEAD = 64
RW_HEADS = C_BR // RW_HEAD
RW_LW = 64
RW_LA = 64
RW_LG = 128
RW_IN = 3 * C_BR + RW_LW + RW_LA + RW_LG
RW_GN_EPS = 64e-5
RW_PAD = 8
RW_GROUP = 16
S5_GW = 16
S5_GROUPS = C_BR // S5_GW
S5_STATE = 64
S5_N = S5_GROUPS * S5_STATE
CONV_W = 31
CONV_HIST = CONV_W - 1
CONV_HIST_PAD = 32
CHUNK = 128
GM_TILE = 8 * CHUNK
GM_HEADS = 4
GM_HEAD = C_BR // GM_HEADS
N_GROUPS = 4
E_PER_GROUP = 4
N_EXPERTS = N_GROUPS * E_PER_GROUP
D_EXPERT = D_MODEL // 4
LN_EPS = 1e-5
DN_ALPHA = (2 * DEPTH) ** 0.25
OFF_S5 = RW_IN
OFF_CV = OFF_S5 + C_BR
OFF_GM = OFF_CV + 2 * C_BR
OFF_GATE = OFF_GM + 2 * C_BR
P_RW = 0
P_CV = P_RW + RW_IN
P_GM = P_CV + 2 * C_BR
P_S5 = P_GM + 2 * C_BR
N_MIX = P_S5 + C_BR

LANES = 128
SUBLANES = 8
TOKEN_TILE = 512
RW_TILE = 512
RW_CHUNK = 64
S5_TILE = 128
CV_TILE = 512
VMEM_LIMIT = 56 * 1024 * 1024

F32 = jnp.float32
BF16 = jnp.bfloat16
MM_DTYPE = jnp.bfloat16
NEG_BIG = -1e30


def _mm(a, b):
    return jnp.dot(a.astype(MM_DTYPE), b.astype(MM_DTYPE), preferred_element_type=F32)


def _split_bf16(a):
    hi = a.astype(BF16)
    return hi, (a - hi.astype(F32)).astype(BF16)


_NN = ((1,), (0,))
_NT = ((1,), (1,))
_TN = ((0,), (0,))


def _dot(a, b, mode, dims=_NN, exact=None):
    dn = (dims, ((), ()))
    f = lambda x, y: lax.dot_general(x, y, dn, preferred_element_type=F32)
    if mode == 'bf16':
        return f(a.astype(BF16), b.astype(BF16))
    if mode == 'x2':
        mode, exact = 'x3', 'b'
    assert mode == 'x3'
    if exact == 'a':
        b_hi, b_lo = _split_bf16(b)
        a = a.astype(BF16)
        return f(a, b_hi) + f(a, b_lo)
    if exact == 'b':
        a_hi, a_lo = _split_bf16(a)
        b = b.astype(BF16)
        return f(a_hi, b) + f(a_lo, b)
    a_hi, a_lo = _split_bf16(a)
    b_hi, b_lo = _split_bf16(b)
    return f(a_hi, b_hi) + (f(a_hi, b_lo) + f(a_lo, b_hi))


PREC = dict(cumsum='x3', headsum='x3', amat='bf16', inv='bf16', state='bf16', apply='bf16', update='bf16', route='x2')
PREC_SHORT = dict(PREC, update='x3')


def _sigmoid(x):
    return jax.nn.sigmoid(x)


def _softplus(x):
    return jnp.maximum(x, 0.0) + jnp.log1p(jnp.exp(-jnp.abs(x)))


def _gelu_tanh(x):
    return 0.5 * x * (1.0 + jnp.tanh(math.sqrt(2.0 / math.pi) * (x + 0.044715 * (x * x * x))))


def _layer_norm(x, g, b):
    mu = jnp.mean(x, axis=-1, keepdims=True)
    d = x - mu
    var = jnp.mean(d * d, axis=-1, keepdims=True)
    return d * lax.rsqrt(var + LN_EPS) * g + b


def _params(sem):
    return pltpu.CompilerParams(dimension_semantics=sem, vmem_limit_bytes=VMEM_LIMIT)


def _full(shape):
    nd = len(shape)
    return pl.BlockSpec(shape, lambda *_: (0,) * nd)


def _state_buffer(buf):
    return [buf], [pl.BlockSpec(memory_space=pl.ANY)]


def _layer_block(shape, layer, **kwargs):
    nd = len(shape)
    return pl.BlockSpec((None,) + tuple(shape), lambda *_: (layer,) + (0,) * nd, **kwargs)


def _inproj_kernel(x_ref, w_ref, z_ref):
    z_ref[...] = _mm(x_ref[...], w_ref[...])


def _inproj(x2d, w_bf16, layer):
    n = x2d.shape[0]
    tm = min(TOKEN_TILE, n)
    return pl.pallas_call(
        _inproj_kernel,
        grid=(n // tm,),
        in_specs=[pl.BlockSpec((tm, D_MODEL), lambda i: (i, 0)), _layer_block((D_MODEL, N_MIX), layer)],
        out_specs=pl.BlockSpec((tm, N_MIX), lambda i: (i, 0)),
        out_shape=jax.ShapeDtypeStruct((n, N_MIX), F32),
        compiler_params=_params(("parallel",)),
        name="inproj",
    )(x2d, w_bf16)


def _heads_bd(x, lane_head):
    return jnp.concatenate([jnp.where(lane_head == h, x, 0.0) for h in range(RW_HEADS)], axis=0)


def _state_bd(wkv):
    zeros_blk = jnp.zeros((RW_HEAD, RW_HEAD), F32)
    return jnp.concatenate(
        [jnp.concatenate([wkv[h] if g == h else zeros_blk for g in range(RW_HEADS)], axis=1)
         for h in range(RW_HEADS)], axis=0)


def _rwkv_prep(z, z_prev, valid, mu_ref, w0_ref, w2_ref, a0_ref, a2_ref, g2_ref, kkw_ref, kaw_ref, hsum, tri_ref,
               chunk, prec):
    zs = z + mu_ref[...] * (z_prev - z)
    r = zs[:, 0:C_BR]
    k = zs[:, C_BR:2 * C_BR]
    v = zs[:, 2 * C_BR:3 * C_BR]
    lwla = zs[:, 3 * C_BR:3 * C_BR + RW_LW + RW_LA]
    lg = zs[:, 3 * C_BR + RW_LW + RW_LA:]
    w_log = -_softplus(-(w0_ref[...] + _mm(jnp.tanh(lwla), w2_ref[...]))) - 0.5
    ld = -jnp.exp(w_log)
    a = _sigmoid(a0_ref[...] + _mm(lwla, a2_ref[...]))
    g = _mm(_sigmoid(lg), g2_ref[...])
    kk = k * kkw_ref[...]
    kk = kk * lax.rsqrt(jnp.maximum(_dot(kk * kk, hsum, prec['headsum'], exact='b'), 1e-24))
    k2 = k * (1.0 + (a - 1.0) * kaw_ref[...])
    bv = kk * a
    if valid is not None:
        ld = jnp.where(valid, ld, 0.0)
        k2 = jnp.where(valid, k2, 0.0)
        v = jnp.where(valid, v, 0.0)
        bv = jnp.where(valid, bv, 0.0)
    t_tile = z.shape[0]
    n = tri_ref.shape[0]
    lc = jnp.concatenate([_dot(tri_ref[...], ld[i * n:(i + 1) * n], prec['cumsum'], exact='a')
                          for i in range(t_tile // n)], axis=0)
    lend = jnp.concatenate([jnp.broadcast_to(lc[(c + 1) * chunk - 1:(c + 1) * chunk], (chunk, C_BR))
                            for c in range(t_tile // chunk)], axis=0)
    e_end = jnp.exp(lend - lc)
    e_neg = jnp.exp(-lc)
    return dict(r=r, k2=k2, v=v, g=g, rt=r * jnp.exp(lc), kkt=kk * jnp.exp(lc - ld), kh=k2 * e_neg, bh=bv * e_neg,
                kw=k2 * e_end, bw=bv * e_end, wc=jnp.exp(lend))


def _rwkv_chunks_local(chunks, strict_ref, incl_ref, lvl_ref, chunk, prec):
    hc = RW_HEADS * chunk
    nk = RW_HEADS * RW_HEAD
    n = range(len(chunks))
    kkt, rt, kh, bh, vv, kw, bw = (list(x) for x in zip(*chunks))
    amat = [_dot(jnp.concatenate([kkt[c], rt[c]], axis=0), jnp.concatenate([kh[c], bh[c]], axis=0),
                 prec['amat'], _NT) for c in n]
    strict = strict_ref[...] != 0.0
    incl = incl_ref[...] != 0.0
    a_kk = [jnp.where(strict, amat[c][0:hc, 0:hc], 0.0) for c in n]
    a_kb = [jnp.where(strict, amat[c][0:hc, hc:2 * hc], 0.0) for c in n]
    a_rk = [jnp.where(incl, amat[c][hc:2 * hc, 0:hc], 0.0) for c in n]
    a_rb = [jnp.where(incl, amat[c][hc:2 * hc, hc:2 * hc], 0.0) for c in n]
    av = [_dot(jnp.concatenate([a_kk[c], a_rk[c]], axis=0), vv[c], prec['apply']) for c in n]
    ri = lax.broadcasted_iota(jnp.int32, (hc, hc), 0)
    cj = lax.broadcasted_iota(jnp.int32, (hc, hc), 1)
    eye = jnp.where(ri == cj, 1.0, 0.0)
    lvl0 = lvl_ref[0] != 0.0
    t_inv = [eye - jnp.where(lvl0, a_kb[c], 0.0) for c in n]
    for lv in range(1, lvl_ref.shape[0]):
        lvl = lvl_ref[lv] != 0.0
        half = [_dot(t_inv[c], jnp.where(lvl, a_kb[c], 0.0), prec['inv']) for c in n]
        t_inv = [t_inv[c] - _dot(half[c], t_inv[c], prec['inv']) for c in n]
    gu = [_dot(t_inv[c], jnp.concatenate([kkt[c], av[c][0:hc]], axis=1), prec['apply']) for c in n]
    pu = [_dot(a_rb[c], gu[c], prec['apply']) for c in n]
    mc = [_dot(gu[c][:, 0:nk], bw[c], prec['update'], _TN) for c in n]
    nn = [_dot(jnp.concatenate([vv[c], gu[c][:, nk:2 * nk]], axis=0), jnp.concatenate([kw[c], -bw[c]], axis=0),
               prec['update'], _TN) for c in n]
    return [(rt[c] - pu[c][:, 0:nk], av[c][hc:2 * hc] - pu[c][:, nk:2 * nk], mc[c], nn[c]) for c in n]


def _rwkv_chunks_direct(chunks, states, wcs, strict_ref, incl_ref, lvl_ref, chunk, prec):
    hc = RW_HEADS * chunk
    n = range(len(chunks))
    kkt, rt, kh, bh, vv, kw, bw = (list(x) for x in zip(*chunks))
    lhs = [jnp.concatenate([kkt[c], rt[c]], axis=0) for c in n]
    amat = [_dot(lhs[c], jnp.concatenate([kh[c], bh[c]], axis=0), prec['amat'], _NT) for c in n]
    ls = [_dot(lhs[c], states[c], prec['state'], _NT) for c in n]
    strict = strict_ref[...] != 0.0
    incl = incl_ref[...] != 0.0
    a_kk = [jnp.where(strict, amat[c][0:hc, 0:hc], 0.0) for c in n]
    a_kb = [jnp.where(strict, amat[c][0:hc, hc:2 * hc], 0.0) for c in n]
    a_rk = [jnp.where(incl, amat[c][hc:2 * hc, 0:hc], 0.0) for c in n]
    a_rb = [jnp.where(incl, amat[c][hc:2 * hc, hc:2 * hc], 0.0) for c in n]
    av = [_dot(jnp.concatenate([a_kk[c], a_rk[c]], axis=0), vv[c], prec['apply']) for c in n]
    ri = lax.broadcasted_iota(jnp.int32, (hc, hc), 0)
    cj = lax.broadcasted_iota(jnp.int32, (hc, hc), 1)
    eye = jnp.where(ri == cj, 1.0, 0.0)
    lvl0 = lvl_ref[0] != 0.0
    t_inv = [eye - jnp.where(lvl0, a_kb[c], 0.0) for c in n]
    for lv in range(1, lvl_ref.shape[0]):
        lvl = lvl_ref[lv] != 0.0
        half = [_dot(t_inv[c], jnp.where(lvl, a_kb[c], 0.0), prec['inv']) for c in n]
        t_inv = [t_inv[c] - _dot(half[c], t_inv[c], prec['inv']) for c in n]
    u = [_dot(t_inv[c], ls[c][0:hc] + av[c][0:hc], prec['apply']) for c in n]
    o = [ls[c][hc:2 * hc] + av[c][hc:2 * hc] - _dot(a_rb[c], u[c], prec['apply']) for c in n]
    s_new = [states[c] * wcs[c] + _dot(jnp.concatenate([vv[c], u[c]], axis=0),
                                       jnp.concatenate([kw[c], -bw[c]], axis=0), prec['update'], _TN) for c in n]
    out = []
    for c in n:
        o_c = o[c][0:chunk]
        for h in range(1, RW_HEADS):
            o_c = o_c + o[c][h * chunk:(h + 1) * chunk]
        out.append((o_c, s_new[c]))
    return out


def _rwkv_chunk_apply(s, local, wc, chunk, prec):
    p, o0, mc, nn = local
    o = _dot(p, s, prec['state'], _NT) + o0
    o_c = o[0:chunk]
    for h in range(1, RW_HEADS):
        o_c = o_c + o[h * chunk:(h + 1) * chunk]
    return o_c, s * wc - _dot(s, mc, prec['state']) + nn


def _rwkv_post(o, pre, rk_ref, gng_ref, gnb_ref, hsum, prec):
    inv_n = 1.0 / RW_HEAD
    o_mu = _dot(o, hsum, prec['headsum'], exact='b') * inv_n
    od = o - o_mu
    o_var = _dot(od * od, hsum, prec['headsum'], exact='b') * inv_n
    on = od * lax.rsqrt(o_var + RW_GN_EPS) * gng_ref[...] + gnb_ref[...]
    bonus = _dot(pre['r'] * pre['k2'] * rk_ref[...], hsum, prec['headsum'], exact='b') * pre['v']
    return (on + bonus) * pre['g']


_RWKV_LOCAL_KEYS = ('kkt', 'rt', 'kh', 'bh', 'v', 'kw', 'bw')


def _rwkv_kernel(z_ref, prev0_ref, wkv0_ref, mu_ref, w0_ref, w2_ref, a0_ref, a2_ref, g2_ref, kkw_ref, kaw_ref,
                 rk_ref, gng_ref, gnb_ref, hsum_ref, tri_ref, strict_ref, incl_ref, lvl_ref, *rest,
                 t_tile, chunk, t_valid, n_tiles, carry, prec):
    y_ref, wkv1_ref, s_scr, prev_scr = rest[-4:]
    i = pl.program_id(1)
    z = z_ref[...]
    row = lax.broadcasted_iota(jnp.int32, (t_tile, 1), 0)
    if carry:
        assert t_valid == t_tile * n_tiles

        @pl.when(i == 0)
        def _():
            s_scr[...] = _state_bd(wkv0_ref)
            prev_scr[...] = prev0_ref[...]

        z_prev = jnp.where(row == 0, prev_scr[...], pltpu.roll(z, 1, 0))
        prev_scr[...] = z[t_tile - 1:t_tile, :]
        valid = None
    else:
        step = row % chunk
        z_prev = jnp.where(step == 0, prev0_ref[...], pltpu.roll(z, 1, 0))
        valid = step < t_valid
    hsum = hsum_ref[...]
    pre = _rwkv_prep(z, z_prev, valid, mu_ref, w0_ref, w2_ref, a0_ref, a2_ref, g2_ref, kkw_ref, kaw_ref, hsum,
                     tri_ref, chunk, prec)
    lane_head = lax.broadcasted_iota(jnp.int32, (chunk, C_BR), 1) // RW_HEAD
    n_chunks = t_tile // chunk
    chunks = [tuple(_heads_bd(pre[key][c * chunk:(c + 1) * chunk], lane_head) for key in _RWKV_LOCAL_KEYS)
              for c in range(n_chunks)]
    wcs = [pre['wc'][c * chunk:c * chunk + 1] for c in range(n_chunks)]
    o_rows = []
    if carry:
        local = _rwkv_chunks_local(chunks, strict_ref, incl_ref, lvl_ref, chunk, prec)
        s = s_scr[...]
        for c in range(n_chunks):
            o_c, s = _rwkv_chunk_apply(s, local[c], wcs[c], chunk, prec)
            o_rows.append(o_c)
    else:
        states = [_state_bd(wkv0_ref.at[c]) for c in range(n_chunks)]
        for c, (o_c, s_c) in enumerate(_rwkv_chunks_direct(chunks, states, wcs, strict_ref, incl_ref, lvl_ref,
                                                           chunk, prec)):
            o_rows.append(o_c)
            for h in range(RW_HEADS):
                wkv1_ref[c, h] = s_c[h * RW_HEAD:(h + 1) * RW_HEAD, h * RW_HEAD:(h + 1) * RW_HEAD]
    y_ref[...] = _rwkv_post(jnp.concatenate(o_rows, axis=0), pre, rk_ref, gng_ref, gnb_ref, hsum, prec)

    if carry:
        s_scr[...] = s

        @pl.when(i == n_tiles - 1)
        def _():
            for h in range(RW_HEADS):
                wkv1_ref[h] = s[h * RW_HEAD:(h + 1) * RW_HEAD, h * RW_HEAD:(h + 1) * RW_HEAD]


def _rwkv(z3d, col_blk, prev0, wkv0, pw, layer, *, t_tile, chunk, t_valid, carry, prec, wkv_out=None):
    bsz, t_len, _ = z3d.shape
    n_tiles = t_len // t_tile
    cs = min(t_tile, max(chunk, 64))
    assert t_tile % cs == 0 and cs % chunk == 0
    idx = np.arange(cs)
    tri = jnp.asarray(((idx[:, None] // chunk == idx[None, :] // chunk)
                       & (idx[None, :] <= idx[:, None])).astype(np.float32))
    hid = np.arange(C_BR) // RW_HEAD
    hsum = jnp.asarray((hid[:, None] == hid[None, :]).astype(np.float32))
    hc = RW_HEADS * chunk
    hh, tt = np.arange(hc) // chunk, np.arange(hc) % chunk
    same_head = hh[:, None] == hh[None, :]
    strict = jnp.asarray((same_head & (tt[None, :] < tt[:, None])).astype(np.float32))
    incl = jnp.asarray((same_head & (tt[None, :] <= tt[:, None])).astype(np.float32))
    lvls = []
    m = 1
    while m < chunk:
        lvls.append(same_head & (tt[:, None] // (2 * m) == tt[None, :] // (2 * m))
                    & (tt[:, None] % (2 * m) >= m) & (tt[None, :] % (2 * m) < m))
        m *= 2
    lvl = jnp.asarray(np.stack(lvls).astype(np.float32))
    lb = lambda shape: _layer_block(shape, layer)
    vec = lambda n: lb((1, n))
    if carry:
        prev_spec = pl.BlockSpec((None, 1, RW_IN), lambda b, i: (b, 0, 0))
        wkv_spec = pl.BlockSpec((None, RW_HEADS, RW_HEAD, RW_HEAD), lambda b, i: (b, 0, 0, 0))
        extra_in, extra_specs = [], []
    else:
        assert n_tiles == 1
        prev_spec = pl.BlockSpec((None, t_tile, RW_IN), lambda b, i: (b, 0, 0))
        wkv_spec = pl.BlockSpec((t_tile // chunk, None, RW_HEADS, RW_HEAD, RW_HEAD),
                                lambda b, i: (b, layer, 0, 0, 0))
        extra_in, extra_specs = _state_buffer(wkv_out)
    operands = [z3d, prev0, wkv0, pw['mu'], pw['w0'], pw['w2'], pw['a0'], pw['a2'], pw['g2'], pw['kk'], pw['ka'],
                pw['rk'], pw['gn_g'], pw['gn_b'], hsum, tri, strict, incl, lvl] + extra_in
    aliases = {len(operands) - 1: 1} if extra_in else {}
    kern = functools.partial(_rwkv_kernel, t_tile=t_tile, chunk=chunk, t_valid=t_valid, n_tiles=n_tiles, carry=carry,
                             prec=prec)
    return pl.pallas_call(
        kern,
        grid=(bsz, n_tiles),
        in_specs=[pl.BlockSpec((None, t_tile, RW_IN), lambda b, i: (b, i, col_blk)), prev_spec, wkv_spec,
                  vec(RW_IN), vec(C_BR), lb((RW_LW + RW_LA, C_BR)), vec(C_BR), lb((RW_LW + RW_LA, C_BR)),
                  lb((RW_LG, C_BR)), vec(C_BR), vec(C_BR), vec(C_BR), vec(C_BR), vec(C_BR),
                  _full((C_BR, C_BR)), _full((cs, cs)), _full((hc, hc)), _full((hc, hc)), _full(lvl.shape)]
        + extra_specs,
        out_specs=[pl.BlockSpec((None, t_tile, C_BR), lambda b, i: (b, i, 0)), wkv_spec],
        out_shape=[jax.ShapeDtypeStruct((bsz, t_len, C_BR), F32), jax.ShapeDtypeStruct(wkv0.shape, F32)],
        input_output_aliases=aliases,
        scratch_shapes=[pltpu.VMEM((C_BR, C_BR), F32), pltpu.VMEM((1, RW_IN), F32)],
        compiler_params=_params(("parallel", "arbitrary")),
        name="rwkv7",
    )(*operands)


def _s5_kernel(u_ref, h0_ref, lbr_ref, lbi_ref, bb_ref, cc_ref, d_ref, gw_ref, gb_ref,
               y_ref, h1_ref, bu_scr, h_scr, *relayout_scr, bsz, t_tile, n_tiles, time_major):
    i = pl.program_id(0)

    @pl.when(i == 0)
    def _():
        h_scr[...] = h0_ref[...]

    rows = bsz * t_tile
    n_lt = S5_N // LANES
    n_ut = C_BR // LANES
    lane_tile = lambda ref, j: ref[:, j * LANES:(j + 1) * LANES]
    seq_rows = lambda t: pl.ds(t, bsz, stride=t_tile)
    step_rows = lambda t: pl.ds(pl.multiple_of(t * bsz, bsz), bsz)
    u = u_ref[...].reshape(rows, C_BR)
    if time_major:
        bt_scr, tm_scr = relayout_scr
        for j in range(n_ut):
            bt_scr[j] = lane_tile(u, j)

        def to_time_major(t, _):
            for j in range(n_ut):
                tm_scr[j, step_rows(t), :] = bt_scr[j, seq_rows(t), :]
            return 0

        lax.fori_loop(0, t_tile, to_time_major, 0, unroll=4)
        u = jnp.concatenate([tm_scr[j] for j in range(n_ut)], axis=1)
        sl_of = step_rows
    else:
        sl_of = seq_rows
    bu = _mm(u, bb_ref[...])
    for j in range(2 * n_lt):
        bu_scr[j] = lane_tile(bu, j)
    lbr = [jnp.broadcast_to(lane_tile(lbr_ref, j), (bsz, LANES)) for j in range(n_lt)]
    lbi = [jnp.broadcast_to(lane_tile(lbi_ref, j), (bsz, LANES)) for j in range(n_lt)]

    def step(t, carry):
        hr, hi = carry
        sl = sl_of(t)
        new_r, new_i = [], []
        for j in range(n_lt):
            nr = lbr[j] * hr[j] - lbi[j] * hi[j] + bu_scr[j, sl, :]
            ni = lbr[j] * hi[j] + lbi[j] * hr[j] + bu_scr[n_lt + j, sl, :]
            bu_scr[j, sl, :] = nr
            bu_scr[n_lt + j, sl, :] = ni
            new_r.append(nr)
            new_i.append(ni)
        return tuple(new_r), tuple(new_i)

    h_init = (tuple(lane_tile(h_scr, j) for j in range(n_lt)),
              tuple(lane_tile(h_scr, n_lt + j) for j in range(n_lt)))
    hr, hi = lax.fori_loop(0, t_tile, step, h_init, unroll=2)
    for j in range(n_lt):
        h_scr[:, j * LANES:(j + 1) * LANES] = hr[j]
        h_scr[:, (n_lt + j) * LANES:(n_lt + j + 1) * LANES] = hi[j]

    hs = jnp.concatenate([bu_scr[j] for j in range(2 * n_lt)], axis=1)
    y = _mm(hs, cc_ref[...]) + d_ref[...] * u
    y = _gelu_tanh(y)
    y = y * _sigmoid(_mm(y, gw_ref[...]) + gb_ref[...])
    if time_major:
        for j in range(n_ut):
            tm_scr[j] = lane_tile(y, j)

        def to_seq_major(t, _):
            for j in range(n_ut):
                bt_scr[j, seq_rows(t), :] = tm_scr[j, step_rows(t), :]
            return 0

        lax.fori_loop(0, t_tile, to_seq_major, 0, unroll=4)
        y = jnp.concatenate([bt_scr[j] for j in range(n_ut)], axis=1)
    y_ref[...] = y.reshape(y_ref.shape)

    @pl.when(i == n_tiles - 1)
    def _():
        h1_ref[...] = h_scr[...]


def _s5(z, col_blk, h0, ps, layer, *, bsz, t_len, t_tile):
    lb = lambda shape: _layer_block(shape, layer)
    n_tiles = t_len // t_tile
    rows = bsz * t_tile
    if z.ndim == 3:
        u_spec = pl.BlockSpec((bsz, t_tile, C_BR), lambda i: (0, i, col_blk))
        y_spec = pl.BlockSpec((bsz, t_tile, C_BR), lambda i: (0, i, 0))
        y_shape = (bsz, t_len, C_BR)
    else:
        assert n_tiles == 1
        u_spec = pl.BlockSpec((rows, C_BR), lambda i: (0, col_blk))
        y_spec = pl.BlockSpec((rows, C_BR), lambda i: (0, 0))
        y_shape = (rows, C_BR)
    time_major = z.ndim == 3 and bsz == SUBLANES
    relayout_scr = [pltpu.VMEM((C_BR // LANES, rows, LANES), F32)] * 2 if time_major else []
    kern = functools.partial(_s5_kernel, bsz=bsz, t_tile=t_tile, n_tiles=n_tiles, time_major=time_major)
    return pl.pallas_call(
        kern,
        grid=(n_tiles,),
        in_specs=[u_spec, _full((bsz, 2 * S5_N)), lb((1, S5_N)), lb((1, S5_N)),
                  lb((C_BR, 2 * S5_N)), lb((2 * S5_N, C_BR)), lb((1, C_BR)), lb((C_BR, C_BR)), lb((1, C_BR))],
        out_specs=[y_spec, _full((bsz, 2 * S5_N))],
        out_shape=[jax.ShapeDtypeStruct(y_shape, F32), jax.ShapeDtypeStruct((bsz, 2 * S5_N), F32)],
        scratch_shapes=[pltpu.VMEM((2 * S5_N // LANES, rows, LANES), F32), pltpu.VMEM((bsz, 2 * S5_N), F32)]
        + relayout_scr,
        compiler_params=_params(("arbitrary",)),
        name="s5",
    )(z, h0, ps['lb_re'], ps['lb_im'], ps['bb'], ps['cc'], ps['d'], ps['glu_w'], ps['glu_b'])


def _conv_taps(full_scr, w_ref, t_tile):
    lo = CONV_HIST_PAD - CONV_HIST
    sub = SUBLANES
    assert t_tile % sub == 0
    acc = None
    for rho in range(sub):
        offs = [o for o in range(rho, lo + CONV_W, sub) if o >= lo]
        rows = t_tile + (sub if rho else 0)
        part = None
        for o in offs:
            term = full_scr[o - rho:o - rho + rows, :] * w_ref[o - lo:o - lo + 1, :]
            part = term if part is None else part + term
        part = part[rho:rho + t_tile]
        acc = part if acc is None else acc + part
    return acc


def _conv_kernel(z_ref, c0_ref, w_ref, b_ref, g_ref, be_ref, *rest, t_tile, n_tiles):
    y_ref, c1_ref, full_scr = rest[-3:]
    i = pl.program_id(1)
    lo = CONV_HIST_PAD - CONV_HIST

    @pl.when(i == 0)
    def _():
        full_scr[0:lo, :] = jnp.zeros((lo, C_BR), F32)
        full_scr[lo:CONV_HIST_PAD, :] = c0_ref[...]

    z = z_ref[...]
    full_scr[CONV_HIST_PAD:CONV_HIST_PAD + t_tile, :] = z[:, 0:C_BR] * _sigmoid(z[:, C_BR:2 * C_BR])
    y = _layer_norm(_conv_taps(full_scr, w_ref, t_tile) + b_ref[...], g_ref[...], be_ref[...])
    y_ref[...] = y * _sigmoid(y)
    hist = full_scr[t_tile:t_tile + CONV_HIST_PAD, :]
    full_scr[0:CONV_HIST_PAD, :] = hist

    @pl.when(i == n_tiles - 1)
    def _():
        c1_ref[...] = hist[lo:, :]


def _conv(z3d, col_blk, conv0, conv_out, layer, pc, *, t_tile):
    bsz, t_len, _ = z3d.shape
    n_tiles = t_len // t_tile
    state_spec = pl.BlockSpec((None, None, CONV_HIST, C_BR), lambda b, i: (b, layer, 0, 0))
    extra_in, extra_specs = _state_buffer(conv_out)
    kern = functools.partial(_conv_kernel, t_tile=t_tile, n_tiles=n_tiles)
    return pl.pallas_call(
        kern,
        grid=(bsz, n_tiles),
        in_specs=[pl.BlockSpec((None, t_tile, 2 * C_BR), lambda b, i: (b, i, col_blk)), state_spec,
                  _layer_block((CONV_W, C_BR), layer), _layer_block((1, C_BR), layer),
                  _layer_block((1, C_BR), layer), _layer_block((1, C_BR), layer)] + extra_specs,
        out_specs=[pl.BlockSpec((None, t_tile, C_BR), lambda b, i: (b, i, 0)), state_spec],
        out_shape=[jax.ShapeDtypeStruct((bsz, t_len, C_BR), F32), jax.ShapeDtypeStruct(conv0.shape, F32)],
        input_output_aliases={6: 1},
        scratch_shapes=[pltpu.VMEM((CONV_HIST_PAD + t_tile, C_BR), F32)],
        compiler_params=_params(("parallel", "arbitrary")),
        name="conv",
    )(z3d, conv0, pc['w'], pc['b'], pc['ln_g'], pc['ln_b'], *extra_in)


def _conv_short_kernel(z_ref, c0_ref, w_ref, b_ref, g_ref, be_ref, *rest, bsz, t_len):
    y_ref, c1_ref, in_scr, out_scr = rest[-4:]
    n_lt = C_BR // LANES
    z = z_ref[...]
    c = z[:, 0:C_BR] * _sigmoid(z[:, C_BR:2 * C_BR])
    for j in range(n_lt):
        in_scr[j] = c[:, j * LANES:(j + 1) * LANES]
    step_rows = lambda t: pl.ds(t, bsz, stride=t_len)
    hist = lambda r: c0_ref[:, r * C_BR:(r + 1) * C_BR]
    new = [jnp.concatenate([in_scr[j, step_rows(t), :] for j in range(n_lt)], axis=1) for t in range(t_len)]
    full = lambda r: hist(r) if r < CONV_HIST else new[r - CONV_HIST]
    for t in range(t_len):
        acc = b_ref[...] + full(t) * w_ref[0:1, :]
        for j in range(1, CONV_W):
            acc = acc + full(t + j) * w_ref[j:j + 1, :]
        y = _layer_norm(acc, g_ref[...], be_ref[...])
        y = y * _sigmoid(y)
        for j in range(n_lt):
            out_scr[j, step_rows(t), :] = y[:, j * LANES:(j + 1) * LANES]
    y_ref[...] = jnp.concatenate([out_scr[j] for j in range(n_lt)], axis=1)
    for r in range(CONV_HIST):
        c1_ref[:, r * C_BR:(r + 1) * C_BR] = full(r + t_len)


def _conv_short(z2d, col_blk, conv0, conv_out, layer, pc, *, bsz, t_len):
    rows = bsz * t_len
    width = CONV_HIST * C_BR
    state_spec = pl.BlockSpec((bsz, width), lambda i: (0, layer))
    extra_in, extra_specs = _state_buffer(conv_out)
    kern = functools.partial(_conv_short_kernel, bsz=bsz, t_len=t_len)
    return pl.pallas_call(
        kern,
        grid=(1,),
        in_specs=[pl.BlockSpec((rows, 2 * C_BR), lambda i: (0, col_blk)), state_spec,
                  _layer_block((CONV_W, C_BR), layer), _layer_block((1, C_BR), layer),
                  _layer_block((1, C_BR), layer), _layer_block((1, C_BR), layer)] + extra_specs,
        out_specs=[pl.BlockSpec((rows, C_BR), lambda i: (0, 0)), state_spec],
        out_shape=[jax.ShapeDtypeStruct((rows, C_BR), F32), jax.ShapeDtypeStruct(conv0.shape, F32)],
        input_output_aliases={6: 1},
        scratch_shapes=[pltpu.VMEM((C_BR // LANES, rows, LANES), F32)] * 2,
        compiler_params=_params(("arbitrary",)),
        name="conv_short",
    )(z2d, conv0, pc['w'], pc['b'], pc['ln_g'], pc['ln_b'], *extra_in)


def _gmlp_kernel(z_ref, g_ref, b_ref, wm_ref, bias_ref, y_ref, *v_ref):
    z = z_ref[...]
    u = z[:, 0:C_BR]
    v = _layer_norm(z[:, C_BR:2 * C_BR], g_ref[...], b_ref[...])
    if v_ref:
        v_ref[0][...] = v
    vb = v.astype(MM_DTYPE)
    head = lax.broadcasted_iota(jnp.int32, (CHUNK, C_BR), 1) // GM_HEAD
    for c in range(z.shape[0] // CHUNK):
        rows = slice(c * CHUNK, (c + 1) * CHUNK)
        s = bias_ref[...]
        for h in range(GM_HEADS):
            s = s + jnp.where(head == h, jnp.dot(wm_ref[h], vb[rows], preferred_element_type=F32), 0.0)
        y_ref[rows, :] = u[rows] * s


def _gmlp(z3d, col_blk, pg, wm, bias, layer, *, emit_v):
    lb = lambda shape: _layer_block(shape, layer)
    bsz, t_len, _ = z3d.shape
    tile = min(GM_TILE, t_len)
    assert t_len % tile == 0 and tile % CHUNK == 0
    out_spec = pl.BlockSpec((None, tile, C_BR), lambda b, i: (b, i, 0))
    n_out = 2 if emit_v else 1
    return pl.pallas_call(
        _gmlp_kernel,
        grid=(bsz, t_len // tile),
        in_specs=[pl.BlockSpec((None, tile, 2 * C_BR), lambda b, i: (b, i, col_blk)),
                  lb((1, C_BR)), lb((1, C_BR)), lb((GM_HEADS, CHUNK, CHUNK)), lb((CHUNK, C_BR))],
        out_specs=[out_spec] * n_out,
        out_shape=[jax.ShapeDtypeStruct((bsz, t_len, C_BR), F32)] * n_out,
        compiler_params=_params(("parallel", "parallel")),
        name="gmlp",
    )(z3d, pg['ln_g'], pg['ln_b'], wm, bias)


def _merge_kernel(x_ref, yrw_ref, ys5_ref, ycv_ref, ygm_ref, wg_ref, wb_ref, wo_ref, g_ref, b_ref, o_ref):
    tm = x_ref.shape[0]
    halves = [slice(0, tm // 2), slice(tm // 2, tm)]
    x = [x_ref[h, :] for h in halves]
    xb = [v.astype(MM_DTYPE) for v in x]
    merged = [None for _ in halves]
    for bidx, y_ref in enumerate((yrw_ref, ys5_ref, ycv_ref, ygm_ref)):
        wg = wg_ref[:, bidx * D_MODEL:(bidx + 1) * D_MODEL]
        gate = [_sigmoid(jnp.dot(v, wg, preferred_element_type=F32)) for v in xb]
        term = [gate[i] * _mm(y_ref[h, :], wb_ref[bidx]) for i, h in enumerate(halves)]
        merged = [t if m is None else m + t for m, t in zip(merged, term)]
    proj = [_mm(m, wo_ref[...]) for m in merged]
    for i, h in enumerate(halves):
        o_ref[h, :] = _layer_norm(DN_ALPHA * x[i] + proj[i], g_ref[...], b_ref[...])


def _merge(x2d, ys, big, pm, layer):
    n = x2d.shape[0]
    tm = min(TOKEN_TILE, n)
    row = lambda w: pl.BlockSpec((tm, w), lambda i: (i, 0))
    return pl.pallas_call(
        _merge_kernel,
        grid=(n // tm,),
        in_specs=[row(D_MODEL), row(C_BR), row(C_BR), row(C_BR), row(C_BR),
                  _layer_block((D_MODEL, N_BRANCH * D_MODEL), layer), _layer_block((N_BRANCH, C_BR, D_MODEL), layer),
                  _layer_block((D_MODEL, D_MODEL), layer), _layer_block((1, D_MODEL), layer),
                  _layer_block((1, D_MODEL), layer)],
        out_specs=row(D_MODEL),
        out_shape=jax.ShapeDtypeStruct((n, D_MODEL), F32),
        compiler_params=_params(("parallel",)),
        name="merge",
    )(x2d, *ys, big['w_gate'], big['w_branch'], big['w_out'], pm['ln1_g'], pm['ln1_b'])


def _moe_kernel(x_ref, wg_ref, bg_ref, wu_ref, wd_ref, g_ref, b_ref, o_ref, hh_scr, *, tm):
    lane = lax.broadcasted_iota(jnp.int32, (tm, LANES), 1)
    x = x_ref[...]
    xb = x.astype(MM_DTYPE)
    logits = _dot(x, wg_ref[...], PREC['route']) + bg_ref[...]
    gl = jnp.where(lane < N_GROUPS, logits, NEG_BIG)
    gmax = jnp.max(gl, axis=-1, keepdims=True)
    g_sel = jnp.min(jnp.where(gl == gmax, lane, LANES), axis=-1, keepdims=True)
    p_group = 1.0 / jnp.sum(jnp.where(lane < N_GROUPS, jnp.exp(gl - gmax), 0.0), axis=-1, keepdims=True)
    first = N_GROUPS + g_sel * E_PER_GROUP
    el = jnp.where((lane >= first) & (lane < first + E_PER_GROUP), logits, NEG_BIG)
    m1 = jnp.max(el, axis=-1, keepdims=True)
    i1 = jnp.min(jnp.where(el == m1, lane, LANES), axis=-1, keepdims=True)
    el2 = jnp.where(lane == i1, NEG_BIG, el)
    m2 = jnp.max(el2, axis=-1, keepdims=True)
    i2 = jnp.min(jnp.where(el2 == m2, lane, LANES), axis=-1, keepdims=True)
    e2 = jnp.exp(m2 - m1)
    w1 = p_group / (1.0 + e2)
    w2 = p_group * e2 / (1.0 + e2)
    for e in range(N_EXPERTS):
        comb_e = jnp.where(i1 == e + N_GROUPS, w1, 0.0) + jnp.where(i2 == e + N_GROUPS, w2, 0.0)
        h = jnp.dot(xb, wu_ref[e], preferred_element_type=F32)
        h1 = h[:, 0:D_EXPERT]
        hh = h1 * _sigmoid(h1) * h[:, D_EXPERT:2 * D_EXPERT] * comb_e
        hh_scr[:, e * D_EXPERT:(e + 1) * D_EXPERT] = hh.astype(MM_DTYPE)
    moe = jnp.dot(hh_scr[...], wd_ref[...], preferred_element_type=F32)
    o_ref[...] = _layer_norm(DN_ALPHA * x + moe, g_ref[...], b_ref[...])


def _moe(x2d, big, pe, layer):
    n = x2d.shape[0]
    tm = min(TOKEN_TILE, n)
    kern = functools.partial(_moe_kernel, tm=tm)
    resident = lambda shape: _layer_block(shape, layer, pipeline_mode=pl.Buffered(1))
    return pl.pallas_call(
        kern,
        grid=(n // tm,),
        in_specs=[pl.BlockSpec((tm, D_MODEL), lambda i: (i, 0)),
                  _layer_block((D_MODEL, LANES), layer), _layer_block((1, LANES), layer),
                  resident((N_EXPERTS, D_MODEL, 2 * D_EXPERT)), resident((N_EXPERTS * D_EXPERT, D_MODEL)),
                  _layer_block((1, D_MODEL), layer), _layer_block((1, D_MODEL), layer)],
        out_specs=pl.BlockSpec((tm, D_MODEL), lambda i: (i, 0)),
        out_shape=jax.ShapeDtypeStruct((n, D_MODEL), F32),
        scratch_shapes=[pltpu.VMEM((tm, N_EXPERTS * D_EXPERT), MM_DTYPE)],
        compiler_params=_params(("parallel",)),
        name="moe",
    )(x2d, pe['wg'], pe['bg'], big['w_up'], big['w_down'], pe['ln2_g'], pe['ln2_b'])


def _block_diag(blocks):
    nl, g, m, n = blocks.shape
    eye = jnp.eye(g, dtype=blocks.dtype)
    return (eye[None, :, None, :, None] * blocks[:, :, :, None, :]).reshape(nl, g * m, g * n)


def _prep_params(p, t_short):
    nl = p['w_in'].shape[0]
    row = lambda a: a.reshape(nl, 1, -1).astype(F32)
    w_in = p['w_in']
    big = dict(w_mix=jnp.concatenate([w_in[..., :OFF_S5], w_in[..., OFF_CV:OFF_GM], w_in[..., OFF_GM:OFF_GATE],
                                      w_in[..., OFF_S5:OFF_CV]], axis=-1).astype(MM_DTYPE),
               w_gate=w_in[..., OFF_GATE:].astype(MM_DTYPE),
               w_branch=p['w_branch'].astype(MM_DTYPE), w_out=p['w_out'].astype(MM_DTYPE),
               w_up=p['moe_w_up'].astype(MM_DTYPE),
               w_down=p['moe_w_down'].astype(MM_DTYPE).reshape(nl, N_EXPERTS * D_EXPERT, D_MODEL))
    zeros_lora = jnp.zeros((nl, RW_LW, C_BR), F32)
    rw = dict(mu=row(p['rw_mu']), w0=row(p['rw_w0']),
              w2=jnp.concatenate([p['rw_w2'], zeros_lora], axis=1).astype(MM_DTYPE), a0=row(p['rw_a0']),
              a2=jnp.concatenate([zeros_lora, p['rw_a2']], axis=1).astype(MM_DTYPE),
              g2=p['rw_g2'].astype(MM_DTYPE), kk=row(p['rw_kk']), ka=row(p['rw_ka']), rk=row(p['rw_rk']),
              gn_g=row(p['rw_gn_g']), gn_b=row(p['rw_gn_b']))
    lr, li = p['s5_lam_re'].astype(F32), p['s5_lam_im'].astype(F32)
    dt = jnp.exp(p['s5_log_dt'].astype(F32))[..., None]
    mag = jnp.exp(lr * dt)
    lb_re, lb_im = mag * jnp.cos(li * dt), mag * jnp.sin(li * dt)
    den = lr * lr + li * li
    q_re = ((lb_re - 1.0) * lr + lb_im * li) / den
    q_im = (lb_im * lr - (lb_re - 1.0) * li) / den
    br, bi = p['s5_b_re'].astype(F32), p['s5_b_im'].astype(F32)
    bb_re = q_re[..., None] * br - q_im[..., None] * bi
    bb_im = q_re[..., None] * bi + q_im[..., None] * br
    t23 = lambda a: jnp.swapaxes(a, 2, 3)
    bb = jnp.concatenate([_block_diag(t23(bb_re)), _block_diag(t23(bb_im))], axis=2).astype(MM_DTYPE)
    cc = jnp.concatenate([_block_diag(t23(p['s5_c_re'].astype(F32))), -_block_diag(t23(p['s5_c_im'].astype(F32)))],
                         axis=1).astype(MM_DTYPE)
    s5 = dict(lb_re=row(lb_re), lb_im=row(lb_im), bb=bb, cc=cc, d=row(p['s5_d']),
              glu_w=p['s5_glu_w'].astype(MM_DTYPE), glu_b=row(p['s5_glu_b']))
    cv = dict(w=p['cv_w'].astype(F32), b=row(p['cv_b']), ln_g=row(p['cv_ln_g']), ln_b=row(p['cv_ln_b']))
    causal = jnp.tril(jnp.ones((CHUNK, CHUNK), dtype=bool))
    wm = jnp.where(causal, p['gm_ws'], 0).astype(F32)
    bias = jnp.repeat(jnp.swapaxes(p['gm_bs'], 1, 2), GM_HEAD, axis=2).astype(F32)
    reps = CHUNK // t_short
    wm_short = jnp.einsum('rs,lhij->lhrisj', jnp.eye(reps, dtype=F32),
                          wm[:, :, :t_short, :t_short]).reshape(nl, GM_HEADS, CHUNK, CHUNK)
    gm = dict(ln_g=row(p['gm_ln_g']), ln_b=row(p['gm_ln_b']), wm=wm.astype(MM_DTYPE), bias=bias,
              wm_short=wm_short.astype(MM_DTYPE), bias_short=jnp.tile(bias[:, :t_short], (1, reps, 1)))
    mg = dict(ln1_g=row(p['ln1_g']), ln1_b=row(p['ln1_b']))
    pad = LANES - N_GROUPS - N_EXPERTS
    wg = jnp.concatenate([p['moe_wg1'], p['moe_wg2'], jnp.zeros((nl, D_MODEL, pad), F32)], axis=2).astype(F32)
    bg = row(jnp.concatenate([p['moe_bg1'], p['moe_bg2'], jnp.zeros((nl, pad), F32)], axis=1))
    moe = dict(wg=wg, bg=bg, ln2_g=row(p['ln2_g']), ln2_b=row(p['ln2_b']))
    return dict(big=big, rw=rw, s5=s5, cv=cv, gm=gm, mg=mg, moe=moe)


def _gmlp_group(z3d, pg, layer, *, is_prompt):
    bsz, t_len, n_cols = z3d.shape
    if is_prompt:
        y, = _gmlp(z3d, P_GM // (2 * C_BR), pg, pg['wm'], pg['bias'], layer, emit_v=False)
        return y, None
    y, v = _gmlp(z3d.reshape(1, bsz * t_len, n_cols), P_GM // (2 * C_BR), pg, pg['wm_short'], pg['bias_short'],
                 layer, emit_v=True)
    return y.reshape(bsz, t_len, C_BR), v.reshape(bsz, t_len, C_BR)


def _run_group(x, wkv0, shift0, s5r0, s5i0, conv0, pr, *, is_prompt):
    bsz, t_len, _ = x.shape
    big = pr['big']
    n = bsz * t_len
    x2d = x.reshape(n, D_MODEL)
    outs = []
    conv_shape = conv0.shape
    if not is_prompt:
        conv0 = conv0.reshape(bsz, DEPTH * CONV_HIST * C_BR)
    conv_buf = jnp.zeros_like(conv0)
    if not is_prompt:
        wkv_buf = jnp.zeros_like(wkv0)
        grp_rows = lambda a: a.reshape((bsz // RW_GROUP, RW_GROUP * a.shape[1]) + a.shape[2:])
    for l in range(DEPTH):
        z2d = _inproj(x2d, big['w_mix'], l)
        z3d = z2d.reshape(bsz, t_len, N_MIX)
        sh0 = shift0[:, l].reshape(bsz, 1, RW_IN)
        if is_prompt:
            y_rw, wkv1 = _rwkv(z3d, P_RW // RW_IN, sh0, wkv0[:, l], pr['rw'], l, t_tile=RW_TILE, chunk=RW_CHUNK,
                               t_valid=t_len, carry=True, prec=PREC)
        else:
            z_rw = grp_rows(jnp.pad(z3d[:, :, P_RW:P_RW + RW_IN], ((0, 0), (0, RW_PAD - t_len), (0, 0))))
            prev0 = grp_rows(jnp.pad(sh0, ((0, 0), (0, RW_PAD - 1), (0, 0))))
            y_rw, wkv_buf = _rwkv(z_rw, 0, prev0, wkv0, pr['rw'], l, t_tile=RW_GROUP * RW_PAD, chunk=RW_PAD,
                                  t_valid=t_len, carry=False, prec=PREC_SHORT, wkv_out=wkv_buf)
            y_rw = y_rw.reshape(bsz, RW_PAD, C_BR)[:, :t_len]
            wkv1 = None
        shift1 = z3d[:, t_len - 1, P_RW:P_RW + RW_IN]
        h0 = jnp.concatenate([s5r0[:, l].reshape(bsz, S5_N), s5i0[:, l].reshape(bsz, S5_N)], axis=1)
        if is_prompt:
            y_s5, h1 = _s5(z3d, P_S5 // C_BR, h0, pr['s5'], l, bsz=bsz, t_len=t_len, t_tile=S5_TILE)
        else:
            y_s5, h1 = _s5(z2d, P_S5 // C_BR, h0, pr['s5'], l, bsz=bsz, t_len=t_len, t_tile=t_len)
        s5r1 = h1[:, :S5_N].reshape(bsz, S5_GROUPS, S5_STATE)
        s5i1 = h1[:, S5_N:].reshape(bsz, S5_GROUPS, S5_STATE)
        if is_prompt:
            y_cv, conv_buf = _conv(z3d, P_CV // (2 * C_BR), conv0, conv_buf, l, pr['cv'], t_tile=CV_TILE)
        else:
            y_cv, conv_buf = _conv_short(z2d, P_CV // (2 * C_BR), conv0, conv_buf, l, pr['cv'], bsz=bsz,
                                         t_len=t_len)
        y_gm, v_gm = _gmlp_group(z3d, pr['gm'], l, is_prompt=is_prompt)
        ys = [y.reshape(n, C_BR) for y in (y_rw, y_s5, y_cv, y_gm)]
        x2d = _merge(x2d, ys, big, pr['mg'], l)
        x2d = _moe(x2d, big, pr['moe'], l)
        outs.append((wkv1, shift1, s5r1, s5i1, v_gm))
    stack = lambda i: None if outs[0][i] is None else jnp.stack([o[i] for o in outs], axis=1)
    wkv = stack(0) if is_prompt else wkv_buf
    return (x2d.reshape(bsz, t_len, D_MODEL),
            (wkv, stack(1), stack(2), stack(3), conv_buf.reshape(conv_shape), stack(4)))


def kernel(x_prompt, x_sample, state_rwkv_wkv, state_rwkv_shift, state_s5_re, state_s5_im, cache_conv,
           w_in, rw_mu, rw_w0, rw_w2, rw_a0, rw_a2, rw_g2, rw_kk, rw_ka, rw_rk, rw_gn_g, rw_gn_b,
           s5_lam_re, s5_lam_im, s5_log_dt, s5_b_re, s5_b_im, s5_c_re, s5_c_im, s5_d, s5_glu_w, s5_glu_b,
           cv_w, cv_b, cv_ln_g, cv_ln_b, gm_ln_g, gm_ln_b, gm_ws, gm_bs,
           w_branch, w_out, ln1_g, ln1_b,
           moe_wg1, moe_bg1, moe_wg2, moe_bg2, moe_w_up, moe_w_down, ln2_g, ln2_b):
    p = dict(w_in=w_in, rw_mu=rw_mu, rw_w0=rw_w0, rw_w2=rw_w2, rw_a0=rw_a0, rw_a2=rw_a2, rw_g2=rw_g2,
             rw_kk=rw_kk, rw_ka=rw_ka, rw_rk=rw_rk, rw_gn_g=rw_gn_g, rw_gn_b=rw_gn_b,
             s5_lam_re=s5_lam_re, s5_lam_im=s5_lam_im, s5_log_dt=s5_log_dt, s5_b_re=s5_b_re, s5_b_im=s5_b_im,
             s5_c_re=s5_c_re, s5_c_im=s5_c_im, s5_d=s5_d, s5_glu_w=s5_glu_w, s5_glu_b=s5_glu_b,
             cv_w=cv_w, cv_b=cv_b, cv_ln_g=cv_ln_g, cv_ln_b=cv_ln_b, gm_ln_g=gm_ln_g, gm_ln_b=gm_ln_b,
             gm_ws=gm_ws, gm_bs=gm_bs, w_branch=w_branch, w_out=w_out, ln1_g=ln1_g, ln1_b=ln1_b,
             moe_wg1=moe_wg1, moe_bg1=moe_bg1, moe_wg2=moe_wg2, moe_bg2=moe_bg2, moe_w_up=moe_w_up,
             moe_w_down=moe_w_down, ln2_g=ln2_g, ln2_b=ln2_b)
    pr = _prep_params(p, x_sample.shape[1])
    bp = x_prompt.shape[0]
    dt = x_prompt.dtype
    y_prompt, (p_wkv, p_shift, p_s5r, p_s5i, p_conv, _) = _run_group(
        x_prompt,
        jnp.zeros((bp, DEPTH, RW_HEADS, RW_HEAD, RW_HEAD), dt),
        jnp.zeros((bp, DEPTH, RW_IN), dt),
        jnp.zeros((bp, DEPTH, S5_GROUPS, S5_STATE), dt),
        jnp.zeros((bp, DEPTH, S5_GROUPS, S5_STATE), dt),
        jnp.zeros((bp, DEPTH, CONV_HIST, C_BR), dt),
        pr, is_prompt=True)
    y_sample, (s_wkv, s_shift, s_s5r, s_s5i, s_conv, s_gmv) = _run_group(
        x_sample, state_rwkv_wkv, state_rwkv_shift, state_s5_re, state_s5_im, cache_conv, pr, is_prompt=False)
    return (y_prompt, y_sample, p_wkv, p_shift, p_s5r, p_s5i, p_conv,
            s_wkv, s_shift, s_s5r, s_s5i, s_conv, s_gmv)
```

```python
import functools
import math

import numpy as np
import jax
import jax.numpy as jnp
from jax import lax
from jax.experimental import pallas as pl
from jax.experimental.pallas import tpu as pltpu

D_MODEL = 1024
DEPTH = 4
N_BRANCH = 4
C_BR = D_MODEL // 4
RW_HEAD = 64
RW_HEADS = C_BR // RW_HEAD
RW_LW = 64
RW_LA = 64
RW_LG = 128
RW_IN = 3 * C_BR + RW_LW + RW_LA + RW_LG
RW_GN_EPS = 64e-5
RW_PAD = 8
RW_GROUP = 16
S5_GW = 16
S5_GROUPS = C_BR // S5_GW
S5_STATE = 64
S5_N = S5_GROUPS * S5_STATE
CONV_W = 31
CONV_HIST = CONV_W - 1
CONV_HIST_PAD = 32
CHUNK = 128
GM_TILE = 8 * CHUNK
GM_HEADS = 4
GM_HEAD = C_BR // GM_HEADS
N_GROUPS = 4
E_PER_GROUP = 4
N_EXPERTS = N_GROUPS * E_PER_GROUP
D_EXPERT = D_MODEL // 4
LN_EPS = 1e-5
DN_ALPHA = (2 * DEPTH) ** 0.25
OFF_S5 = RW_IN
OFF_CV = OFF_S5 + C_BR
OFF_GM = OFF_CV + 2 * C_BR
OFF_GATE = OFF_GM + 2 * C_BR
P_RW = 0
P_CV = P_RW + RW_IN
P_GM = P_CV + 2 * C_BR
P_S5 = P_GM + 2 * C_BR
N_MIX = P_S5 + C_BR

LANES = 128
SUBLANES = 8
TOKEN_TILE = 512
RW_TILE = 512
RW_CHUNK = 64
S5_TILE = 128
CV_TILE = 512
VMEM_LIMIT = 56 * 1024 * 1024

F32 = jnp.float32
BF16 = jnp.bfloat16
MM_DTYPE = jnp.bfloat16
Z_DTYPE = jnp.bfloat16
NEG_BIG = -1e30


def _mm(a, b):
    return jnp.dot(a.astype(MM_DTYPE), b.astype(MM_DTYPE), preferred_element_type=F32)


def _split_bf16(a):
    hi = a.astype(BF16)
    return hi, (a - hi.astype(F32)).astype(BF16)


_NN = ((1,), (0,))
_NT = ((1,), (1,))
_TN = ((0,), (0,))


def _dot(a, b, mode, dims=_NN, exact=None):
    dn = (dims, ((), ()))
    f = lambda x, y: lax.dot_general(x, y, dn, preferred_element_type=F32)
    if mode == 'bf16':
        return f(a.astype(BF16), b.astype(BF16))
    if mode == 'x2':
        mode, exact = 'x3', 'b'
    assert mode == 'x3'
    if exact == 'a':
        b_hi, b_lo = _split_bf16(b)
        a = a.astype(BF16)
        return f(a, b_hi) + f(a, b_lo)
    if exact == 'b':
        a_hi, a_lo = _split_bf16(a)
        b = b.astype(BF16)
        return f(a_hi, b) + f(a_lo, b)
    a_hi, a_lo = _split_bf16(a)
    b_hi, b_lo = _split_bf16(b)
    return f(a_hi, b_hi) + (f(a_hi, b_lo) + f(a_lo, b_hi))


PREC = dict(cumsum='x3', headsum='x3', amat='bf16', inv='bf16', state='bf16', apply='bf16', update='bf16', route='x2')
PREC_SHORT = dict(PREC, update='x3')


def _sigmoid(x):
    return jax.nn.sigmoid(x)


def _softplus(x):
    return jnp.maximum(x, 0.0) + jnp.log1p(jnp.exp(-jnp.abs(x)))


def _gelu_tanh(x):
    return 0.5 * x * (1.0 + jnp.tanh(math.sqrt(2.0 / math.pi) * (x + 0.044715 * (x * x * x))))


def _layer_norm(x, g, b):
    mu = jnp.mean(x, axis=-1, keepdims=True)
    d = x - mu
    var = jnp.mean(d * d, axis=-1, keepdims=True)
    return d * lax.rsqrt(var + LN_EPS) * g + b


def _params(sem):
    return pltpu.CompilerParams(dimension_semantics=sem, vmem_limit_bytes=VMEM_LIMIT)


def _full(shape):
    nd = len(shape)
    return pl.BlockSpec(shape, lambda *_: (0,) * nd)


def _state_buffer(buf):
    return [buf], [pl.BlockSpec(memory_space=pl.ANY)]


def _layer_block(shape, layer, **kwargs):
    nd = len(shape)
    return pl.BlockSpec((None,) + tuple(shape), lambda *_: (layer,) + (0,) * nd, **kwargs)


def _inproj_kernel(x_ref, w_ref, z_ref):
    z_ref[...] = _mm(x_ref[...], w_ref[...]).astype(z_ref.dtype)


def _inproj(x2d, w_bf16, layer):
    n = x2d.shape[0]
    tm = min(TOKEN_TILE, n)
    return pl.pallas_call(
        _inproj_kernel,
        grid=(n // tm,),
        in_specs=[pl.BlockSpec((tm, D_MODEL), lambda i: (i, 0)), _layer_block((D_MODEL, N_MIX), layer)],
        out_specs=pl.BlockSpec((tm, N_MIX), lambda i: (i, 0)),
        out_shape=jax.ShapeDtypeStruct((n, N_MIX), Z_DTYPE),
        compiler_params=_params(("parallel",)),
        name="inproj",
    )(x2d, w_bf16)


def _heads_bd(x, lane_head):
    return jnp.concatenate([jnp.where(lane_head == h, x, 0.0) for h in range(RW_HEADS)], axis=0)


def _state_bd(wkv):
    zeros_blk = jnp.zeros((RW_HEAD, RW_HEAD), F32)
    return jnp.concatenate(
        [jnp.concatenate([wkv[h] if g == h else zeros_blk for g in range(RW_HEADS)], axis=1)
         for h in range(RW_HEADS)], axis=0)


def _rwkv_prep(z, z_prev, valid, mu_ref, w0_ref, w2_ref, a0_ref, a2_ref, g2_ref, kkw_ref, kaw_ref, hsum, tri_ref,
               chunk, prec):
    zs = z + mu_ref[...] * (z_prev - z)
    r = zs[:, 0:C_BR]
    k = zs[:, C_BR:2 * C_BR]
    v = zs[:, 2 * C_BR:3 * C_BR]
    lwla = zs[:, 3 * C_BR:3 * C_BR + RW_LW + RW_LA]
    lg = zs[:, 3 * C_BR + RW_LW + RW_LA:]
    w_log = -_softplus(-(w0_ref[...] + _mm(jnp.tanh(lwla), w2_ref[...]))) - 0.5
    ld = -jnp.exp(w_log)
    a = _sigmoid(a0_ref[...] + _mm(lwla, a2_ref[...]))
    g = _mm(_sigmoid(lg), g2_ref[...])
    kk = k * kkw_ref[...]
    kk = kk * lax.rsqrt(jnp.maximum(_dot(kk * kk, hsum, prec['headsum'], exact='b'), 1e-24))
    k2 = k * (1.0 + (a - 1.0) * kaw_ref[...])
    bv = kk * a
    if valid is not None:
        ld = jnp.where(valid, ld, 0.0)
        k2 = jnp.where(valid, k2, 0.0)
        v = jnp.where(valid, v, 0.0)
        bv = jnp.where(valid, bv, 0.0)
    t_tile = z.shape[0]
    n = tri_ref.shape[0]
    lc = jnp.concatenate([_dot(tri_ref[...], ld[i * n:(i + 1) * n], prec['cumsum'], exact='a')
                          for i in range(t_tile // n)], axis=0)
    lend = jnp.concatenate([jnp.broadcast_to(lc[(c + 1) * chunk - 1:(c + 1) * chunk], (chunk, C_BR))
                            for c in range(t_tile // chunk)], axis=0)
    e_end = jnp.exp(lend - lc)
    e_neg = jnp.exp(-lc)
    return dict(r=r, k2=k2, v=v, g=g, rt=r * jnp.exp(lc), kkt=kk * jnp.exp(lc - ld), kh=k2 * e_neg, bh=bv * e_neg,
                kw=k2 * e_end, bw=bv * e_end, wc=jnp.exp(lend))


def _rwkv_chunks_local(chunks, strict_ref, incl_ref, lvl_ref, chunk, prec):
    hc = RW_HEADS * chunk
    nk = RW_HEADS * RW_HEAD
    n = range(len(chunks))
    kkt, rt, kh, bh, vv, kw, bw = (list(x) for x in zip(*chunks))
    amat = [_dot(jnp.concatenate([kkt[c], rt[c]], axis=0), jnp.concatenate([kh[c], bh[c]], axis=0),
                 prec['amat'], _NT) for c in n]
    strict = strict_ref[...] != 0.0
    incl = incl_ref[...] != 0.0
    a_kk = [jnp.where(strict, amat[c][0:hc, 0:hc], 0.0) for c in n]
    a_kb = [jnp.where(strict, amat[c][0:hc, hc:2 * hc], 0.0) for c in n]
    a_rk = [jnp.where(incl, amat[c][hc:2 * hc, 0:hc], 0.0) for c in n]
    a_rb = [jnp.where(incl, amat[c][hc:2 * hc, hc:2 * hc], 0.0) for c in n]
    av = [_dot(jnp.concatenate([a_kk[c], a_rk[c]], axis=0), vv[c], prec['apply']) for c in n]
    ri = lax.broadcasted_iota(jnp.int32, (hc, hc), 0)
    cj = lax.broadcasted_iota(jnp.int32, (hc, hc), 1)
    eye = jnp.where(ri == cj, 1.0, 0.0)
    lvl0 = lvl_ref[0] != 0.0
    t_inv = [eye - jnp.where(lvl0, a_kb[c], 0.0) for c in n]
    for lv in range(1, lvl_ref.shape[0]):
        lvl = lvl_ref[lv] != 0.0
        half = [_dot(t_inv[c], jnp.where(lvl, a_kb[c], 0.0), prec['inv']) for c in n]
        t_inv = [t_inv[c] - _dot(half[c], t_inv[c], prec['inv']) for c in n]
    gu = [_dot(t_inv[c], jnp.concatenate([kkt[c], av[c][0:hc]], axis=1), prec['apply']) for c in n]
    pu = [_dot(a_rb[c], gu[c], prec['apply']) for c in n]
    mc = [_dot(gu[c][:, 0:nk], bw[c], prec['update'], _TN) for c in n]
    nn = [_dot(jnp.concatenate([vv[c], gu[c][:, nk:2 * nk]], axis=0), jnp.concatenate([kw[c], -bw[c]], axis=0),
               prec['update'], _TN) for c in n]
    return [(rt[c] - pu[c][:, 0:nk], av[c][hc:2 * hc] - pu[c][:, nk:2 * nk], mc[c], nn[c]) for c in n]


def _rwkv_chunks_direct(chunks, states, wcs, strict_ref, incl_ref, lvl_ref, chunk, prec):
    hc = RW_HEADS * chunk
    n = range(len(chunks))
    kkt, rt, kh, bh, vv, kw, bw = (list(x) for x in zip(*chunks))
    lhs = [jnp.concatenate([kkt[c], rt[c]], axis=0) for c in n]
    amat = [_dot(lhs[c], jnp.concatenate([kh[c], bh[c]], axis=0), prec['amat'], _NT) for c in n]
    ls = [_dot(lhs[c], states[c], prec['state'], _NT) for c in n]
    strict = strict_ref[...] != 0.0
    incl = incl_ref[...] != 0.0
    a_kk = [jnp.where(strict, amat[c][0:hc, 0:hc], 0.0) for c in n]
    a_kb = [jnp.where(strict, amat[c][0:hc, hc:2 * hc], 0.0) for c in n]
    a_rk = [jnp.where(incl, amat[c][hc:2 * hc, 0:hc], 0.0) for c in n]
    a_rb = [jnp.where(incl, amat[c][hc:2 * hc, hc:2 * hc], 0.0) for c in n]
    av = [_dot(jnp.concatenate([a_kk[c], a_rk[c]], axis=0), vv[c], prec['apply']) for c in n]
    ri = lax.broadcasted_iota(jnp.int32, (hc, hc), 0)
    cj = lax.broadcasted_iota(jnp.int32, (hc, hc), 1)
    eye = jnp.where(ri == cj, 1.0, 0.0)
    lvl0 = lvl_ref[0] != 0.0
    t_inv = [eye - jnp.where(lvl0, a_kb[c], 0.0) for c in n]
    for lv in range(1, lvl_ref.shape[0]):
        lvl = lvl_ref[lv] != 0.0
        half = [_dot(t_inv[c], jnp.where(lvl, a_kb[c], 0.0), prec['inv']) for c in n]
        t_inv = [t_inv[c] - _dot(half[c], t_inv[c], prec['inv']) for c in n]
    u = [_dot(t_inv[c], ls[c][0:hc] + av[c][0:hc], prec['apply']) for c in n]
    o = [ls[c][hc:2 * hc] + av[c][hc:2 * hc] - _dot(a_rb[c], u[c], prec['apply']) for c in n]
    s_new = [states[c] * wcs[c] + _dot(jnp.concatenate([vv[c], u[c]], axis=0),
                                       jnp.concatenate([kw[c], -bw[c]], axis=0), prec['update'], _TN) for c in n]
    out = []
    for c in n:
        o_c = o[c][0:chunk]
        for h in range(1, RW_HEADS):
            o_c = o_c + o[c][h * chunk:(h + 1) * chunk]
        out.append((o_c, s_new[c]))
    return out


def _rwkv_chunk_apply(s, local, wc, chunk, prec):
    p, o0, mc, nn = local
    o = _dot(p, s, prec['state'], _NT) + o0
    o_c = o[0:chunk]
    for h in range(1, RW_HEADS):
        o_c = o_c + o[h * chunk:(h + 1) * chunk]
    return o_c, s * wc - _dot(s, mc, prec['state']) + nn


def _rwkv_post(o, pre, rk_ref, gng_ref, gnb_ref, hsum, prec):
    inv_n = 1.0 / RW_HEAD
    o_mu = _dot(o, hsum, prec['headsum'], exact='b') * inv_n
    od = o - o_mu
    o_var = _dot(od * od, hsum, prec['headsum'], exact='b') * inv_n
    on = od * lax.rsqrt(o_var + RW_GN_EPS) * gng_ref[...] + gnb_ref[...]
    bonus = _dot(pre['r'] * pre['k2'] * rk_ref[...], hsum, prec['headsum'], exact='b') * pre['v']
    return (on + bonus) * pre['g']


_RWKV_LOCAL_KEYS = ('kkt', 'rt', 'kh', 'bh', 'v', 'kw', 'bw')


def _rwkv_kernel(z_ref, prev0_ref, wkv0_ref, mu_ref, w0_ref, w2_ref, a0_ref, a2_ref, g2_ref, kkw_ref, kaw_ref,
                 rk_ref, gng_ref, gnb_ref, hsum_ref, tri_ref, strict_ref, incl_ref, lvl_ref, *rest,
                 t_tile, chunk, t_valid, n_tiles, carry, prec):
    y_ref, wkv1_ref, s_scr, prev_scr = rest[-4:]
    i = pl.program_id(1)
    z = z_ref[...].astype(F32)
    row = lax.broadcasted_iota(jnp.int32, (t_tile, 1), 0)
    if carry:
        assert t_valid == t_tile * n_tiles

        @pl.when(i == 0)
        def _():
            s_scr[...] = _state_bd(wkv0_ref)
            prev_scr[...] = prev0_ref[...]

        z_prev = jnp.where(row == 0, prev_scr[...], pltpu.roll(z, 1, 0))
        prev_scr[...] = z[t_tile - 1:t_tile, :]
        valid = None
    else:
        step = row % chunk
        z_prev = jnp.where(step == 0, prev0_ref[...], pltpu.roll(z, 1, 0))
        valid = step < t_valid
    hsum = hsum_ref[...]
    pre = _rwkv_prep(z, z_prev, valid, mu_ref, w0_ref, w2_ref, a0_ref, a2_ref, g2_ref, kkw_ref, kaw_ref, hsum,
                     tri_ref, chunk, prec)
    lane_head = lax.broadcasted_iota(jnp.int32, (chunk, C_BR), 1) // RW_HEAD
    n_chunks = t_tile // chunk
    chunks = [tuple(_heads_bd(pre[key][c * chunk:(c + 1) * chunk], lane_head) for key in _RWKV_LOCAL_KEYS)
              for c in range(n_chunks)]
    wcs = [pre['wc'][c * chunk:c * chunk + 1] for c in range(n_chunks)]
    o_rows = []
    if carry:
        local = _rwkv_chunks_local(chunks, strict_ref, incl_ref, lvl_ref, chunk, prec)
        s = s_scr[...]
        for c in range(n_chunks):
            o_c, s = _rwkv_chunk_apply(s, local[c], wcs[c], chunk, prec)
            o_rows.append(o_c)
    else:
        states = [_state_bd(wkv0_ref.at[c]) for c in range(n_chunks)]
        for c, (o_c, s_c) in enumerate(_rwkv_chunks_direct(chunks, states, wcs, strict_ref, incl_ref, lvl_ref,
                                                           chunk, prec)):
            o_rows.append(o_c)
            for h in range(RW_HEADS):
                wkv1_ref[c, h] = s_c[h * RW_HEAD:(h + 1) * RW_HEAD, h * RW_HEAD:(h + 1) * RW_HEAD]
    y_ref[...] = _rwkv_post(jnp.concatenate(o_rows, axis=0), pre, rk_ref, gng_ref, gnb_ref, hsum, prec)

    if carry:
        s_scr[...] = s

        @pl.when(i == n_tiles - 1)
        def _():
            for h in range(RW_HEADS):
                wkv1_ref[h] = s[h * RW_HEAD:(h + 1) * RW_HEAD, h * RW_HEAD:(h + 1) * RW_HEAD]


def _rwkv(z3d, col_blk, prev0, wkv0, pw, layer, *, t_tile, chunk, t_valid, carry, prec, wkv_out=None):
    bsz, t_len, _ = z3d.shape
    n_tiles = t_len // t_tile
    cs = min(t_tile, max(chunk, 64))
    assert t_tile % cs == 0 and cs % chunk == 0
    idx = np.arange(cs)
    tri = jnp.asarray(((idx[:, None] // chunk == idx[None, :] // chunk)
                       & (idx[None, :] <= idx[:, None])).astype(np.float32))
    hid = np.arange(C_BR) // RW_HEAD
    hsum = jnp.asarray((hid[:, None] == hid[None, :]).astype(np.float32))
    hc = RW_HEADS * chunk
    hh, tt = np.arange(hc) // chunk, np.arange(hc) % chunk
    same_head = hh[:, None] == hh[None, :]
    strict = jnp.asarray((same_head & (tt[None, :] < tt[:, None])).astype(np.float32))
    incl = jnp.asarray((same_head & (tt[None, :] <= tt[:, None])).astype(np.float32))
    lvls = []
    m = 1
    while m < chunk:
        lvls.append(same_head & (tt[:, None] // (2 * m) == tt[None, :] // (2 * m))
                    & (tt[:, None] % (2 * m) >= m) & (tt[None, :] % (2 * m) < m))
        m *= 2
    lvl = jnp.asarray(np.stack(lvls).astype(np.float32))
    lb = lambda shape: _layer_block(shape, layer)
    vec = lambda n: lb((1, n))
    if carry:
        prev_spec = pl.BlockSpec((None, 1, RW_IN), lambda b, i: (b, 0, 0))
        wkv_spec = pl.BlockSpec((None, RW_HEADS, RW_HEAD, RW_HEAD), lambda b, i: (b, 0, 0, 0))
        extra_in, extra_specs = [], []
    else:
        assert n_tiles == 1
        prev_spec = pl.BlockSpec((None, t_tile, RW_IN), lambda b, i: (b, 0, 0))
        wkv_spec = pl.BlockSpec((t_tile // chunk, None, RW_HEADS, RW_HEAD, RW_HEAD),
                                lambda b, i: (b, layer, 0, 0, 0))
        extra_in, extra_specs = _state_buffer(wkv_out)
    operands = [z3d, prev0, wkv0, pw['mu'], pw['w0'], pw['w2'], pw['a0'], pw['a2'], pw['g2'], pw['kk'], pw['ka'],
                pw['rk'], pw['gn_g'], pw['gn_b'], hsum, tri, strict, incl, lvl] + extra_in
    aliases = {len(operands) - 1: 1} if extra_in else {}
    kern = functools.partial(_rwkv_kernel, t_tile=t_tile, chunk=chunk, t_valid=t_valid, n_tiles=n_tiles, carry=carry,
                             prec=prec)
    return pl.pallas_call(
        kern,
        grid=(bsz, n_tiles),
        in_specs=[pl.BlockSpec((None, t_tile, RW_IN), lambda b, i: (b, i, col_blk)), prev_spec, wkv_spec,
                  vec(RW_IN), vec(C_BR), lb((RW_LW + RW_LA, C_BR)), vec(C_BR), lb((RW_LW + RW_LA, C_BR)),
                  lb((RW_LG, C_BR)), vec(C_BR), vec(C_BR), vec(C_BR), vec(C_BR), vec(C_BR),
                  _full((C_BR, C_BR)), _full((cs, cs)), _full((hc, hc)), _full((hc, hc)), _full(lvl.shape)]
        + extra_specs,
        out_specs=[pl.BlockSpec((None, t_tile, C_BR), lambda b, i: (b, i, 0)), wkv_spec],
        out_shape=[jax.ShapeDtypeStruct((bsz, t_len, C_BR), F32), jax.ShapeDtypeStruct(wkv0.shape, F32)],
        input_output_aliases=aliases,
        scratch_shapes=[pltpu.VMEM((C_BR, C_BR), F32), pltpu.VMEM((1, RW_IN), F32)],
        compiler_params=_params(("parallel", "arbitrary")),
        name="rwkv7",
    )(*operands)


def _s5_kernel(u_ref, h0_ref, lbr_ref, lbi_ref, bb_ref, cc_ref, d_ref, gw_ref, gb_ref,
               y_ref, h1_ref, bu_scr, h_scr, *relayout_scr, bsz, t_tile, n_tiles, time_major):
    i = pl.program_id(0)

    @pl.when(i == 0)
    def _():
        h_scr[...] = h0_ref[...]

    rows = bsz * t_tile
    n_lt = S5_N // LANES
    n_ut = C_BR // LANES
    lane_tile = lambda ref, j: ref[:, j * LANES:(j + 1) * LANES]
    seq_rows = lambda t: pl.ds(t, bsz, stride=t_tile)
    step_rows = lambda t: pl.ds(pl.multiple_of(t * bsz, bsz), bsz)
    u = u_ref[...].astype(F32).reshape(rows, C_BR)
    if time_major:
        bt_scr, tm_scr = relayout_scr
        for j in range(n_ut):
            bt_scr[j] = lane_tile(u, j)

        def to_time_major(t, _):
            for j in range(n_ut):
                tm_scr[j, step_rows(t), :] = bt_scr[j, seq_rows(t), :]
            return 0

        lax.fori_loop(0, t_tile, to_time_major, 0, unroll=4)
        u = jnp.concatenate([tm_scr[j] for j in range(n_ut)], axis=1)
        sl_of = step_rows
    else:
        sl_of = seq_rows
    bu = _mm(u, bb_ref[...])
    for j in range(2 * n_lt):
        bu_scr[j] = lane_tile(bu, j)
    lbr = [jnp.broadcast_to(lane_tile(lbr_ref, j), (bsz, LANES)) for j in range(n_lt)]
    lbi = [jnp.broadcast_to(lane_tile(lbi_ref, j), (bsz, LANES)) for j in range(n_lt)]

    def step(t, carry):
        hr, hi = carry
        sl = sl_of(t)
        new_r, new_i = [], []
        for j in range(n_lt):
            nr = lbr[j] * hr[j] - lbi[j] * hi[j] + bu_scr[j, sl, :]
            ni = lbr[j] * hi[j] + lbi[j] * hr[j] + bu_scr[n_lt + j, sl, :]
            bu_scr[j, sl, :] = nr
            bu_scr[n_lt + j, sl, :] = ni
            new_r.append(nr)
            new_i.append(ni)
        return tuple(new_r), tuple(new_i)

    h_init = (tuple(lane_tile(h_scr, j) for j in range(n_lt)),
              tuple(lane_tile(h_scr, n_lt + j) for j in range(n_lt)))
    hr, hi = lax.fori_loop(0, t_tile, step, h_init, unroll=2)
    for j in range(n_lt):
        h_scr[:, j * LANES:(j + 1) * LANES] = hr[j]
        h_scr[:, (n_lt + j) * LANES:(n_lt + j + 1) * LANES] = hi[j]

    hs = jnp.concatenate([bu_scr[j] for j in range(2 * n_lt)], axis=1)
    y = _mm(hs, cc_ref[...]) + d_ref[...] * u
    y = _gelu_tanh(y)
    y = y * _sigmoid(_mm(y, gw_ref[...]) + gb_ref[...])
    if time_major:
        for j in range(n_ut):
            tm_scr[j] = lane_tile(y, j)

        def to_seq_major(t, _):
            for j in range(n_ut):
                bt_scr[j, seq_rows(t), :] = tm_scr[j, step_rows(t), :]
            return 0

        lax.fori_loop(0, t_tile, to_seq_major, 0, unroll=4)
        y = jnp.concatenate([bt_scr[j] for j in range(n_ut)], axis=1)
    y_ref[...] = y.reshape(y_ref.shape)

    @pl.when(i == n_tiles - 1)
    def _():
        h1_ref[...] = h_scr[...]


def _s5(z, col_blk, h0, ps, layer, *, bsz, t_len, t_tile):
    lb = lambda shape: _layer_block(shape, layer)
    n_tiles = t_len // t_tile
    rows = bsz * t_tile
    if z.ndim == 3:
        u_spec = pl.BlockSpec((bsz, t_tile, C_BR), lambda i: (0, i, col_blk))
        y_spec = pl.BlockSpec((bsz, t_tile, C_BR), lambda i: (0, i, 0))
        y_shape = (bsz, t_len, C_BR)
    else:
        assert n_tiles == 1
        u_spec = pl.BlockSpec((rows, C_BR), lambda i: (0, col_blk))
        y_spec = pl.BlockSpec((rows, C_BR), lambda i: (0, 0))
        y_shape = (rows, C_BR)
    time_major = z.ndim == 3 and bsz == SUBLANES
    relayout_scr = [pltpu.VMEM((C_BR // LANES, rows, LANES), F32)] * 2 if time_major else []
    kern = functools.partial(_s5_kernel, bsz=bsz, t_tile=t_tile, n_tiles=n_tiles, time_major=time_major)
    return pl.pallas_call(
        kern,
        grid=(n_tiles,),
        in_specs=[u_spec, _full((bsz, 2 * S5_N)), lb((1, S5_N)), lb((1, S5_N)),
                  lb((C_BR, 2 * S5_N)), lb((2 * S5_N, C_BR)), lb((1, C_BR)), lb((C_BR, C_BR)), lb((1, C_BR))],
        out_specs=[y_spec, _full((bsz, 2 * S5_N))],
        out_shape=[jax.ShapeDtypeStruct(y_shape, F32), jax.ShapeDtypeStruct((bsz, 2 * S5_N), F32)],
        scratch_shapes=[pltpu.VMEM((2 * S5_N // LANES, rows, LANES), F32), pltpu.VMEM((bsz, 2 * S5_N), F32)]
        + relayout_scr,
        compiler_params=_params(("arbitrary",)),
        name="s5",
    )(z, h0, ps['lb_re'], ps['lb_im'], ps['bb'], ps['cc'], ps['d'], ps['glu_w'], ps['glu_b'])


def _conv_taps(full_scr, w_ref, t_tile):
    lo = CONV_HIST_PAD - CONV_HIST
    sub = SUBLANES
    assert t_tile % sub == 0
    acc = None
    for rho in range(sub):
        offs = [o for o in range(rho, lo + CONV_W, sub) if o >= lo]
        rows = t_tile + (sub if rho else 0)
        part = None
        for o in offs:
            term = full_scr[o - rho:o - rho + rows, :] * w_ref[o - lo:o - lo + 1, :]
            part = term if part is None else part + term
        part = part[rho:rho + t_tile]
        acc = part if acc is None else acc + part
    return acc


def _conv_kernel(z_ref, c0_ref, w_ref, b_ref, g_ref, be_ref, *rest, t_tile, n_tiles):
    y_ref, c1_ref, full_scr = rest[-3:]
    i = pl.program_id(1)
    lo = CONV_HIST_PAD - CONV_HIST

    @pl.when(i == 0)
    def _():
        full_scr[0:lo, :] = jnp.zeros((lo, C_BR), F32)
        full_scr[lo:CONV_HIST_PAD, :] = c0_ref[...]

    z = z_ref[...].astype(F32)
    full_scr[CONV_HIST_PAD:CONV_HIST_PAD + t_tile, :] = z[:, 0:C_BR] * _sigmoid(z[:, C_BR:2 * C_BR])
    y = _layer_norm(_conv_taps(full_scr, w_ref, t_tile) + b_ref[...], g_ref[...], be_ref[...])
    y_ref[...] = y * _sigmoid(y)
    hist = full_scr[t_tile:t_tile + CONV_HIST_PAD, :]
    full_scr[0:CONV_HIST_PAD, :] = hist

    @pl.when(i == n_tiles - 1)
    def _():
        c1_ref[...] = hist[lo:, :]


def _conv(z3d, col_blk, conv0, conv_out, layer, pc, *, t_tile):
    bsz, t_len, _ = z3d.shape
    n_tiles = t_len // t_tile
    state_spec = pl.BlockSpec((None, None, CONV_HIST, C_BR), lambda b, i: (b, layer, 0, 0))
    extra_in, extra_specs = _state_buffer(conv_out)
    kern = functools.partial(_conv_kernel, t_tile=t_tile, n_tiles=n_tiles)
    return pl.pallas_call(
        kern,
        grid=(bsz, n_tiles),
        in_specs=[pl.BlockSpec((None, t_tile, 2 * C_BR), lambda b, i: (b, i, col_blk)), state_spec,
                  _layer_block((CONV_W, C_BR), layer), _layer_block((1, C_BR), layer),
                  _layer_block((1, C_BR), layer), _layer_block((1, C_BR), layer)] + extra_specs,
        out_specs=[pl.BlockSpec((None, t_tile, C_BR), lambda b, i: (b, i, 0)), state_spec],
        out_shape=[jax.ShapeDtypeStruct((bsz, t_len, C_BR), F32), jax.ShapeDtypeStruct(conv0.shape, F32)],
        input_output_aliases={6: 1},
        scratch_shapes=[pltpu.VMEM((CONV_HIST_PAD + t_tile, C_BR), F32)],
        compiler_params=_params(("parallel", "arbitrary")),
        name="conv",
    )(z3d, conv0, pc['w'], pc['b'], pc['ln_g'], pc['ln_b'], *extra_in)


def _conv_short_kernel(z_ref, c0_ref, w_ref, b_ref, g_ref, be_ref, *rest, bsz, t_len):
    y_ref, c1_ref, in_scr, out_scr = rest[-4:]
    n_lt = C_BR // LANES
    z = z_ref[...].astype(F32)
    c = z[:, 0:C_BR] * _sigmoid(z[:, C_BR:2 * C_BR])
    for j in range(n_lt):
        in_scr[j] = c[:, j * LANES:(j + 1) * LANES]
    step_rows = lambda t: pl.ds(t, bsz, stride=t_len)
    hist = lambda r: c0_ref[:, r * C_BR:(r + 1) * C_BR]
    new = [jnp.concatenate([in_scr[j, step_rows(t), :] for j in range(n_lt)], axis=1) for t in range(t_len)]
    full = lambda r: hist(r) if r < CONV_HIST else new[r - CONV_HIST]
    for t in range(t_len):
        acc = b_ref[...] + full(t) * w_ref[0:1, :]
        for j in range(1, CONV_W):
            acc = acc + full(t + j) * w_ref[j:j + 1, :]
        y = _layer_norm(acc, g_ref[...], be_ref[...])
        y = y * _sigmoid(y)
        for j in range(n_lt):
            out_scr[j, step_rows(t), :] = y[:, j * LANES:(j + 1) * LANES]
    y_ref[...] = jnp.concatenate([out_scr[j] for j in range(n_lt)], axis=1)
    for r in range(CONV_HIST):
        c1_ref[:, r * C_BR:(r + 1) * C_BR] = full(r + t_len)


def _conv_short(z2d, col_blk, conv0, conv_out, layer, pc, *, bsz, t_len):
    rows = bsz * t_len
    width = CONV_HIST * C_BR
    state_spec = pl.BlockSpec((bsz, width), lambda i: (0, layer))
    extra_in, extra_specs = _state_buffer(conv_out)
    kern = functools.partial(_conv_short_kernel, bsz=bsz, t_len=t_len)
    return pl.pallas_call(
        kern,
        grid=(1,),
        in_specs=[pl.BlockSpec((rows, 2 * C_BR), lambda i: (0, col_blk)), state_spec,
                  _layer_block((CONV_W, C_BR), layer), _layer_block((1, C_BR), layer),
                  _layer_block((1, C_BR), layer), _layer_block((1, C_BR), layer)] + extra_specs,
        out_specs=[pl.BlockSpec((rows, C_BR), lambda i: (0, 0)), state_spec],
        out_shape=[jax.ShapeDtypeStruct((rows, C_BR), F32), jax.ShapeDtypeStruct(conv0.shape, F32)],
        input_output_aliases={6: 1},
        scratch_shapes=[pltpu.VMEM((C_BR // LANES, rows, LANES), F32)] * 2,
        compiler_params=_params(("arbitrary",)),
        name="conv_short",
    )(z2d, conv0, pc['w'], pc['b'], pc['ln_g'], pc['ln_b'], *extra_in)


def _gmlp_kernel(z_ref, g_ref, b_ref, wm_ref, bias_ref, y_ref, *v_ref):
    z = z_ref[...].astype(F32)
    u = z[:, 0:C_BR]
    v = _layer_norm(z[:, C_BR:2 * C_BR], g_ref[...], b_ref[...])
    if v_ref:
        v_ref[0][...] = v
    vb = v.astype(MM_DTYPE)
    head = lax.broadcasted_iota(jnp.int32, (CHUNK, C_BR), 1) // GM_HEAD
    for c in range(z.shape[0] // CHUNK):
        rows = slice(c * CHUNK, (c + 1) * CHUNK)
        s = bias_ref[...]
        for h in range(GM_HEADS):
            s = s + jnp.where(head == h, jnp.dot(wm_ref[h], vb[rows], preferred_element_type=F32), 0.0)
        y_ref[rows, :] = u[rows] * s


def _gmlp(z3d, col_blk, pg, wm, bias, layer, *, emit_v):
    lb = lambda shape: _layer_block(shape, layer)
    bsz, t_len, _ = z3d.shape
    tile = min(GM_TILE, t_len)
    assert t_len % tile == 0 and tile % CHUNK == 0
    out_spec = pl.BlockSpec((None, tile, C_BR), lambda b, i: (b, i, 0))
    n_out = 2 if emit_v else 1
    return pl.pallas_call(
        _gmlp_kernel,
        grid=(bsz, t_len // tile),
        in_specs=[pl.BlockSpec((None, tile, 2 * C_BR), lambda b, i: (b, i, col_blk)),
                  lb((1, C_BR)), lb((1, C_BR)), lb((GM_HEADS, CHUNK, CHUNK)), lb((CHUNK, C_BR))],
        out_specs=[out_spec] * n_out,
        out_shape=[jax.ShapeDtypeStruct((bsz, t_len, C_BR), F32)] * n_out,
        compiler_params=_params(("parallel", "parallel")),
        name="gmlp",
    )(z3d, pg['ln_g'], pg['ln_b'], wm, bias)


def _merge_kernel(x_ref, yrw_ref, ys5_ref, ycv_ref, ygm_ref, wg_ref, wb_ref, wo_ref, g_ref, b_ref, o_ref):
    tm = x_ref.shape[0]
    halves = [slice(0, tm // 2), slice(tm // 2, tm)]
    x = [x_ref[h, :] for h in halves]
    xb = [v.astype(MM_DTYPE) for v in x]
    merged = [None for _ in halves]
    for bidx, y_ref in enumerate((yrw_ref, ys5_ref, ycv_ref, ygm_ref)):
        wg = wg_ref[:, bidx * D_MODEL:(bidx + 1) * D_MODEL]
        gate = [_sigmoid(jnp.dot(v, wg, preferred_element_type=F32)) for v in xb]
        term = [gate[i] * _mm(y_ref[h, :], wb_ref[bidx]) for i, h in enumerate(halves)]
        merged = [t if m is None else m + t for m, t in zip(merged, term)]
    proj = [_mm(m, wo_ref[...]) for m in merged]
    for i, h in enumerate(halves):
        o_ref[h, :] = _layer_norm(DN_ALPHA * x[i] + proj[i], g_ref[...], b_ref[...])


def _merge(x2d, ys, big, pm, layer):
    n = x2d.shape[0]
    tm = min(TOKEN_TILE, n)
    row = lambda w: pl.BlockSpec((tm, w), lambda i: (i, 0))
    return pl.pallas_call(
        _merge_kernel,
        grid=(n // tm,),
        in_specs=[row(D_MODEL), row(C_BR), row(C_BR), row(C_BR), row(C_BR),
                  _layer_block((D_MODEL, N_BRANCH * D_MODEL), layer), _layer_block((N_BRANCH, C_BR, D_MODEL), layer),
                  _layer_block((D_MODEL, D_MODEL), layer), _layer_block((1, D_MODEL), layer),
                  _layer_block((1, D_MODEL), layer)],
        out_specs=row(D_MODEL),
        out_shape=jax.ShapeDtypeStruct((n, D_MODEL), F32),
        compiler_params=_params(("parallel",)),
        name="merge",
    )(x2d, *ys, big['w_gate'], big['w_branch'], big['w_out'], pm['ln1_g'], pm['ln1_b'])


def _moe_kernel(x_ref, wg_ref, bg_ref, wu_ref, wd_ref, g_ref, b_ref, o_ref, hh_scr, *, tm):
    lane = lax.broadcasted_iota(jnp.int32, (tm, LANES), 1)
    x = x_ref[...]
    xb = x.astype(MM_DTYPE)
    logits = _dot(x, wg_ref[...], PREC['route']) + bg_ref[...]
    gl = jnp.where(lane < N_GROUPS, logits, NEG_BIG)
    gmax = jnp.max(gl, axis=-1, keepdims=True)
    g_sel = jnp.min(jnp.where(gl == gmax, lane, LANES), axis=-1, keepdims=True)
    p_group = 1.0 / jnp.sum(jnp.where(lane < N_GROUPS, jnp.exp(gl - gmax), 0.0), axis=-1, keepdims=True)
    first = N_GROUPS + g_sel * E_PER_GROUP
    el = jnp.where((lane >= first) & (lane < first + E_PER_GROUP), logits, NEG_BIG)
    m1 = jnp.max(el, axis=-1, keepdims=True)
    i1 = jnp.min(jnp.where(el == m1, lane, LANES), axis=-1, keepdims=True)
    el2 = jnp.where(lane == i1, NEG_BIG, el)
    m2 = jnp.max(el2, axis=-1, keepdims=True)
    i2 = jnp.min(jnp.where(el2 == m2, lane, LANES), axis=-1, keepdims=True)
    e2 = jnp.exp(m2 - m1)
    w1 = p_group / (1.0 + e2)
    w2 = p_group * e2 / (1.0 + e2)
    for e in range(N_EXPERTS):
        comb_e = jnp.where(i1 == e + N_GROUPS, w1, 0.0) + jnp.where(i2 == e + N_GROUPS, w2, 0.0)
        h = jnp.dot(xb, wu_ref[e], preferred_element_type=F32)
        h1 = h[:, 0:D_EXPERT]
        hh = h1 * _sigmoid(h1) * h[:, D_EXPERT:2 * D_EXPERT] * comb_e
        hh_scr[:, e * D_EXPERT:(e + 1) * D_EXPERT] = hh.astype(MM_DTYPE)
    moe = jnp.dot(hh_scr[...], wd_ref[...], preferred_element_type=F32)
    o_ref[...] = _layer_norm(DN_ALPHA * x + moe, g_ref[...], b_ref[...])


def _moe(x2d, big, pe, layer):
    n = x2d.shape[0]
    tm = min(TOKEN_TILE, n)
    kern = functools.partial(_moe_kernel, tm=tm)
    resident = lambda shape: _layer_block(shape, layer, pipeline_mode=pl.Buffered(1))
    return pl.pallas_call(
        kern,
        grid=(n // tm,),
        in_specs=[pl.BlockSpec((tm, D_MODEL), lambda i: (i, 0)),
                  _layer_block((D_MODEL, LANES), layer), _layer_block((1, LANES), layer),
                  resident((N_EXPERTS, D_MODEL, 2 * D_EXPERT)), resident((N_EXPERTS * D_EXPERT, D_MODEL)),
                  _layer_block((1, D_MODEL), layer), _layer_block((1, D_MODEL), layer)],
        out_specs=pl.BlockSpec((tm, D_MODEL), lambda i: (i, 0)),
        out_shape=jax.ShapeDtypeStruct((n, D_MODEL), F32),
        scratch_shapes=[pltpu.VMEM((tm, N_EXPERTS * D_EXPERT), MM_DTYPE)],
        compiler_params=_params(("parallel",)),
        name="moe",
    )(x2d, pe['wg'], pe['bg'], big['w_up'], big['w_down'], pe['ln2_g'], pe['ln2_b'])


def _block_diag(blocks):
    nl, g, m, n = blocks.shape
    eye = jnp.eye(g, dtype=blocks.dtype)
    return (eye[None, :, None, :, None] * blocks[:, :, :, None, :]).reshape(nl, g * m, g * n)


def _prep_params(p, t_short):
    nl = p['w_in'].shape[0]
    row = lambda a: a.reshape(nl, 1, -1).astype(F32)
    w_in = p['w_in']
    big = dict(w_mix=jnp.concatenate([w_in[..., :OFF_S5], w_in[..., OFF_CV:OFF_GM], w_in[..., OFF_GM:OFF_GATE],
                                      w_in[..., OFF_S5:OFF_CV]], axis=-1).astype(MM_DTYPE),
               w_gate=w_in[..., OFF_GATE:].astype(MM_DTYPE),
               w_branch=p['w_branch'].astype(MM_DTYPE), w_out=p['w_out'].astype(MM_DTYPE),
               w_up=p['moe_w_up'].astype(MM_DTYPE),
               w_down=p['moe_w_down'].astype(MM_DTYPE).reshape(nl, N_EXPERTS * D_EXPERT, D_MODEL))
    zeros_lora = jnp.zeros((nl, RW_LW, C_BR), F32)
    rw = dict(mu=row(p['rw_mu']), w0=row(p['rw_w0']),
              w2=jnp.concatenate([p['rw_w2'], zeros_lora], axis=1).astype(MM_DTYPE), a0=row(p['rw_a0']),
              a2=jnp.concatenate([zeros_lora, p['rw_a2']], axis=1).astype(MM_DTYPE),
              g2=p['rw_g2'].astype(MM_DTYPE), kk=row(p['rw_kk']), ka=row(p['rw_ka']), rk=row(p['rw_rk']),
              gn_g=row(p['rw_gn_g']), gn_b=row(p['rw_gn_b']))
    lr, li = p['s5_lam_re'].astype(F32), p['s5_lam_im'].astype(F32)
    dt = jnp.exp(p['s5_log_dt'].astype(F32))[..., None]
    mag = jnp.exp(lr * dt)
    lb_re, lb_im = mag * jnp.cos(li * dt), mag * jnp.sin(li * dt)
    den = lr * lr + li * li
    q_re = ((lb_re - 1.0) * lr + lb_im * li) / den
    q_im = (lb_im * lr - (lb_re - 1.0) * li) / den
    br, bi = p['s5_b_re'].astype(F32), p['s5_b_im'].astype(F32)
    bb_re = q_re[..., None] * br - q_im[..., None] * bi
    bb_im = q_re[..., None] * bi + q_im[..., None] * br
    t23 = lambda a: jnp.swapaxes(a, 2, 3)
    bb = jnp.concatenate([_block_diag(t23(bb_re)), _block_diag(t23(bb_im))], axis=2).astype(MM_DTYPE)
    cc = jnp.concatenate([_block_diag(t23(p['s5_c_re'].astype(F32))), -_block_diag(t23(p['s5_c_im'].astype(F32)))],
                         axis=1).astype(MM_DTYPE)
    s5 = dict(lb_re=row(lb_re), lb_im=row(lb_im), bb=bb, cc=cc, d=row(p['s5_d']),
              glu_w=p['s5_glu_w'].astype(MM_DTYPE), glu_b=row(p['s5_glu_b']))
    cv = dict(w=p['cv_w'].astype(F32), b=row(p['cv_b']), ln_g=row(p['cv_ln_g']), ln_b=row(p['cv_ln_b']))
    causal = jnp.tril(jnp.ones((CHUNK, CHUNK), dtype=bool))
    wm = jnp.where(causal, p['gm_ws'], 0).astype(F32)
    bias = jnp.repeat(jnp.swapaxes(p['gm_bs'], 1, 2), GM_HEAD, axis=2).astype(F32)
    reps = CHUNK // t_short
    wm_short = jnp.einsum('rs,lhij->lhrisj', jnp.eye(reps, dtype=F32),
                          wm[:, :, :t_short, :t_short]).reshape(nl, GM_HEADS, CHUNK, CHUNK)
    gm = dict(ln_g=row(p['gm_ln_g']), ln_b=row(p['gm_ln_b']), wm=wm.astype(MM_DTYPE), bias=bias,
              wm_short=wm_short.astype(MM_DTYPE), bias_short=jnp.tile(bias[:, :t_short], (1, reps, 1)))
    mg = dict(ln1_g=row(p['ln1_g']), ln1_b=row(p['ln1_b']))
    pad = LANES - N_GROUPS - N_EXPERTS
    wg = jnp.concatenate([p['moe_wg1'], p['moe_wg2'], jnp.zeros((nl, D_MODEL, pad), F32)], axis=2).astype(F32)
    bg = row(jnp.concatenate([p['moe_bg1'], p['moe_bg2'], jnp.zeros((nl, pad), F32)], axis=1))
    moe = dict(wg=wg, bg=bg, ln2_g=row(p['ln2_g']), ln2_b=row(p['ln2_b']))
    return dict(big=big, rw=rw, s5=s5, cv=cv, gm=gm, mg=mg, moe=moe)


def _gmlp_group(z3d, pg, layer, *, is_prompt):
    bsz, t_len, n_cols = z3d.shape
    if is_prompt:
        y, = _gmlp(z3d, P_GM // (2 * C_BR), pg, pg['wm'], pg['bias'], layer, emit_v=False)
        return y, None
    y, v = _gmlp(z3d.reshape(1, bsz * t_len, n_cols), P_GM // (2 * C_BR), pg, pg['wm_short'], pg['bias_short'],
                 layer, emit_v=True)
    return y.reshape(bsz, t_len, C_BR), v.reshape(bsz, t_len, C_BR)


def _run_group(x, wkv0, shift0, s5r0, s5i0, conv0, pr, *, is_prompt):
    bsz, t_len, _ = x.shape
    big = pr['big']
    n = bsz * t_len
    x2d = x.reshape(n, D_MODEL)
    outs = []
    conv_shape = conv0.shape
    if not is_prompt:
        conv0 = conv0.reshape(bsz, DEPTH * CONV_HIST * C_BR)
    conv_buf = jnp.zeros_like(conv0)
    if not is_prompt:
        wkv_buf = jnp.zeros_like(wkv0)
        grp_rows = lambda a: a.reshape((bsz // RW_GROUP, RW_GROUP * a.shape[1]) + a.shape[2:])
    for l in range(DEPTH):
        z2d = _inproj(x2d, big['w_mix'], l)
        z3d = z2d.reshape(bsz, t_len, N_MIX)
        sh0 = shift0[:, l].reshape(bsz, 1, RW_IN)
        if is_prompt:
            y_rw, wkv1 = _rwkv(z3d, P_RW // RW_IN, sh0, wkv0[:, l], pr['rw'], l, t_tile=RW_TILE, chunk=RW_CHUNK,
                               t_valid=t_len, carry=True, prec=PREC)
        else:
            z_rw = grp_rows(jnp.pad(z3d[:, :, P_RW:P_RW + RW_IN], ((0, 0), (0, RW_PAD - t_len), (0, 0))))
            prev0 = grp_rows(jnp.pad(sh0, ((0, 0), (0, RW_PAD - 1), (0, 0))))
            y_rw, wkv_buf = _rwkv(z_rw, 0, prev0, wkv0, pr['rw'], l, t_tile=RW_GROUP * RW_PAD, chunk=RW_PAD,
                                  t_valid=t_len, carry=False, prec=PREC_SHORT, wkv_out=wkv_buf)
            y_rw = y_rw.reshape(bsz, RW_PAD, C_BR)[:, :t_len]
            wkv1 = None
        shift1 = z3d[:, t_len - 1, P_RW:P_RW + RW_IN].astype(F32)
        h0 = jnp.concatenate([s5r0[:, l].reshape(bsz, S5_N), s5i0[:, l].reshape(bsz, S5_N)], axis=1)
        if is_prompt:
            y_s5, h1 = _s5(z3d, P_S5 // C_BR, h0, pr['s5'], l, bsz=bsz, t_len=t_len, t_tile=S5_TILE)
        else:
            y_s5, h1 = _s5(z2d, P_S5 // C_BR, h0, pr['s5'], l, bsz=bsz, t_len=t_len, t_tile=t_len)
        s5r1 = h1[:, :S5_N].reshape(bsz, S5_GROUPS, S5_STATE)
        s5i1 = h1[:, S5_N:].reshape(bsz, S5_GROUPS, S5_STATE)
        if is_prompt:
            y_cv, conv_buf = _conv(z3d, P_CV // (2 * C_BR), conv0, conv_buf, l, pr['cv'], t_tile=CV_TILE)
        else:
            y_cv, conv_buf = _conv_short(z2d, P_CV // (2 * C_BR), conv0, conv_buf, l, pr['cv'], bsz=bsz,
                                         t_len=t_len)
        y_gm, v_gm = _gmlp_group(z3d, pr['gm'], l, is_prompt=is_prompt)
        ys = [y.reshape(n, C_BR) for y in (y_rw, y_s5, y_cv, y_gm)]
        x2d = _merge(x2d, ys, big, pr['mg'], l)
        x2d = _moe(x2d, big, pr['moe'], l)
        outs.append((wkv1, shift1, s5r1, s5i1, v_gm))
    stack = lambda i: None if outs[0][i] is None else jnp.stack([o[i] for o in outs], axis=1)
    wkv = stack(0) if is_prompt else wkv_buf
    return (x2d.reshape(bsz, t_len, D_MODEL),
            (wkv, stack(1), stack(2), stack(3), conv_buf.reshape(conv_shape), stack(4)))


def kernel(x_prompt, x_sample, state_rwkv_wkv, state_rwkv_shift, state_s5_re, state_s5_im, cache_conv,
           w_in, rw_mu, rw_w0, rw_w2, rw_a0, rw_a2, rw_g2, rw_kk, rw_ka, rw_rk, rw_gn_g, rw_gn_b,
           s5_lam_re, s5_lam_im, s5_log_dt, s5_b_re, s5_b_im, s5_c_re, s5_c_im, s5_d, s5_glu_w, s5_glu_b,
           cv_w, cv_b, cv_ln_g, cv_ln_b, gm_ln_g, gm_ln_b, gm_ws, gm_bs,
           w_branch, w_out, ln1_g, ln1_b,
           moe_wg1, moe_bg1, moe_wg2, moe_bg2, moe_w_up, moe_w_down, ln2_g, ln2_b):
    p = dict(w_in=w_in, rw_mu=rw_mu, rw_w0=rw_w0, rw_w2=rw_w2, rw_a0=rw_a0, rw_a2=rw_a2, rw_g2=rw_g2,
             rw_kk=rw_kk, rw_ka=rw_ka, rw_rk=rw_rk, rw_gn_g=rw_gn_g, rw_gn_b=rw_gn_b,
             s5_lam_re=s5_lam_re, s5_lam_im=s5_lam_im, s5_log_dt=s5_log_dt, s5_b_re=s5_b_re, s5_b_im=s5_b_im,
             s5_c_re=s5_c_re, s5_c_im=s5_c_im, s5_d=s5_d, s5_glu_w=s5_glu_w, s5_glu_b=s5_glu_b,
             cv_w=cv_w, cv_b=cv_b, cv_ln_g=cv_ln_g, cv_ln_b=cv_ln_b, gm_ln_g=gm_ln_g, gm_ln_b=gm_ln_b,
             gm_ws=gm_ws, gm_bs=gm_bs, w_branch=w_branch, w_out=w_out, ln1_g=ln1_g, ln1_b=ln1_b,
             moe_wg1=moe_wg1, moe_bg1=moe_bg1, moe_wg2=moe_wg2, moe_bg2=moe_bg2, moe_w_up=moe_w_up,
             moe_w_down=moe_w_down, ln2_g=ln2_g, ln2_b=ln2_b)
    pr = _prep_params(p, x_sample.shape[1])
    bp = x_prompt.shape[0]
    dt = x_prompt.dtype
    y_prompt, (p_wkv, p_shift, p_s5r, p_s5i, p_conv, _) = _run_group(
        x_prompt,
        jnp.zeros((bp, DEPTH, RW_HEADS, RW_HEAD, RW_HEAD), dt),
        jnp.zeros((bp, DEPTH, RW_IN), dt),
        jnp.zeros((bp, DEPTH, S5_GROUPS, S5_STATE), dt),
        jnp.zeros((bp, DEPTH, S5_GROUPS, S5_STATE), dt),
        jnp.zeros((bp, DEPTH, CONV_HIST, C_BR), dt),
        pr, is_prompt=True)
    y_sample, (s_wkv, s_shift, s_s5r, s_s5i, s_conv, s_gmv) = _run_group(
        x_sample, state_rwkv_wkv, state_rwkv_shift, state_s5_re, state_s5_im, cache_conv, pr, is_prompt=False)
    return (y_prompt, y_sample, p_wkv, p_shift, p_s5r, p_s5i, p_conv,
            s_wkv, s_shift, s_s5r, s_s5i, s_conv, s_gmv)
```
